```python
import math
import jax, jax.numpy as jnp
from jax import lax
import numpy as np

D_MODEL = 2048
BATCH = 4
SEQ = 2048
DEPTH = 4

GRID_W = 64
CTX_LEN = 256
EPS = 1e-6
F32 = jnp.float32

D_MIX = D_MODEL
D_CONV = D_MIX // 4
D_SSM = D_MIX // 4
D_ATTN = D_MIX - D_CONV - D_SSM
CONV_W = 3

MLA_HEADS = 8
QK_NOPE = 128
QK_ROPE = 64
V_HEAD = D_ATTN // MLA_HEADS
Q_RANK = 512
KV_RANK = 256
ROPE_BASE = 10000.0
MLA_SCALE = (QK_NOPE + QK_ROPE) ** -0.5
Q_BLOCK = 128

SSM_GROUP = 16
SSM_GROUPS = D_SSM // SSM_GROUP
SSM_STATE = 64
DT_MIN = 0.001
DT_MAX = 0.1

D_FF = -(-8 * D_MODEL // (3 * 256)) * 256

PROJ_SIZES = (D_CONV, D_CONV, D_CONV, Q_RANK, KV_RANK, QK_ROPE, D_SSM)
D_PROJ = sum(PROJ_SIZES)
PROJ_SPLITS = tuple(int(s) for s in np.cumsum(PROJ_SIZES)[:-1])

kernel_name = 'hybrid_conv_mla_s5_dit_block'


def rms_norm(x, g):
    xf = x.astype(F32)
    y = xf * lax.rsqrt(jnp.mean(xf * xf, axis=-1, keepdims=True) + EPS)
    return (y * g.astype(F32)).astype(x.dtype)


def modulate(h, shift, scale):
    return h * (1.0 + scale) + shift


def swiglu(h, w_gate, w_up, w_down):
    return (jax.nn.silu(h @ w_gate) * (h @ w_up)) @ w_down


def axial_rope_tables(rows):
    row = jnp.repeat(jnp.arange(rows, dtype=F32), GRID_W)
    col = jnp.tile(jnp.arange(GRID_W, dtype=F32), rows)
    n_freq = QK_ROPE // 4
    inv = ROPE_BASE ** (-jnp.arange(n_freq, dtype=F32) / n_freq)
    ang = jnp.stack([row[:, None] * inv, col[:, None] * inv], axis=1)
    return jnp.cos(ang), jnp.sin(ang)


def apply_axial_rope(x, cos, sin):
    xs = x.astype(F32).reshape(*x.shape[:-1], 2, 2, QK_ROPE // 4)
    x1, x2 = xs[..., 0, :], xs[..., 1, :]
    out = jnp.stack([x1 * cos - x2 * sin, x2 * cos + x1 * sin], axis=-2)
    return out.reshape(x.shape).astype(x.dtype)


def depthwise_conv(x, w):
    pad = CONV_W // 2
    return lax.conv_general_dilated(x, w[:, None, :], window_strides=(1,), padding=((pad, pad),),
                                    dimension_numbers=('NWC', 'WIO', 'NWC'),
                                    feature_group_count=x.shape[-1])


def short_conv_mixer(h, b_gate, c_gate, conv_w):
    return b_gate * depthwise_conv(c_gate * h, conv_w)


def mla_queries(c_q, q_norm, w_uq, rope):
    bn, L, _ = c_q.shape
    q = (rms_norm(c_q, q_norm) @ w_uq).reshape(bn, L, MLA_HEADS, QK_NOPE + QK_ROPE)
    q_nope, q_rope = q[..., :QK_NOPE], q[..., QK_NOPE:]
    if rope is not None:
        cos, sin = rope
        q_rope = apply_axial_rope(q_rope, cos[:, None], sin[:, None])
    return q_nope, q_rope


def mla_keys_values(c_kv, k_rope, kv_norm, w_ukv, rope):
    bn, L, _ = c_kv.shape
    kv = (rms_norm(c_kv, kv_norm) @ w_ukv).reshape(bn, L, MLA_HEADS, QK_NOPE + V_HEAD)
    if rope is not None:
        cos, sin = rope
        k_rope = apply_axial_rope(k_rope, cos, sin)
    return kv[..., :QK_NOPE], k_rope, kv[..., QK_NOPE:]


def mla_softmax_attend(qn, qr, kn, kr, v):
    s = jnp.einsum('bqhd,bkhd->bhqk', qn, kn) + jnp.einsum('bqhr,bkr->bhqk', qr, kr)
    p = jax.nn.softmax(s.astype(F32) * MLA_SCALE, axis=-1).astype(v.dtype)
    return jnp.einsum('bhqk,bkhd->bqhd', p, v)


def latent_attention(qn, qr, kn, kr, v):
    bn, L = qn.shape[:2]
    nb = L // Q_BLOCK

    def blocks(t):
        return jnp.moveaxis(t.reshape(bn, nb, Q_BLOCK, *t.shape[2:]), 1, 0)

    out = lax.map(lambda q: mla_softmax_attend(q[0], q[1], kn, kr, v), (blocks(qn), blocks(qr)))
    return jnp.moveaxis(out, 0, 1).reshape(bn, L, MLA_HEADS * V_HEAD)


def s5_discretise(a_re, a_im, log_dt, b_re, b_im):
    a = lax.complex(a_re.astype(F32), a_im.astype(F32))
    dt = jnp.exp(log_dt.astype(F32))[:, None]
    a_bar = jnp.exp(a * dt)
    b = lax.complex(b_re.astype(F32), b_im.astype(F32))
    b_bar = ((a_bar - 1.0) / a)[..., None] * b
    return a_bar, b_bar


def _linear_recurrence(e1, e2):
    a1, b1 = e1
    a2, b2 = e2
    return a1 * a2, a2 * b1 + b2


def s5_scan(u, a_bar, b_bar, h0, reverse):
    bu = jnp.einsum('blgc,gpc->blgp', u.astype(jnp.complex64), b_bar)
    if h0 is not None:
        first = -1 if reverse else 0
        bu = bu.at[:, first].add(a_bar * h0)
    a = jnp.broadcast_to(a_bar, bu.shape)
    _, h = lax.associative_scan(_linear_recurrence, (a, bu), axis=1, reverse=reverse)
    return h


def s5_readout(h, c_re, c_im):
    c = lax.complex(c_re.astype(F32), c_im.astype(F32))
    return jnp.einsum('blgp,gcp->blgc', h, c).real


def s5_output(u, h_f, h_b, c_re, c_im, ssm_d, w_glu, b_glu, dtype):
    bn, L = u.shape[:2]
    y = (s5_readout(h_f, c_re[0], c_im[0]) + s5_readout(h_b, c_re[1], c_im[1])).reshape(bn, L, D_SSM)
    y = y + ssm_d.astype(F32) * u.reshape(bn, L, D_SSM)
    g = jax.nn.gelu(y)
    return (g * jax.nn.sigmoid(g @ w_glu.astype(F32) + b_glu.astype(F32))).astype(dtype)


def merge_head_groups(y_conv, y_att, y_ssm, mix_norm, w_o):
    y = jnp.concatenate([
        rms_norm(y_conv, mix_norm[:D_CONV]),
        rms_norm(y_att, mix_norm[D_CONV:D_CONV + D_ATTN]),
        rms_norm(y_ssm, mix_norm[D_CONV + D_ATTN:]),
    ], axis=-1)
    return y @ w_o


def hybrid_mixer(hl, hc, rope, ctx_out, w_in, conv_w, q_norm, w_uq, kv_norm, w_ukv,
                 a_re, a_im, log_dt, b_re, b_im, c_re, c_im, ssm_d, w_glu, b_glu, mix_norm, w_o):
    bn, L, _ = hl.shape
    lc = hc.shape[1]
    hconv_l, bg_l, cg_l, cq_l, ckv_l, kr_l, u_l = jnp.split(hl @ w_in, PROJ_SPLITS, axis=-1)
    hconv_c, bg_c, cg_c, cq_c, ckv_c, kr_c, u_c = jnp.split(hc @ w_in, PROJ_SPLITS, axis=-1)

    conv_l = short_conv_mixer(hconv_l, bg_l, cg_l, conv_w)

    kn_c, krc, v_c = mla_keys_values(ckv_c, kr_c, kv_norm, w_ukv, None)
    kn_l, krl, v_l = mla_keys_values(ckv_l, kr_l, kv_norm, w_ukv, rope)
    qn_l, qr_l = mla_queries(cq_l, q_norm, w_uq, rope)
    att_l = latent_attention(qn_l, qr_l,
                             jnp.concatenate([kn_c, kn_l], axis=1),
                             jnp.concatenate([krc, krl], axis=1),
                             jnp.concatenate([v_c, v_l], axis=1))

    uc = u_c.astype(F32).reshape(bn, lc, SSM_GROUPS, SSM_GROUP)
    ul = u_l.astype(F32).reshape(bn, L, SSM_GROUPS, SSM_GROUP)
    a_f, bb_f = s5_discretise(a_re[0], a_im[0], log_dt[0], b_re[0], b_im[0])
    a_b, bb_b = s5_discretise(a_re[1], a_im[1], log_dt[1], b_re[1], b_im[1])
    hc_f = s5_scan(uc, a_f, bb_f, None, False)
    hc_b = s5_scan(uc, a_b, bb_b, None, True)
    hl_f = s5_scan(ul, a_f, bb_f, hc_f[:, -1], False)
    hl_b = s5_scan(ul, a_b, bb_b, hc_b[:, 0], True)
    ssm_l = s5_output(ul, hl_f, hl_b, c_re, c_im, ssm_d, w_glu, b_glu, hl.dtype)

    yl = merge_head_groups(conv_l, att_l, ssm_l, mix_norm, w_o)
    if not ctx_out:
        return yl, None

    conv_c = short_conv_mixer(hconv_c, bg_c, cg_c, conv_w)
    qn_c, qr_c = mla_queries(cq_c, q_norm, w_uq, None)
    att_c = mla_softmax_attend(qn_c, qr_c, kn_c, krc, v_c).reshape(bn, lc, MLA_HEADS * V_HEAD)
    ssm_c = s5_output(uc, hc_f, hc_b, c_re, c_im, ssm_d, w_glu, b_glu, hc.dtype)
    yc = merge_head_groups(conv_c, att_c, ssm_c, mix_norm, w_o)
    return yl, yc


def setup_inputs(seed: int = 0) -> dict:
    key = jax.random.key(seed)
    ks = iter(jax.random.split(key, 40))

    def nrm(shape, scale):
        return jax.random.normal(next(ks), shape, F32) * scale

    G, P, Hg = SSM_GROUPS, SSM_STATE, SSM_GROUP
    n = jnp.arange(P, dtype=F32)
    return {
        'x': nrm((BATCH, SEQ, D_MODEL), 1.0),
        'c': nrm((BATCH, D_MODEL), 1.0),
        'ctx': nrm((BATCH, CTX_LEN, D_MODEL), 1.0),
        'c_ctx': nrm((D_MODEL,), 1.0),
        'w_ada': nrm((DEPTH, D_MODEL, 6 * D_MODEL), 0.5 * D_MODEL ** -0.5),
        'b_ada': nrm((DEPTH, 6 * D_MODEL), 0.02),
        'norm1_g': 1.0 + nrm((DEPTH, D_MODEL), 0.02),
        'norm2_g': 1.0 + nrm((DEPTH, D_MODEL), 0.02),
        'w_in': nrm((DEPTH, D_MODEL, D_PROJ), D_MODEL ** -0.5),
        'conv_w': nrm((DEPTH, CONV_W, D_CONV), CONV_W ** -0.5),
        'mla_q_norm': 1.0 + nrm((DEPTH, Q_RANK), 0.02),
        'w_uq': nrm((DEPTH, Q_RANK, MLA_HEADS * (QK_NOPE + QK_ROPE)), Q_RANK ** -0.5),
        'mla_kv_norm': 1.0 + nrm((DEPTH, KV_RANK), 0.02),
        'w_ukv': nrm((DEPTH, KV_RANK, MLA_HEADS * (QK_NOPE + V_HEAD)), KV_RANK ** -0.5),
        'ssm_a_re': -0.5 + nrm((DEPTH, 2, G, P), 0.01),
        'ssm_a_im': math.pi * n + nrm((DEPTH, 2, G, P), 0.01),
        'ssm_log_dt': jax.random.uniform(next(ks), (DEPTH, 2, G), F32, math.log(DT_MIN), math.log(DT_MAX)),
        'ssm_b_re': nrm((DEPTH, 2, G, P, Hg), (2 * Hg) ** -0.5),
        'ssm_b_im': nrm((DEPTH, 2, G, P, Hg), (2 * Hg) ** -0.5),
        'ssm_c_re': nrm((DEPTH, 2, G, Hg, P), (2 * P) ** -0.5),
        'ssm_c_im': nrm((DEPTH, 2, G, Hg, P), (2 * P) ** -0.5),
        'ssm_d': nrm((DEPTH, D_SSM), 1.0),
        'w_glu': nrm((DEPTH, D_SSM, D_SSM), D_SSM ** -0.5),
        'b_glu': nrm((DEPTH, D_SSM), 0.02),
        'mix_norm': 1.0 + nrm((DEPTH, D_MIX), 0.02),
        'w_o': nrm((DEPTH, D_MIX, D_MODEL), D_MIX ** -0.5),
        'w_gate': nrm((DEPTH, D_MODEL, D_FF), D_MODEL ** -0.5),
        'w_up': nrm((DEPTH, D_MODEL, D_FF), D_MODEL ** -0.5),
        'w_down': nrm((DEPTH, D_FF, D_MODEL), D_FF ** -0.5),
        'final_norm': 1.0 + nrm((D_MODEL,), 0.02),
    }


def reference(x, c, ctx, c_ctx, w_ada, b_ada, norm1_g, norm2_g, w_in, conv_w, mla_q_norm, w_uq,
              mla_kv_norm, w_ukv, ssm_a_re, ssm_a_im, ssm_log_dt, ssm_b_re, ssm_b_im, ssm_c_re,
              ssm_c_im, ssm_d, w_glu, b_glu, mix_norm, w_o, w_gate, w_up, w_down, final_norm):
    L = x.shape[1]
    ROWS = L // GRID_W
    rope = axial_rope_tables(ROWS)
    silu_c = jax.nn.silu(c)
    silu_cc = jax.nn.silu(c_ctx)
    xl, xc = x, ctx
    for i in range(DEPTH):
        ctx_out = i < DEPTH - 1
        sh1, sc1, g1, sh2, sc2, g2 = [m[:, None] for m in jnp.split(silu_c @ w_ada[i] + b_ada[i], 6, axis=-1)]
        csh1, csc1, cg1, csh2, csc2, cg2 = jnp.split(silu_cc @ w_ada[i] + b_ada[i], 6, axis=-1)

        hl = modulate(rms_norm(xl, norm1_g[i]), sh1, sc1)
        hc = modulate(rms_norm(xc, norm1_g[i]), csh1, csc1)
        yl, yc = hybrid_mixer(hl, hc, rope, ctx_out, w_in[i], conv_w[i], mla_q_norm[i], w_uq[i],
                              mla_kv_norm[i], w_ukv[i], ssm_a_re[i], ssm_a_im[i], ssm_log_dt[i],
                              ssm_b_re[i], ssm_b_im[i], ssm_c_re[i], ssm_c_im[i], ssm_d[i],
                              w_glu[i], b_glu[i], mix_norm[i], w_o[i])
        xl = xl + g1 * yl
        hl2 = modulate(rms_norm(xl, norm2_g[i]), sh2, sc2)
        xl = xl + g2 * swiglu(hl2, w_gate[i], w_up[i], w_down[i])

        if ctx_out:
            xc = xc + cg1 * yc
            hc2 = modulate(rms_norm(xc, norm2_g[i]), csh2, csc2)
            xc = xc + cg2 * swiglu(hc2, w_gate[i], w_up[i], w_down[i])
    return rms_norm(xl, final_norm)
```

```python
import functools
import math

import jax
import jax.numpy as jnp
from jax import lax
from jax.experimental import pallas as pl
from jax.experimental.pallas import tpu as pltpu

F32 = jnp.float32
BF16 = jnp.bfloat16

EPS = 1e-6
GRID_W = 64
CONV_W = 3
D_CONV = 512
D_SSM = 512
D_ATTN = 1024
MLA_HEADS = 8
QK_NOPE = 128
QK_ROPE = 64
V_HEAD = 128
Q_RANK = 512
KV_RANK = 256
ROPE_BASE = 10000.0
MLA_SCALE = (QK_NOPE + QK_ROPE) ** -0.5
SSM_GROUP = 16
SSM_GROUPS = 32
SSM_STATE = 64
CHUNK = 16

QK_PAD = 256
LANE = 128
TM = 1024
MIB = 1024 * 1024

COL_CONV = 0
COL_CQ = 3 * D_CONV
COL_U = COL_CQ + Q_RANK
COL_CKV = COL_U + D_SSM
COL_KR = COL_CKV + KV_RANK
N_PROJ = 3072


def _cparams(sem, vmem_mib):
    return pltpu.CompilerParams(dimension_semantics=sem, vmem_limit_bytes=vmem_mib * MIB)


def _rms(x, g):
    return x * lax.rsqrt(jnp.mean(x * x, axis=-1, keepdims=True) + EPS) * g


def _ada_kernel(c_ref, w_ref, b_ref, o_ref):
    cv = c_ref[...]
    s = (cv * jax.nn.sigmoid(cv)).astype(BF16)
    o_ref[...] = jnp.dot(s, w_ref[...].astype(BF16), preferred_element_type=F32) + b_ref[...]


def ada_modulation(cvec, w_ada, b_ada, tn=1024):
    depth, d, n = w_ada.shape
    rows = cvec.shape[0]
    return pl.pallas_call(
        _ada_kernel,
        grid=(depth, n // tn),
        in_specs=[
            pl.BlockSpec((rows, d), lambda l, j: (0, 0)),
            pl.BlockSpec((None, d, tn), lambda l, j: (l, 0, j)),
            pl.BlockSpec((None, 1, tn), lambda l, j: (l, 0, j)),
        ],
        out_specs=pl.BlockSpec((None, rows, tn), lambda l, j: (l, 0, j)),
        out_shape=jax.ShapeDtypeStruct((depth, rows, n), F32),
        compiler_params=_cparams(("arbitrary", "arbitrary"), 40),
    )(cvec, w_ada, b_ada.reshape(depth, 1, n))


def _nm_mm_kernel(x_ref, g_ref, sh_ref, sc_ref, *rest, swiglu, row_chunk):
    if swiglu:
        wg_ref, wu_ref, o_ref, a_scr = rest
    else:
        w_ref, o_ref, a_scr = rest
    tm = x_ref.shape[0]

    @pl.when(pl.program_id(1) == 0)
    def _():
        g = g_ref[...]
        sh = sh_ref[...]
        sc1 = 1.0 + sc_ref[...]
        for r in range(0, tm, row_chunk):
            x = x_ref[r:r + row_chunk, :]
            a_scr[r:r + row_chunk, :] = (_rms(x, g) * sc1 + sh).astype(BF16)

    a = a_scr[...]
    if swiglu:
        gt = jnp.dot(a, wg_ref[...], preferred_element_type=F32)
        up = jnp.dot(a, wu_ref[...], preferred_element_type=F32)
        o_ref[...] = (gt * jax.nn.sigmoid(gt) * up).astype(o_ref.dtype)
    else:
        o_ref[...] = jnp.dot(a, w_ref[...], preferred_element_type=F32).astype(o_ref.dtype)


def norm_mod_matmul(x, gain, mod4, layer, col_shift, col_scale, ws, *, m_tiles, tiles_per_seq,
                    n_batch, tn, out_dtype, tm=TM):
    m, d = x.shape
    n = ws[0].shape[1]
    swiglu = len(ws) == 2

    def mod_spec(col):
        return pl.BlockSpec((None, None, 1, d),
                            lambda i, j: (layer, jnp.minimum(i // tiles_per_seq, n_batch), 0, col))

    in_specs = [
        pl.BlockSpec((tm, d), lambda i, j: (i, 0)),
        pl.BlockSpec((1, d), lambda i, j: (0, 0)),
        mod_spec(col_shift),
        mod_spec(col_scale),
    ] + [pl.BlockSpec((d, tn), lambda i, j: (0, j)) for _ in ws]
    return pl.pallas_call(
        functools.partial(_nm_mm_kernel, swiglu=swiglu, row_chunk=min(256, tm)),
        grid=(m_tiles, n // tn),
        in_specs=in_specs,
        out_specs=pl.BlockSpec((tm, tn), lambda i, j: (i, j)),
        out_shape=jax.ShapeDtypeStruct((m, n), out_dtype),
        scratch_shapes=[pltpu.VMEM((tm, d), BF16)],
        compiler_params=_cparams(("arbitrary", "arbitrary"), 48),
    )(x, gain.reshape(1, d), mod4, mod4, *ws)


def _mm_resid_kernel(*refs, splits):
    n = len(splits)
    a_refs = refs[:n]
    w_ref, x_ref, gate_ref, o_ref = refs[n:]
    acc = None
    off = 0
    for a_ref, k in zip(a_refs, splits):
        part = jnp.dot(a_ref[...], w_ref[off:off + k, :], preferred_element_type=F32)
        acc = part if acc is None else acc + part
        off += k
    o_ref[...] = x_ref[...] + gate_ref[...] * acc


def matmul_gated_residual(a_parts, w, x, mod4, layer, col_gate, *, m_tiles, tiles_per_seq,
                          n_batch, tn, tm=TM):
    m, d = x.shape
    k_total, n = w.shape
    splits = tuple(a.shape[1] for a in a_parts)
    assert sum(splits) == k_total and n == d
    gate_blocks = d // tn
    in_specs = [pl.BlockSpec((tm, k), lambda i, j: (i, 0)) for k in splits] + [
        pl.BlockSpec((k_total, tn), lambda i, j: (0, j)),
        pl.BlockSpec((tm, tn), lambda i, j: (i, j)),
        pl.BlockSpec((None, None, 1, tn),
                     lambda i, j: (layer, jnp.minimum(i // tiles_per_seq, n_batch), 0,
                                   col_gate * gate_blocks + j)),
    ]
    return pl.pallas_call(
        functools.partial(_mm_resid_kernel, splits=splits),
        grid=(m_tiles, n // tn),
        in_specs=in_specs,
        out_specs=pl.BlockSpec((tm, tn), lambda i, j: (i, j)),
        out_shape=jax.ShapeDtypeStruct((m, n), F32),
        compiler_params=_cparams(("arbitrary", "arbitrary"), 52),
    )(*a_parts, w, x, mod4)


def _conv_kernel(h_ref, bg_ref, cg_ref, w_ref, gn_ref, o_ref, z_scr, *, row_chunk):
    seq, ch = h_ref.shape
    zeros8 = jnp.zeros((8, ch), F32)
    z_scr[0:8, :] = zeros8
    z_scr[seq + 8:seq + 16, :] = zeros8
    for r in range(0, seq, row_chunk):
        z_scr[8 + r:8 + r + row_chunk, :] = cg_ref[r:r + row_chunk, :] * h_ref[r:r + row_chunk, :]
    w0 = w_ref[0:1, :]
    w1 = w_ref[1:2, :]
    w2 = w_ref[2:3, :]
    gn = gn_ref[...]
    for r in range(0, seq, row_chunk):
        zp = z_scr[7 + r:7 + r + row_chunk, :]
        zc = z_scr[8 + r:8 + r + row_chunk, :]
        zn = z_scr[9 + r:9 + r + row_chunk, :]
        y = bg_ref[r:r + row_chunk, :] * (w0 * zp + w1 * zc + w2 * zn)
        o_ref[r:r + row_chunk, :] = _rms(y, gn).astype(o_ref.dtype)


def conv_mixer(proj, conv_w, gn, *, seq, n_seq, row_block0, rows_out):
    ch = D_CONV
    in_specs = [pl.BlockSpec((seq, ch), functools.partial(lambda s, c: (row_block0 + s, c), c=c))
                for c in range(3)]
    in_specs += [pl.BlockSpec((CONV_W, ch), lambda s: (0, 0)), pl.BlockSpec((1, ch), lambda s: (0, 0))]
    return pl.pallas_call(
        functools.partial(_conv_kernel, row_chunk=min(256, seq)),
        grid=(n_seq,),
        in_specs=in_specs,
        out_specs=pl.BlockSpec((seq, ch), lambda s: (s, 0)),
        out_shape=jax.ShapeDtypeStruct((rows_out, ch), BF16),
        scratch_shapes=[pltpu.VMEM((seq + 16, ch), F32)],
        compiler_params=_cparams(("arbitrary",), 48),
    )(proj, proj, proj, conv_w, gn.reshape(1, ch))


def _q_proj_kernel(cq_ref, g_ref, w_ref, cos_ref, sin_ref, o_ref):
    a = _rms(cq_ref[...], g_ref[...]).astype(BF16)
    q = jnp.dot(a, w_ref[...], preferred_element_type=F32)
    cos = cos_ref[...]
    sin = sin_ref[...]
    for h in range(MLA_HEADS):
        c0 = h * QK_PAD
        o_ref[:, c0:c0 + QK_NOPE] = (q[:, c0:c0 + QK_NOPE] * MLA_SCALE).astype(o_ref.dtype)
        blk = q[:, c0 + QK_NOPE:c0 + QK_PAD]
        rot = blk * cos + pltpu.roll(blk, QK_ROPE, axis=1) * sin
        o_ref[:, c0 + QK_NOPE:c0 + QK_PAD] = (rot * MLA_SCALE).astype(o_ref.dtype)


def q_projection(proj, q_norm, w_q, cos_t, sin_t, *, rows, tm, lat_tiles, tiles_per_seq):
    tab_map = lambda i: (jnp.where(i < lat_tiles, i % tiles_per_seq, tiles_per_seq), 0)
    n = MLA_HEADS * QK_PAD
    return pl.pallas_call(
        _q_proj_kernel,
        grid=(rows // tm,),
        in_specs=[
            pl.BlockSpec((tm, Q_RANK), lambda i: (i, COL_CQ // Q_RANK)),
            pl.BlockSpec((1, Q_RANK), lambda i: (0, 0)),
            pl.BlockSpec((Q_RANK, n), lambda i: (0, 0)),
            pl.BlockSpec((tm, LANE), tab_map),
            pl.BlockSpec((tm, LANE), tab_map),
        ],
        out_specs=pl.BlockSpec((tm, n), lambda i: (i, 0)),
        out_shape=jax.ShapeDtypeStruct((rows, n), BF16),
        compiler_params=_cparams(("arbitrary",), 40),
    )(proj, q_norm.reshape(1, Q_RANK), w_q, cos_t, sin_t)


def _kv_proj_kernel(ckv_ref, kr_ref, g_ref, w_ref, cos_ref, sin_ref, k_ref, v_ref):
    a = _rms(ckv_ref[...], g_ref[...]).astype(BF16)
    kv = jnp.dot(a, w_ref[...], preferred_element_type=F32)
    blk = kr_ref[...]
    rot = (blk * cos_ref[...] + pltpu.roll(blk, QK_ROPE, axis=1) * sin_ref[...]).astype(k_ref.dtype)
    nk = MLA_HEADS * QK_NOPE
    for h in range(MLA_HEADS):
        c0 = h * QK_PAD
        k_ref[:, c0:c0 + QK_NOPE] = kv[:, h * QK_NOPE:(h + 1) * QK_NOPE].astype(k_ref.dtype)
        k_ref[:, c0 + QK_NOPE:c0 + QK_PAD] = rot
    v_ref[...] = kv[:, nk:].astype(v_ref.dtype)


def kv_projection(proj, kv_norm, w_kv, cos_t, sin_t, *, rows, tm, lat_tiles, tiles_per_seq):
    tab_map = lambda i: (jnp.where(i < lat_tiles, i % tiles_per_seq, tiles_per_seq), 0)
    nk = MLA_HEADS * QK_PAD
    nv = MLA_HEADS * V_HEAD
    return pl.pallas_call(
        _kv_proj_kernel,
        grid=(rows // tm,),
        in_specs=[
            pl.BlockSpec((tm, KV_RANK), lambda i: (i, COL_CKV // KV_RANK)),
            pl.BlockSpec((tm, LANE), lambda i: (i, COL_KR // LANE)),
            pl.BlockSpec((1, KV_RANK), lambda i: (0, 0)),
            pl.BlockSpec((KV_RANK, MLA_HEADS * (QK_NOPE + V_HEAD)), lambda i: (0, 0)),
            pl.BlockSpec((tm, LANE), tab_map),
            pl.BlockSpec((tm, LANE), tab_map),
        ],
        out_specs=[pl.BlockSpec((tm, nk), lambda i: (i, 0)), pl.BlockSpec((tm, nv), lambda i: (i, 0))],
        out_shape=[jax.ShapeDtypeStruct((rows, nk), BF16), jax.ShapeDtypeStruct((rows, nv), BF16)],
        compiler_params=_cparams(("arbitrary",), 40),
    )(proj, proj, kv_norm.reshape(1, KV_RANK), w_kv, cos_t, sin_t)


_NT = (((1,), (1,)), ((), ()))


def _attn_kernel(q_ref, kl_ref, kc_ref, vl_ref, vc_ref, gn_ref, o_ref, acc_scr, *, lat_tiles):
    t = pl.program_id(1)

    def heads(with_latent):
        for h in range(MLA_HEADS):
            q = q_ref[:, h * QK_PAD:(h + 1) * QK_PAD]
            sc = lax.dot_general(q, kc_ref[:, h * QK_PAD:(h + 1) * QK_PAD], _NT,
                                 preferred_element_type=F32)
            m = jnp.max(sc, axis=-1, keepdims=True)
            if with_latent:
                sl = lax.dot_general(q, kl_ref[:, h * QK_PAD:(h + 1) * QK_PAD], _NT,
                                     preferred_element_type=F32)
                m = jnp.maximum(m, jnp.max(sl, axis=-1, keepdims=True))
                pl_ = jnp.exp(sl - m)
            pc = jnp.exp(sc - m)
            den = jnp.sum(pc, axis=-1, keepdims=True)
            o = jnp.dot(pc.astype(BF16), vc_ref[:, h * V_HEAD:(h + 1) * V_HEAD],
                        preferred_element_type=F32)
            if with_latent:
                den = den + jnp.sum(pl_, axis=-1, keepdims=True)
                o = o + jnp.dot(pl_.astype(BF16), vl_ref[:, h * V_HEAD:(h + 1) * V_HEAD],
                                preferred_element_type=F32)
            acc_scr[:, h * V_HEAD:(h + 1) * V_HEAD] = o / den

    @pl.when(t < lat_tiles)
    def _():
        heads(True)

    @pl.when(t >= lat_tiles)
    def _():
        heads(False)

    o_ref[...] = _rms(acc_scr[...], gn_ref[...]).astype(o_ref.dtype)


def latent_attention(q, k, v, gn, *, n_batch, seq, ctx_len, tq, ctx_queries, rows_out):
    lat_tiles = seq // tq
    assert ctx_len == tq
    q_tiles = lat_tiles + (1 if ctx_queries else 0)
    ctx_blk0 = n_batch * seq // ctx_len
    nq = MLA_HEADS * QK_PAD
    nv = MLA_HEADS * V_HEAD
    qmap = lambda b, t: (jnp.where(t < lat_tiles, b * lat_tiles + t, ctx_blk0 + b), 0)
    return pl.pallas_call(
        functools.partial(_attn_kernel, lat_tiles=lat_tiles),
        grid=(n_batch, q_tiles),
        in_specs=[
            pl.BlockSpec((tq, nq), qmap),
            pl.BlockSpec((seq, nq), lambda b, t: (b, 0)),
            pl.BlockSpec((ctx_len, nq), lambda b, t: (ctx_blk0 + b, 0)),
            pl.BlockSpec((seq, nv), lambda b, t: (b, 0)),
            pl.BlockSpec((ctx_len, nv), lambda b, t: (ctx_blk0 + b, 0)),
            pl.BlockSpec((1, nv), lambda b, t: (0, 0)),
        ],
        out_specs=pl.BlockSpec((tq, nv), qmap),
        out_shape=jax.ShapeDtypeStruct((rows_out, nv), BF16),
        scratch_shapes=[pltpu.VMEM((tq, nv), F32)],
        compiler_params=_cparams(("arbitrary", "arbitrary"), 56),
    )(q, k, k, v, v, gn.reshape(1, nv))


def _s5_disc_kernel(are_ref, aim_ref, ldt_ref, abr_ref, abi_ref, cfr_ref, cfi_ref):
    ar = are_ref[...]
    ai = aim_ref[...]
    dt = jnp.exp(ldt_ref[...])
    mag = jnp.exp(ar * dt)
    th = ai * dt
    br = mag * jnp.cos(th)
    bi = mag * jnp.sin(th)
    nr = br - 1.0
    den = ar * ar + ai * ai
    abr_ref[...] = br
    abi_ref[...] = bi
    cfr_ref[...] = (nr * ar + bi * ai) / den
    cfi_ref[...] = (bi * ar - nr * ai) / den


def s5_discretise(a_re, a_im, log_dt):
    shp = a_re.shape
    rows = shp[0] * shp[1] * shp[2]
    flat = lambda t: t.reshape(rows, shp[3])
    ldt = jnp.broadcast_to(log_dt[..., None], shp)
    outs = pl.pallas_call(
        _s5_disc_kernel,
        out_shape=[jax.ShapeDtypeStruct((rows, shp[3]), F32)] * 4,
    )(flat(a_re), flat(a_im), flat(ldt))
    return [o.reshape(shp) for o in outs]


def _cmul(xr, xi, yr, yi):
    return xr * yr - xi * yi, xr * yi + xi * yr


def _s5_mats_kernel(abc_re, abc_im, cex_re, cex_im, afb_re, afb_im, cfb_re, cfb_im, bfb_re, bfb_im,
                    cpp_re, cpp_im, bx_ref, by_ref, ma_ref, mi_ref, mo_ref, a16_ref):
    tw = CHUNK * SSM_GROUP
    p = SSM_STATE
    lane = lax.broadcasted_iota(jnp.int32, (p, tw), 1)
    tblk = lane // SSM_GROUP
    lane16 = lax.broadcasted_iota(jnp.int32, (SSM_GROUP, tw), 1)
    lane128 = lax.broadcasted_iota(jnp.int32, (1, 2 * p), 1)
    fwd_lanes = lane128 < p
    sgn = jnp.where(lax.broadcasted_iota(jnp.int32, (SSM_GROUP, 2 * p), 1) < p, 1.0, -1.0)
    zeros = jnp.zeros((p, tw), F32)
    mi_blocks = []
    for d in range(2):
        ar = jnp.broadcast_to(abc_re[d], (p, tw))
        ai = jnp.broadcast_to(abc_im[d], (p, tw))
        sel_r = tblk if d == 0 else (CHUNK - 1) - tblk
        sel_e = tblk + 1 if d == 0 else CHUNK - tblk
        pr, pi = jnp.ones((p, tw), F32), zeros
        e_re = e_im = r_re = r_im = zeros
        for k in range(CHUNK + 1):
            if k < CHUNK:
                r_re = jnp.where(sel_r == k, pr, r_re)
                r_im = jnp.where(sel_r == k, pi, r_im)
            if k >= 1:
                e_re = jnp.where(sel_e == k, pr, e_re)
                e_im = jnp.where(sel_e == k, pi, e_im)
            if k < CHUNK:
                pr, pi = _cmul(pr, pi, ar, ai)
        cr = cex_re[d]
        ci = cex_im[d]
        w_re, w_im = _cmul(cr, ci, e_re, e_im)
        base = d * 4 * p
        if d == 0:
            mo_ref[base:base + p, :] = w_re
            mo_ref[base + p:base + 2 * p, :] = zeros
            mo_ref[base + 2 * p:base + 3 * p, :] = -w_im
            mo_ref[base + 3 * p:base + 4 * p, :] = zeros
        else:
            mo_ref[base:base + p, :] = zeros
            mo_ref[base + p:base + 2 * p, :] = w_re
            mo_ref[base + 2 * p:base + 3 * p, :] = zeros
            mo_ref[base + 3 * p:base + 4 * p, :] = -w_im
        rr_re, rr_im = _cmul(cr, ci, r_re, r_im)
        stacked = jnp.concatenate([rr_re, rr_im], axis=0)
        lm = sgn * cpp_re[d] * bx_ref[d] - cpp_im[d] * by_ref[d]
        kall = jnp.dot(lm, stacked, preferred_element_type=F32, precision=lax.Precision.HIGHEST)
        for s in range(CHUNK):
            if d == 0:
                shift = SSM_GROUP * s
                keep = lane16 >= SSM_GROUP * s
            else:
                shift = (SSM_GROUP * (s + 1)) % tw
                keep = lane16 < SSM_GROUP * (s + 1)
            rolled = pltpu.roll(kall, shift, axis=1) if shift else kall
            blk = jnp.where(keep, rolled, 0.0)
            if d == 0:
                mi_blocks.append(blk)
            else:
                mi_ref[SSM_GROUP * s:SSM_GROUP * (s + 1), :] = mi_blocks[s] + blk

    a_re = afb_re[...]
    a_im = afb_im[...]
    cf_re = cfb_re[...]
    cf_im = cfb_im[...]
    b_re = bfb_re[...]
    b_im = bfb_im[...]
    powers = []
    qr, qi = jnp.ones((1, 2 * p), F32), jnp.zeros((1, 2 * p), F32)
    for k in range(CHUNK + 1):
        powers.append((qr, qi))
        if k < CHUNK:
            qr, qi = _cmul(qr, qi, a_re, a_im)
    for s in range(CHUNK):
        g_re = jnp.where(fwd_lanes, powers[CHUNK - 1 - s][0], powers[s][0])
        g_im = jnp.where(fwd_lanes, powers[CHUNK - 1 - s][1], powers[s][1])
        g_re, g_im = _cmul(g_re, g_im, cf_re, cf_im)
        rows = slice(SSM_GROUP * s, SSM_GROUP * (s + 1))
        ma_ref[rows, 0:2 * p] = g_re * b_re - g_im * b_im
        ma_ref[rows, 2 * p:4 * p] = g_re * b_im + g_im * b_re
    a16_ref[0:1, :] = powers[CHUNK][0]
    a16_ref[1:2, :] = powers[CHUNK][1]


def s5_matrices(abar_re, abar_im, coef_re, coef_im, b_re, b_im, c_re, c_im):
    depth, _, g, p = abar_re.shape
    hg = b_re.shape[-1]
    tw = CHUNK * hg
    t0213 = lambda t: t.transpose(0, 2, 1, 3)
    abc = [t0213(t)[..., None] for t in (abar_re, abar_im)]
    cex = [jnp.tile(t.transpose(0, 2, 1, 4, 3), (1, 1, 1, 1, CHUNK)) for t in (c_re, c_im)]
    fb = lambda t: t0213(t).reshape(depth, g, 1, 2 * p)
    afb = [fb(t) for t in (abar_re, abar_im)]
    cfb = [fb(t) for t in (coef_re, coef_im)]
    bfb = [t.transpose(0, 2, 4, 1, 3).reshape(depth, g, hg, 2 * p) for t in (b_re, b_im)]
    cpp = [t0213(jnp.concatenate([t, t], axis=-1))[:, :, :, None, :] for t in (coef_re, coef_im)]
    bt_re = b_re.transpose(0, 2, 1, 4, 3)
    bt_im = b_im.transpose(0, 2, 1, 4, 3)
    bx = jnp.concatenate([bt_re, bt_im], axis=-1)
    by = jnp.concatenate([bt_im, bt_re], axis=-1)
    ins = [*abc, *cex, *afb, *cfb, *bfb, *cpp, bx, by]

    def spec(t):
        blk = (None, None) + t.shape[2:]
        nz = len(t.shape) - 2
        return pl.BlockSpec(blk, lambda l, gi: (l, gi) + (0,) * nz)

    def ospec(r, c):
        return pl.BlockSpec((None, None, r, c), lambda l, gi: (l, gi, 0, 0))

    return pl.pallas_call(
        _s5_mats_kernel,
        grid=(depth, g),
        in_specs=[spec(t) for t in ins],
        out_specs=[ospec(tw, 4 * p), ospec(tw, tw), ospec(8 * p, tw), ospec(2, 2 * p)],
        out_shape=[
            jax.ShapeDtypeStruct((depth, g, tw, 4 * p), F32),
            jax.ShapeDtypeStruct((depth, g, tw, tw), F32),
            jax.ShapeDtypeStruct((depth, g, 8 * p, tw), F32),
            jax.ShapeDtypeStruct((depth, g, 2, 2 * p), F32),
        ],
        compiler_params=_cparams(("arbitrary", "arbitrary"), 32),
    )(*ins)


def _s5_main_kernel(u_ref, ma_ref, mi_ref, mo_ref, a_ref, y_ref, x_scr, spf_scr, spb_scr, *,
                    order_f, order_b, nb):
    hp = lax.Precision.HIGHEST
    p2 = 2 * SSM_STATE
    u = u_ref[...]
    x_scr[...] = jnp.dot(u, ma_ref[...], preferred_element_type=F32, precision=hp)
    a_re = a_ref[0:1, :]
    a_im = a_ref[1:2, :]
    fwd_lanes = lax.broadcasted_iota(jnp.int32, (nb, p2), 1) < SSM_STATE
    s_re = jnp.zeros((nb, p2), F32)
    s_im = jnp.zeros((nb, p2), F32)
    for jf, jb in zip(order_f, order_b):
        rf = slice(nb * jf, nb * jf + nb)
        rb = slice(nb * jb, nb * jb + nb)
        spf_scr[rf, 0:p2] = s_re
        spf_scr[rf, p2:2 * p2] = s_im
        spb_scr[rb, 0:p2] = s_re
        spb_scr[rb, p2:2 * p2] = s_im
        xr = jnp.where(fwd_lanes, x_scr[rf, 0:p2], x_scr[rb, 0:p2])
        xi = jnp.where(fwd_lanes, x_scr[rf, p2:2 * p2], x_scr[rb, p2:2 * p2])
        s_re, s_im = a_re * s_re - a_im * s_im + xr, a_re * s_im + a_im * s_re + xi
    y = jnp.dot(u, mi_ref[...], preferred_element_type=F32, precision=hp)
    y = y + jnp.dot(spf_scr[...], mo_ref[0:2 * p2, :], preferred_element_type=F32, precision=hp)
    y = y + jnp.dot(spb_scr[...], mo_ref[2 * p2:4 * p2, :], preferred_element_type=F32, precision=hp)
    y_ref[...] = y


def s5_chunked(u_g, ma, mi, mo, a16, layer, *, lat_chunks, ctx_chunks, nb):
    g, rows, tw = u_g.shape
    j_tot = lat_chunks + ctx_chunks
    order_f = tuple(range(lat_chunks, j_tot)) + tuple(range(lat_chunks))
    order_b = tuple(range(j_tot - 1, lat_chunks - 1, -1)) + tuple(range(lat_chunks - 1, -1, -1))
    p4 = 4 * SSM_STATE

    def wspec(r, c):
        return pl.BlockSpec((None, None, r, c), lambda gi: (layer, gi, 0, 0))

    return pl.pallas_call(
        functools.partial(_s5_main_kernel, order_f=order_f, order_b=order_b, nb=nb),
        grid=(g,),
        in_specs=[
            pl.BlockSpec((None, rows, tw), lambda gi: (gi, 0, 0)),
            wspec(tw, p4), wspec(tw, tw), wspec(2 * p4, tw), wspec(2, 2 * SSM_STATE),
        ],
        out_specs=pl.BlockSpec((None, rows, tw), lambda gi: (gi, 0, 0)),
        out_shape=jax.ShapeDtypeStruct((g, rows, tw), F32),
        scratch_shapes=[pltpu.VMEM((rows, p4), F32)] * 3,
        compiler_params=_cparams(("arbitrary",), 32),
    )(u_g, ma, mi, mo, a16)


def _s5_out_kernel(y_ref, u_ref, d_ref, w_ref, b_ref, gn_ref, o_ref):
    y = y_ref[...] + d_ref[...] * u_ref[...]
    g = jax.nn.gelu(y)
    z = jnp.dot(g.astype(BF16), w_ref[...], preferred_element_type=F32) + b_ref[...]
    o_ref[...] = _rms(g * jax.nn.sigmoid(z), gn_ref[...]).astype(o_ref.dtype)


def s5_output(y, proj, ssm_d, w_glu, b_glu, gn, *, rows, tm):
    ch = D_SSM
    vec = lambda: pl.BlockSpec((1, ch), lambda i: (0, 0))
    return pl.pallas_call(
        _s5_out_kernel,
        grid=(rows // tm,),
        in_specs=[
            pl.BlockSpec((tm, ch), lambda i: (i, 0)),
            pl.BlockSpec((tm, ch), lambda i: (i, COL_U // ch)),
            vec(),
            pl.BlockSpec((ch, ch), lambda i: (0, 0)),
            vec(), vec(),
        ],
        out_specs=pl.BlockSpec((tm, ch), lambda i: (i, 0)),
        out_shape=jax.ShapeDtypeStruct((rows, ch), BF16),
        compiler_params=_cparams(("arbitrary",), 32),
    )(y, proj, ssm_d.reshape(1, ch), w_glu, b_glu.reshape(1, ch), gn.reshape(1, ch))


def _final_norm_kernel(x_ref, g_ref, o_ref):
    o_ref[...] = _rms(x_ref[...], g_ref[...])


def final_rms_norm(x, g, *, rows, tm):
    d = x.shape[1]
    return pl.pallas_call(
        _final_norm_kernel,
        grid=(rows // tm,),
        in_specs=[pl.BlockSpec((tm, d), lambda i: (i, 0)), pl.BlockSpec((1, d), lambda i: (0, 0))],
        out_specs=pl.BlockSpec((tm, d), lambda i: (i, 0)),
        out_shape=jax.ShapeDtypeStruct((rows, d), F32),
        compiler_params=_cparams(("arbitrary",), 32),
    )(x, g.reshape(1, d))


def _rope_partner_perm():
    idx = []
    for i in range(QK_ROPE):
        idx.append(i + 16 if (i % 32) < 16 else i - 16)
    return jnp.asarray(idx, jnp.int32)


def rope_tables(seq, tab_tile):
    rows = seq // GRID_W
    row = jnp.repeat(jnp.arange(rows, dtype=F32), GRID_W)
    col = jnp.tile(jnp.arange(GRID_W, dtype=F32), rows)
    n_freq = QK_ROPE // 4
    inv = ROPE_BASE ** (-jnp.arange(n_freq, dtype=F32) / n_freq)
    ar = row[:, None] * inv
    ac = col[:, None] * inv
    cos = jnp.concatenate([jnp.cos(ar), jnp.cos(ar), jnp.cos(ac), jnp.cos(ac)], axis=1)
    sin = jnp.concatenate([-jnp.sin(ar), jnp.sin(ar), -jnp.sin(ac), jnp.sin(ac)], axis=1)
    cos = jnp.concatenate([cos, jnp.ones((tab_tile, QK_ROPE), F32)], axis=0)
    sin = jnp.concatenate([sin, jnp.zeros((tab_tile, QK_ROPE), F32)], axis=0)
    pad = jnp.zeros((seq + tab_tile, LANE - QK_ROPE), F32)
    return jnp.concatenate([cos, pad], axis=1), jnp.concatenate([sin, pad], axis=1)


def prep_w_in(w_in):
    d = w_in.shape[0]
    s = [0, 512, 1024, 1536, 2048, 2304, 2368, 2880]
    hconv, bg, cg, cq, ckv, kr, u = [w_in[:, s[i]:s[i + 1]] for i in range(7)]
    krp = kr[:, _rope_partner_perm()]
    pad = jnp.zeros((d, N_PROJ - (COL_KR + LANE)), w_in.dtype)
    return jnp.concatenate([hconv, bg, cg, cq, u, ckv, kr, krp, pad], axis=1).astype(BF16)


def prep_w_uq(w_uq):
    r = w_uq.shape[0]
    w = w_uq.reshape(r, MLA_HEADS, QK_NOPE + QK_ROPE)
    rope = w[:, :, QK_NOPE:]
    w = jnp.concatenate([w, rope[:, :, _rope_partner_perm()]], axis=-1)
    return w.reshape(r, MLA_HEADS * QK_PAD).astype(BF16)


def prep_w_ukv(w_ukv):
    r = w_ukv.shape[0]
    w = w_ukv.reshape(r, MLA_HEADS, QK_NOPE + V_HEAD)
    kn = w[:, :, :QK_NOPE].reshape(r, MLA_HEADS * QK_NOPE)
    v = w[:, :, QK_NOPE:].reshape(r, MLA_HEADS * V_HEAD)
    return jnp.concatenate([kn, v], axis=1).astype(BF16)


def kernel(x, c, ctx, c_ctx, w_ada, b_ada, norm1_g, norm2_g, w_in, conv_w, mla_q_norm, w_uq, mla_kv_norm, w_ukv, ssm_a_re, ssm_a_im, ssm_log_dt, ssm_b_re, ssm_b_im, ssm_c_re, ssm_c_im, ssm_d, w_glu, b_glu, mix_norm, w_o, w_gate, w_up, w_down, final_norm):
    nb, seq, d = x.shape
    ctx_len = ctx.shape[1]
    depth = w_ada.shape[0]
    r_lat = nb * seq
    r_ctx = nb * ctx_len
    r_all = r_lat + r_ctx
    assert seq % TM == 0 and r_ctx == TM and seq % CHUNK == 0 and ctx_len % CHUNK == 0
    tiles_per_seq = seq // TM
    lat_tiles_m = r_lat // TM
    d_ff = w_gate.shape[2]

    xs = jnp.concatenate([x.reshape(r_lat, d), ctx.reshape(r_ctx, d)], axis=0)

    cvec = jnp.concatenate([c, c_ctx[None, :], jnp.zeros((8 - nb - 1, d), F32)], axis=0)
    mod = ada_modulation(cvec, w_ada, b_ada)
    mod4 = mod.reshape(depth, 8, 1, 6 * d)

    tp = 512
    cos_t, sin_t = rope_tables(seq, tp)
    abar_re, abar_im, coef_re, coef_im = s5_discretise(ssm_a_re, ssm_a_im, ssm_log_dt)
    ma, mi, mo, a16 = s5_matrices(abar_re, abar_im, coef_re, coef_im,
                                  ssm_b_re, ssm_b_im, ssm_c_re, ssm_c_im)
    lat_chunks = seq // CHUNK
    ctx_chunks = ctx_len // CHUNK
    tw = CHUNK * SSM_GROUP

    for i in range(depth):
        ctx_out = i < depth - 1
        m_tiles = lat_tiles_m + (1 if ctx_out else 0)
        rows_out = r_all if ctx_out else r_lat
        tile_kw = dict(m_tiles=m_tiles, tiles_per_seq=tiles_per_seq, n_batch=nb)

        proj = norm_mod_matmul(xs, norm1_g[i], mod4, i, 0, 1, [prep_w_in(w_in[i])],
                               m_tiles=lat_tiles_m + 1, tiles_per_seq=tiles_per_seq, n_batch=nb,
                               tn=512, out_dtype=F32)

        y_conv = conv_mixer(proj, conv_w[i], mix_norm[i, :D_CONV], seq=seq, n_seq=nb,
                            row_block0=0, rows_out=r_lat)
        if ctx_out:
            y_conv_c = conv_mixer(proj, conv_w[i], mix_norm[i, :D_CONV], seq=ctx_len, n_seq=nb,
                                  row_block0=r_lat // ctx_len, rows_out=r_ctx)
            y_conv = jnp.concatenate([y_conv, y_conv_c], axis=0)

        q_rows = rows_out
        q = q_projection(proj, mla_q_norm[i], prep_w_uq(w_uq[i]), cos_t, sin_t, rows=q_rows, tm=tp,
                         lat_tiles=r_lat // tp, tiles_per_seq=seq // tp)
        k, v = kv_projection(proj, mla_kv_norm[i], prep_w_ukv(w_ukv[i]), cos_t, sin_t, rows=r_all,
                             tm=tp, lat_tiles=r_lat // tp, tiles_per_seq=seq // tp)
        y_att = latent_attention(q, k, v, mix_norm[i, D_CONV:D_CONV + D_ATTN], n_batch=nb, seq=seq,
                                 ctx_len=ctx_len, tq=ctx_len, ctx_queries=ctx_out, rows_out=rows_out)

        u = proj[:, COL_U:COL_U + D_SSM]
        u_lat = u[:r_lat].reshape(nb, lat_chunks, CHUNK, SSM_GROUPS, SSM_GROUP)
        u_ctx = u[r_lat:].reshape(nb, ctx_chunks, CHUNK, SSM_GROUPS, SSM_GROUP)
        u_g = jnp.concatenate([u_lat, u_ctx], axis=1).transpose(3, 1, 0, 2, 4)
        u_g = u_g.reshape(SSM_GROUPS, (lat_chunks + ctx_chunks) * nb, tw)
        y_g = s5_chunked(u_g, ma, mi, mo, a16, i, lat_chunks=lat_chunks, ctx_chunks=ctx_chunks, nb=nb)
        y_t = y_g.reshape(SSM_GROUPS, lat_chunks + ctx_chunks, nb, CHUNK, SSM_GROUP)
        y_t = y_t.transpose(2, 1, 3, 0, 4)
        y_tok = y_t[:, :lat_chunks].reshape(r_lat, D_SSM)
        if ctx_out:
            y_tok = jnp.concatenate([y_tok, y_t[:, lat_chunks:].reshape(r_ctx, D_SSM)], axis=0)
        y_ssm = s5_output(y_tok, proj, ssm_d[i], w_glu[i].astype(BF16), b_glu[i],
                          mix_norm[i, D_CONV + D_ATTN:], rows=rows_out, tm=tp)

        xs = matmul_gated_residual([y_conv, y_att, y_ssm], w_o[i].astype(BF16), xs, mod4, i, 2,
                                   tn=512, **tile_kw)
        hidden = norm_mod_matmul(xs, norm2_g[i], mod4, i, 3, 4,
                                 [w_gate[i].astype(BF16), w_up[i].astype(BF16)],
                                 tn=512, out_dtype=BF16, **tile_kw)
        xs = matmul_gated_residual([hidden], w_down[i].astype(BF16), xs, mod4, i, 5,
                                   tn=512, **tile_kw)

    out = final_rms_norm(xs, final_norm, rows=r_lat, tm=tp)
    return out.reshape(nb, seq, d)
```

```python
import functools
import math

import jax
import jax.numpy as jnp
from jax import lax
from jax.experimental import pallas as pl
from jax.experimental.pallas import tpu as pltpu

F32 = jnp.float32
BF16 = jnp.bfloat16

EPS = 1e-6
GRID_W = 64
CONV_W = 3
D_CONV = 512
D_SSM = 512
D_ATTN = 1024
MLA_HEADS = 8
QK_NOPE = 128
QK_ROPE = 64
V_HEAD = 128
Q_RANK = 512
KV_RANK = 256
ROPE_BASE = 10000.0
MLA_SCALE = (QK_NOPE + QK_ROPE) ** -0.5
SSM_GROUP = 16
SSM_GROUPS = 32
SSM_STATE = 64
CHUNK = 16

QK_PAD = 256
LANE = 128
TM = 1024
MIB = 1024 * 1024

COL_CONV = 0
COL_CQ = 3 * D_CONV
COL_U = COL_CQ + Q_RANK
COL_CKV = COL_U + D_SSM
COL_KR = COL_CKV + KV_RANK
N_PROJ = 3072


def _cparams(sem, vmem_mib):
    return pltpu.CompilerParams(dimension_semantics=sem, vmem_limit_bytes=vmem_mib * MIB)


def _rms(x, g):
    return x * lax.rsqrt(jnp.mean(x * x, axis=-1, keepdims=True) + EPS) * g


def _ada_kernel(c_ref, w_ref, b_ref, o_ref):
    cv = c_ref[...]
    s = (cv * jax.nn.sigmoid(cv)).astype(BF16)
    o_ref[...] = jnp.dot(s, w_ref[...].astype(BF16), preferred_element_type=F32) + b_ref[...]


def ada_modulation(cvec, w_ada, b_ada, tn=1024):
    depth, d, n = w_ada.shape
    rows = cvec.shape[0]
    return pl.pallas_call(
        _ada_kernel,
        grid=(depth, n // tn),
        in_specs=[
            pl.BlockSpec((rows, d), lambda l, j: (0, 0)),
            pl.BlockSpec((None, d, tn), lambda l, j: (l, 0, j)),
            pl.BlockSpec((None, 1, tn), lambda l, j: (l, 0, j)),
        ],
        out_specs=pl.BlockSpec((None, rows, tn), lambda l, j: (l, 0, j)),
        out_shape=jax.ShapeDtypeStruct((depth, rows, n), F32),
        compiler_params=_cparams(("arbitrary", "arbitrary"), 40),
    )(cvec, w_ada, b_ada.reshape(depth, 1, n))


def _nm_mm_kernel(x_ref, g_ref, sh_ref, sc_ref, *rest, swiglu, row_chunk):
    if swiglu:
        wg_ref, wu_ref, o_ref, a_scr = rest
    else:
        w_ref, o_ref, a_scr = rest
    tm = x_ref.shape[0]

    @pl.when(pl.program_id(1) == 0)
    def _():
        g = g_ref[...]
        sh = sh_ref[...]
        sc1 = 1.0 + sc_ref[...]
        for r in range(0, tm, row_chunk):
            x = x_ref[r:r + row_chunk, :]
            a_scr[r:r + row_chunk, :] = (_rms(x, g) * sc1 + sh).astype(BF16)

    a = a_scr[...]
    if swiglu:
        gt = jnp.dot(a, wg_ref[...], preferred_element_type=F32)
        up = jnp.dot(a, wu_ref[...], preferred_element_type=F32)
        o_ref[...] = (gt * jax.nn.sigmoid(gt) * up).astype(o_ref.dtype)
    else:
        o_ref[...] = jnp.dot(a, w_ref[...], preferred_element_type=F32).astype(o_ref.dtype)


def norm_mod_matmul(x, gain, mod4, layer, col_shift, col_scale, ws, *, m_tiles, tiles_per_seq,
                    n_batch, tn, out_dtype, tm=TM):
    m, d = x.shape
    n = ws[0].shape[1]
    swiglu = len(ws) == 2

    def mod_spec(col):
        return pl.BlockSpec((None, None, 1, d),
                            lambda i, j: (layer, jnp.minimum(i // tiles_per_seq, n_batch), 0, col))

    in_specs = [
        pl.BlockSpec((tm, d), lambda i, j: (i, 0)),
        pl.BlockSpec((1, d), lambda i, j: (0, 0)),
        mod_spec(col_shift),
        mod_spec(col_scale),
    ] + [pl.BlockSpec((d, tn), lambda i, j: (0, j)) for _ in ws]
    return pl.pallas_call(
        functools.partial(_nm_mm_kernel, swiglu=swiglu, row_chunk=min(256, tm)),
        grid=(m_tiles, n // tn),
        in_specs=in_specs,
        out_specs=pl.BlockSpec((tm, tn), lambda i, j: (i, j)),
        out_shape=jax.ShapeDtypeStruct((m, n), out_dtype),
        scratch_shapes=[pltpu.VMEM((tm, d), BF16)],
        compiler_params=_cparams(("arbitrary", "arbitrary"), 48),
    )(x, gain.reshape(1, d), mod4, mod4, *ws)


def _mm_resid_kernel(*refs, splits):
    n = len(splits)
    a_refs = refs[:n]
    w_ref, x_ref, gate_ref, o_ref = refs[n:]
    acc = None
    off = 0
    for a_ref, k in zip(a_refs, splits):
        part = jnp.dot(a_ref[...], w_ref[off:off + k, :], preferred_element_type=F32)
        acc = part if acc is None else acc + part
        off += k
    o_ref[...] = x_ref[...] + gate_ref[...] * acc


def matmul_gated_residual(a_parts, w, x, mod4, layer, col_gate, *, m_tiles, tiles_per_seq,
                          n_batch, tn, tm=TM):
    m, d = x.shape
    k_total, n = w.shape
    splits = tuple(a.shape[1] for a in a_parts)
    assert sum(splits) == k_total and n == d
    gate_blocks = d // tn
    in_specs = [pl.BlockSpec((tm, k), lambda i, j: (i, 0)) for k in splits] + [
        pl.BlockSpec((k_total, tn), lambda i, j: (0, j)),
        pl.BlockSpec((tm, tn), lambda i, j: (i, j)),
        pl.BlockSpec((None, None, 1, tn),
                     lambda i, j: (layer, jnp.minimum(i // tiles_per_seq, n_batch), 0,
                                   col_gate * gate_blocks + j)),
    ]
    return pl.pallas_call(
        functools.partial(_mm_resid_kernel, splits=splits),
        grid=(m_tiles, n // tn),
        in_specs=in_specs,
        out_specs=pl.BlockSpec((tm, tn), lambda i, j: (i, j)),
        out_shape=jax.ShapeDtypeStruct((m, n), F32),
        compiler_params=_cparams(("arbitrary", "arbitrary"), 52),
    )(*a_parts, w, x, mod4)


def _conv_kernel(h_ref, bg_ref, cg_ref, w_ref, gn_ref, *rest, row_chunk):
    o_ref, z_scr = rest[-2:]
    seq, ch = h_ref.shape
    zeros8 = jnp.zeros((8, ch), F32)
    z_scr[0:8, :] = zeros8
    z_scr[seq + 8:seq + 16, :] = zeros8
    for r in range(0, seq, row_chunk):
        z_scr[8 + r:8 + r + row_chunk, :] = (cg_ref[r:r + row_chunk, :].astype(F32)
                                             * h_ref[r:r + row_chunk, :].astype(F32))
    w0 = w_ref[0:1, :]
    w1 = w_ref[1:2, :]
    w2 = w_ref[2:3, :]
    gn = gn_ref[...]
    for r in range(0, seq, row_chunk):
        zp = z_scr[7 + r:7 + r + row_chunk, :]
        zc = z_scr[8 + r:8 + r + row_chunk, :]
        zn = z_scr[9 + r:9 + r + row_chunk, :]
        y = bg_ref[r:r + row_chunk, :].astype(F32) * (w0 * zp + w1 * zc + w2 * zn)
        o_ref[r:r + row_chunk, :] = _rms(y, gn).astype(o_ref.dtype)


def conv_mixer(proj, conv_w, gn, *, seq, n_seq, row_block0, rows_out, into=None):
    ch = D_CONV
    in_specs = [pl.BlockSpec((seq, ch), functools.partial(lambda s, c: (row_block0 + s, c), c=c))
                for c in range(3)]
    in_specs += [pl.BlockSpec((CONV_W, ch), lambda s: (0, 0)), pl.BlockSpec((1, ch), lambda s: (0, 0))]
    args = [proj, proj, proj, conv_w, gn.reshape(1, ch)]
    aliases = {}
    if into is not None:
        in_specs.append(pl.BlockSpec(memory_space=pl.ANY))
        args.append(into)
        aliases = {len(args) - 1: 0}
    return pl.pallas_call(
        functools.partial(_conv_kernel, row_chunk=min(256, seq)),
        grid=(n_seq,),
        in_specs=in_specs,
        out_specs=pl.BlockSpec((seq, ch), lambda s: (row_block0 + s, 0)),
        out_shape=jax.ShapeDtypeStruct((rows_out, ch), BF16),
        scratch_shapes=[pltpu.VMEM((seq + 16, ch), F32)],
        input_output_aliases=aliases,
        compiler_params=_cparams(("arbitrary",), 48),
    )(*args)


def _q_proj_kernel(cq_ref, g_ref, w_ref, cos_ref, sin_ref, o_ref):
    a = _rms(cq_ref[...].astype(F32), g_ref[...]).astype(BF16)
    q = jnp.dot(a, w_ref[...], preferred_element_type=F32)
    cos = cos_ref[...]
    sin = sin_ref[...]
    for h in range(MLA_HEADS):
        c0 = h * QK_PAD
        o_ref[:, c0:c0 + QK_NOPE] = (q[:, c0:c0 + QK_NOPE] * MLA_SCALE).astype(o_ref.dtype)
        blk = q[:, c0 + QK_NOPE:c0 + QK_PAD]
        rot = blk * cos + pltpu.roll(blk, QK_ROPE, axis=1) * sin
        o_ref[:, c0 + QK_NOPE:c0 + QK_PAD] = (rot * MLA_SCALE).astype(o_ref.dtype)


def q_projection(proj, q_norm, w_q, cos_t, sin_t, *, rows, tm, lat_tiles, tiles_per_seq):
    tab_map = lambda i: (jnp.where(i < lat_tiles, i % tiles_per_seq, tiles_per_seq), 0)
    n = MLA_HEADS * QK_PAD
    return pl.pallas_call(
        _q_proj_kernel,
        grid=(rows // tm,),
        in_specs=[
            pl.BlockSpec((tm, Q_RANK), lambda i: (i, COL_CQ // Q_RANK)),
            pl.BlockSpec((1, Q_RANK), lambda i: (0, 0)),
            pl.BlockSpec((Q_RANK, n), lambda i: (0, 0)),
            pl.BlockSpec((tm, LANE), tab_map),
            pl.BlockSpec((tm, LANE), tab_map),
        ],
        out_specs=pl.BlockSpec((tm, n), lambda i: (i, 0)),
        out_shape=jax.ShapeDtypeStruct((rows, n), BF16),
        compiler_params=_cparams(("arbitrary",), 40),
    )(proj, q_norm.reshape(1, Q_RANK), w_q, cos_t, sin_t)


def _kv_proj_kernel(ckv_ref, kr_ref, g_ref, w_ref, cos_ref, sin_ref, k_ref, v_ref):
    a = _rms(ckv_ref[...].astype(F32), g_ref[...]).astype(BF16)
    kv = jnp.dot(a, w_ref[...], preferred_element_type=F32)
    blk = kr_ref[...].astype(F32)
    rot = (blk * cos_ref[...] + pltpu.roll(blk, QK_ROPE, axis=1) * sin_ref[...]).astype(k_ref.dtype)
    nk = MLA_HEADS * QK_NOPE
    for h in range(MLA_HEADS):
        c0 = h * QK_PAD
        k_ref[:, c0:c0 + QK_NOPE] = kv[:, h * QK_NOPE:(h + 1) * QK_NOPE].astype(k_ref.dtype)
        k_ref[:, c0 + QK_NOPE:c0 + QK_PAD] = rot
    v_ref[...] = kv[:, nk:].astype(v_ref.dtype)


def kv_projection(proj, kv_norm, w_kv, cos_t, sin_t, *, rows, tm, lat_tiles, tiles_per_seq):
    tab_map = lambda i: (jnp.where(i < lat_tiles, i % tiles_per_seq, tiles_per_seq), 0)
    nk = MLA_HEADS * QK_PAD
    nv = MLA_HEADS * V_HEAD
    return pl.pallas_call(
        _kv_proj_kernel,
        grid=(rows // tm,),
        in_specs=[
            pl.BlockSpec((tm, KV_RANK), lambda i: (i, COL_CKV // KV_RANK)),
            pl.BlockSpec((tm, LANE), lambda i: (i, COL_KR // LANE)),
            pl.BlockSpec((1, KV_RANK), lambda i: (0, 0)),
            pl.BlockSpec((KV_RANK, MLA_HEADS * (QK_NOPE + V_HEAD)), lambda i: (0, 0)),
            pl.BlockSpec((tm, LANE), tab_map),
            pl.BlockSpec((tm, LANE), tab_map),
        ],
        out_specs=[pl.BlockSpec((tm, nk), lambda i: (i, 0)), pl.BlockSpec((tm, nv), lambda i: (i, 0))],
        out_shape=[jax.ShapeDtypeStruct((rows, nk), BF16), jax.ShapeDtypeStruct((rows, nv), BF16)],
        compiler_params=_cparams(("arbitrary",), 40),
    )(proj, proj, kv_norm.reshape(1, KV_RANK), w_kv, cos_t, sin_t)


_NT = (((1,), (1,)), ((), ()))


def _attn_kernel(q_ref, kl_ref, kc_ref, vl_ref, vc_ref, gn_ref, o_ref, acc_scr, *, lat_tiles):
    t = pl.program_id(1)

    def heads(with_latent):
        for h in range(MLA_HEADS):
            q = q_ref[:, h * QK_PAD:(h + 1) * QK_PAD]
            sc = lax.dot_general(q, kc_ref[:, h * QK_PAD:(h + 1) * QK_PAD], _NT,
                                 preferred_element_type=F32)
            m = jnp.max(sc, axis=-1, keepdims=True)
            if with_latent:
                sl = lax.dot_general(q, kl_ref[:, h * QK_PAD:(h + 1) * QK_PAD], _NT,
                                     preferred_element_type=F32)
                m = jnp.maximum(m, jnp.max(sl, axis=-1, keepdims=True))
                pl_ = jnp.exp(sl - m)
            pc = jnp.exp(sc - m)
            den = jnp.sum(pc, axis=-1, keepdims=True)
            o = jnp.dot(pc.astype(BF16), vc_ref[:, h * V_HEAD:(h + 1) * V_HEAD],
                        preferred_element_type=F32)
            if with_latent:
                den = den + jnp.sum(pl_, axis=-1, keepdims=True)
                o = o + jnp.dot(pl_.astype(BF16), vl_ref[:, h * V_HEAD:(h + 1) * V_HEAD],
                                preferred_element_type=F32)
            acc_scr[:, h * V_HEAD:(h + 1) * V_HEAD] = o / den

    @pl.when(t < lat_tiles)
    def _():
        heads(True)

    @pl.when(t >= lat_tiles)
    def _():
        heads(False)

    o_ref[...] = _rms(acc_scr[...], gn_ref[...]).astype(o_ref.dtype)


def latent_attention(q, k, v, gn, *, n_batch, seq, ctx_len, tq, ctx_queries, rows_out):
    lat_tiles = seq // tq
    assert ctx_len == tq
    q_tiles = lat_tiles + (1 if ctx_queries else 0)
    ctx_blk0 = n_batch * seq // ctx_len
    nq = MLA_HEADS * QK_PAD
    nv = MLA_HEADS * V_HEAD
    qmap = lambda b, t: (jnp.where(t < lat_tiles, b * lat_tiles + t, ctx_blk0 + b), 0)
    return pl.pallas_call(
        functools.partial(_attn_kernel, lat_tiles=lat_tiles),
        grid=(n_batch, q_tiles),
        in_specs=[
            pl.BlockSpec((tq, nq), qmap),
            pl.BlockSpec((seq, nq), lambda b, t: (b, 0)),
            pl.BlockSpec((ctx_len, nq), lambda b, t: (ctx_blk0 + b, 0)),
            pl.BlockSpec((seq, nv), lambda b, t: (b, 0)),
            pl.BlockSpec((ctx_len, nv), lambda b, t: (ctx_blk0 + b, 0)),
            pl.BlockSpec((1, nv), lambda b, t: (0, 0)),
        ],
        out_specs=pl.BlockSpec((tq, nv), qmap),
        out_shape=jax.ShapeDtypeStruct((rows_out, nv), BF16),
        scratch_shapes=[pltpu.VMEM((tq, nv), F32)],
        compiler_params=_cparams(("arbitrary", "arbitrary"), 56),
    )(q, k, k, v, v, gn.reshape(1, nv))


def _s5_disc_kernel(are_ref, aim_ref, ldt_ref, abr_ref, abi_ref, cfr_ref, cfi_ref):
    ar = are_ref[...]
    ai = aim_ref[...]
    dt = jnp.exp(ldt_ref[...])
    mag = jnp.exp(ar * dt)
    th = ai * dt
    br = mag * jnp.cos(th)
    bi = mag * jnp.sin(th)
    nr = br - 1.0
    den = ar * ar + ai * ai
    abr_ref[...] = br
    abi_ref[...] = bi
    cfr_ref[...] = (nr * ar + bi * ai) / den
    cfi_ref[...] = (bi * ar - nr * ai) / den


def s5_discretise(a_re, a_im, log_dt):
    shp = a_re.shape
    rows = shp[0] * shp[1] * shp[2]
    flat = lambda t: t.reshape(rows, shp[3])
    ldt = jnp.broadcast_to(log_dt[..., None], shp)
    outs = pl.pallas_call(
        _s5_disc_kernel,
        out_shape=[jax.ShapeDtypeStruct((rows, shp[3]), F32)] * 4,
    )(flat(a_re), flat(a_im), flat(ldt))
    return [o.reshape(shp) for o in outs]


def _cmul(xr, xi, yr, yi):
    return xr * yr - xi * yi, xr * yi + xi * yr


def _s5_mats_kernel(abc_re, abc_im, cex_re, cex_im, afb_re, afb_im, cfb_re, cfb_im, bfb_re, bfb_im,
                    cpp_re, cpp_im, bx_ref, by_ref, ma_ref, mi_ref, mo_ref, a16_ref):
    tw = CHUNK * SSM_GROUP
    p = SSM_STATE
    lane = lax.broadcasted_iota(jnp.int32, (p, tw), 1)
    tblk = lane // SSM_GROUP
    lane16 = lax.broadcasted_iota(jnp.int32, (SSM_GROUP, tw), 1)
    lane128 = lax.broadcasted_iota(jnp.int32, (1, 2 * p), 1)
    fwd_lanes = lane128 < p
    sgn = jnp.where(lax.broadcasted_iota(jnp.int32, (SSM_GROUP, 2 * p), 1) < p, 1.0, -1.0)
    zeros = jnp.zeros((p, tw), F32)
    mi_blocks = []
    for d in range(2):
        ar = jnp.broadcast_to(abc_re[d], (p, tw))
        ai = jnp.broadcast_to(abc_im[d], (p, tw))
        sel_r = tblk if d == 0 else (CHUNK - 1) - tblk
        sel_e = tblk + 1 if d == 0 else CHUNK - tblk
        pr, pi = jnp.ones((p, tw), F32), zeros
        e_re = e_im = r_re = r_im = zeros
        for k in range(CHUNK + 1):
            if k < CHUNK:
                r_re = jnp.where(sel_r == k, pr, r_re)
                r_im = jnp.where(sel_r == k, pi, r_im)
            if k >= 1:
                e_re = jnp.where(sel_e == k, pr, e_re)
                e_im = jnp.where(sel_e == k, pi, e_im)
            if k < CHUNK:
                pr, pi = _cmul(pr, pi, ar, ai)
        cr = cex_re[d]
        ci = cex_im[d]
        w_re, w_im = _cmul(cr, ci, e_re, e_im)
        base = d * 4 * p
        mdt = mo_ref.dtype
        zeros_m = zeros.astype(mdt)
        if d == 0:
            mo_ref[base:base + p, :] = w_re.astype(mdt)
            mo_ref[base + p:base + 2 * p, :] = zeros_m
            mo_ref[base + 2 * p:base + 3 * p, :] = (-w_im).astype(mdt)
            mo_ref[base + 3 * p:base + 4 * p, :] = zeros_m
        else:
            mo_ref[base:base + p, :] = zeros_m
            mo_ref[base + p:base + 2 * p, :] = w_re.astype(mdt)
            mo_ref[base + 2 * p:base + 3 * p, :] = zeros_m
            mo_ref[base + 3 * p:base + 4 * p, :] = (-w_im).astype(mdt)
        rr_re, rr_im = _cmul(cr, ci, r_re, r_im)
        stacked = jnp.concatenate([rr_re, rr_im], axis=0)
        lm = sgn * cpp_re[d] * bx_ref[d] - cpp_im[d] * by_ref[d]
        kall = jnp.dot(lm, stacked, preferred_element_type=F32, precision=lax.Precision.HIGHEST)
        for s in range(CHUNK):
            if d == 0:
                shift = SSM_GROUP * s
                keep = lane16 >= SSM_GROUP * s
            else:
                shift = (SSM_GROUP * (s + 1)) % tw
                keep = lane16 < SSM_GROUP * (s + 1)
            rolled = pltpu.roll(kall, shift, axis=1) if shift else kall
            blk = jnp.where(keep, rolled, 0.0)
            if d == 0:
                mi_blocks.append(blk)
            else:
                mi_ref[SSM_GROUP * s:SSM_GROUP * (s + 1), :] = (mi_blocks[s] + blk).astype(mi_ref.dtype)

    a_re = afb_re[...]
    a_im = afb_im[...]
    cf_re = cfb_re[...]
    cf_im = cfb_im[...]
    b_re = bfb_re[...]
    b_im = bfb_im[...]
    powers = []
    qr, qi = jnp.ones((1, 2 * p), F32), jnp.zeros((1, 2 * p), F32)
    for k in range(CHUNK + 1):
        powers.append((qr, qi))
        if k < CHUNK:
            qr, qi = _cmul(qr, qi, a_re, a_im)
    for s in range(CHUNK):
        g_re = jnp.where(fwd_lanes, powers[CHUNK - 1 - s][0], powers[s][0])
        g_im = jnp.where(fwd_lanes, powers[CHUNK - 1 - s][1], powers[s][1])
        g_re, g_im = _cmul(g_re, g_im, cf_re, cf_im)
        rows = slice(SSM_GROUP * s, SSM_GROUP * (s + 1))
        ma_ref[rows, 0:2 * p] = (g_re * b_re - g_im * b_im).astype(ma_ref.dtype)
        ma_ref[rows, 2 * p:4 * p] = (g_re * b_im + g_im * b_re).astype(ma_ref.dtype)
    a16_ref[0:1, :] = powers[CHUNK][0]
    a16_ref[1:2, :] = powers[CHUNK][1]


def s5_matrices(abar_re, abar_im, coef_re, coef_im, b_re, b_im, c_re, c_im):
    depth, _, g, p = abar_re.shape
    hg = b_re.shape[-1]
    tw = CHUNK * hg
    t0213 = lambda t: t.transpose(0, 2, 1, 3)
    abc = [t0213(t)[..., None] for t in (abar_re, abar_im)]
    cex = [jnp.tile(t.transpose(0, 2, 1, 4, 3), (1, 1, 1, 1, CHUNK)) for t in (c_re, c_im)]
    fb = lambda t: t0213(t).reshape(depth, g, 1, 2 * p)
    afb = [fb(t) for t in (abar_re, abar_im)]
    cfb = [fb(t) for t in (coef_re, coef_im)]
    bfb = [t.transpose(0, 2, 4, 1, 3).reshape(depth, g, hg, 2 * p) for t in (b_re, b_im)]
    cpp = [t0213(jnp.concatenate([t, t], axis=-1))[:, :, :, None, :] for t in (coef_re, coef_im)]
    bt_re = b_re.transpose(0, 2, 1, 4, 3)
    bt_im = b_im.transpose(0, 2, 1, 4, 3)
    bx = jnp.concatenate([bt_re, bt_im], axis=-1)
    by = jnp.concatenate([bt_im, bt_re], axis=-1)
    ins = [*abc, *cex, *afb, *cfb, *bfb, *cpp, bx, by]

    def spec(t):
        blk = (None, None) + t.shape[2:]
        nz = len(t.shape) - 2
        return pl.BlockSpec(blk, lambda l, gi: (l, gi) + (0,) * nz)

    def ospec(r, c):
        return pl.BlockSpec((None, None, r, c), lambda l, gi: (l, gi, 0, 0))

    return pl.pallas_call(
        _s5_mats_kernel,
        grid=(depth, g),
        in_specs=[spec(t) for t in ins],
        out_specs=[ospec(tw, 4 * p), ospec(tw, tw), ospec(8 * p, tw), ospec(2, 2 * p)],
        out_shape=[
            jax.ShapeDtypeStruct((depth, g, tw, 4 * p), BF16),
            jax.ShapeDtypeStruct((depth, g, tw, tw), BF16),
            jax.ShapeDtypeStruct((depth, g, 8 * p, tw), BF16),
            jax.ShapeDtypeStruct((depth, g, 2, 2 * p), F32),
        ],
        compiler_params=_cparams(("arbitrary", "arbitrary"), 32),
    )(*ins)


GROUP_BLOCK = LANE // SSM_GROUP


def lane_swap_matrix():
    idx = jnp.arange(GROUP_BLOCK * LANE)
    a, b, c = idx // LANE, (idx // SSM_GROUP) % GROUP_BLOCK, idx % SSM_GROUP
    dst = b * LANE + a * SSM_GROUP + c
    return (dst[:, None] == idx[None, :]).astype(BF16)


def _s5_main_kernel(*refs, order_f, order_b, nb, lat_chunks, ctx_chunks):
    u_refs = refs[:CHUNK]
    (p_ref, ma_ref, mi_ref, mo_ref, a_ref, y_ref, ug_scr, yg_scr,
     xr_scr, xi_scr, fr_scr, fi_scr, br_scr, bi_scr) = refs[CHUNK:]
    p2 = 2 * SSM_STATE
    half = CHUNK // 2

    for th in range(2):
        slab = jnp.concatenate([u_refs[half * th + tl][...] for tl in range(half)], axis=1)
        perm = jnp.dot(slab, p_ref[...], preferred_element_type=F32).astype(BF16)
        for g in range(GROUP_BLOCK):
            ug_scr[g, :, th * LANE:(th + 1) * LANE] = perm[:, g * LANE:(g + 1) * LANE]

    for g in range(GROUP_BLOCK):
        x = jnp.dot(ug_scr[g], ma_ref[g], preferred_element_type=F32)
        xr_scr[g] = x[:, 0:p2]
        xi_scr[g] = x[:, p2:2 * p2]

    def rows_of(j):
        if j < lat_chunks:
            return pl.ds(j, nb, stride=lat_chunks)
        return pl.ds(nb * lat_chunks + (j - lat_chunks), nb, stride=ctx_chunks)

    fwd_lanes = lax.broadcasted_iota(jnp.int32, (nb, p2), 1) < SSM_STATE
    a_re = [a_ref[g, 0:1, :] for g in range(GROUP_BLOCK)]
    a_im = [a_ref[g, 1:2, :] for g in range(GROUP_BLOCK)]
    s_re = [jnp.zeros((nb, p2), F32) for _ in range(GROUP_BLOCK)]
    s_im = [jnp.zeros((nb, p2), F32) for _ in range(GROUP_BLOCK)]
    for jf, jb in zip(order_f, order_b):
        rf = rows_of(jf)
        rb = rows_of(jb)
        for g in range(GROUP_BLOCK):
            fr_scr[g, rf, :] = s_re[g]
            fi_scr[g, rf, :] = s_im[g]
            br_scr[g, rb, :] = s_re[g]
            bi_scr[g, rb, :] = s_im[g]
            xr = jnp.where(fwd_lanes, xr_scr[g, rf, :], xr_scr[g, rb, :])
            xi = jnp.where(fwd_lanes, xi_scr[g, rf, :], xi_scr[g, rb, :])
            s_re[g], s_im[g] = (a_re[g] * s_re[g] - a_im[g] * s_im[g] + xr,
                                a_re[g] * s_im[g] + a_im[g] * s_re[g] + xi)

    for g in range(GROUP_BLOCK):
        y = jnp.dot(ug_scr[g], mi_ref[g], preferred_element_type=F32)
        sp_f = jnp.concatenate([fr_scr[g], fi_scr[g]], axis=1).astype(BF16)
        sp_b = jnp.concatenate([br_scr[g], bi_scr[g]], axis=1).astype(BF16)
        y = y + jnp.dot(sp_f, mo_ref[g, 0:2 * p2, :], preferred_element_type=F32)
        y = y + jnp.dot(sp_b, mo_ref[g, 2 * p2:4 * p2, :], preferred_element_type=F32)
        yg_scr[g] = y.astype(BF16)

    for th in range(2):
        slab = jnp.concatenate([yg_scr[g, :, th * LANE:(th + 1) * LANE] for g in range(GROUP_BLOCK)], axis=1)
        perm = jnp.dot(slab, p_ref[...], preferred_element_type=F32)
        for tl in range(half):
            y_ref[half * th + tl] = perm[:, tl * LANE:(tl + 1) * LANE].astype(y_ref.dtype)


def s5_chunked(proj, ma, mi, mo, a16, layer, *, lat_chunks, ctx_chunks, nb):
    r_all, n_proj = proj.shape
    rows = r_all // CHUNK
    tw = CHUNK * SSM_GROUP
    j_tot = lat_chunks + ctx_chunks
    order_f = tuple(range(lat_chunks, j_tot)) + tuple(range(lat_chunks))
    order_b = tuple(range(j_tot - 1, lat_chunks - 1, -1)) + tuple(range(lat_chunks - 1, -1, -1))
    p4 = 4 * SSM_STATE
    proj_c = proj.reshape(rows, CHUNK * n_proj)
    n_gb = SSM_GROUPS // GROUP_BLOCK

    def uspec(t):
        col0 = (t * n_proj + COL_U) // LANE
        return pl.BlockSpec((rows, LANE), lambda gb: (0, col0 + gb))

    def wspec(r, c):
        return pl.BlockSpec((None, GROUP_BLOCK, r, c), lambda gb: (layer, gb, 0, 0))

    return pl.pallas_call(
        functools.partial(_s5_main_kernel, order_f=order_f, order_b=order_b, nb=nb,
                          lat_chunks=lat_chunks, ctx_chunks=ctx_chunks),
        grid=(n_gb,),
        in_specs=[uspec(t) for t in range(CHUNK)] + [
            pl.BlockSpec((GROUP_BLOCK * LANE, GROUP_BLOCK * LANE), lambda gb: (0, 0)),
            wspec(tw, p4), wspec(tw, tw), wspec(2 * p4, tw), wspec(2, 2 * SSM_STATE),
        ],
        out_specs=pl.BlockSpec((CHUNK, rows, LANE), lambda gb: (0, 0, gb)),
        out_shape=jax.ShapeDtypeStruct((CHUNK, rows, D_SSM), BF16),
        scratch_shapes=[
            pltpu.VMEM((GROUP_BLOCK, rows, tw), BF16),
            pltpu.VMEM((GROUP_BLOCK, rows, tw), BF16),
        ] + [pltpu.VMEM((GROUP_BLOCK, rows, 2 * SSM_STATE), F32)] * 6,
        compiler_params=_cparams(("arbitrary",), 56),
    )(*([proj_c] * CHUNK), lane_swap_matrix(), ma, mi, mo, a16)


def _s5_out_kernel(y_ref, u_ref, d_ref, w_ref, b_ref, gn_ref, o_ref):
    y = y_ref[...].astype(F32) + d_ref[...] * u_ref[...].astype(F32)
    g = jax.nn.gelu(y)
    z = jnp.dot(g.astype(BF16), w_ref[...], preferred_element_type=F32) + b_ref[...]
    o_ref[...] = _rms(g * jax.nn.sigmoid(z), gn_ref[...]).astype(o_ref.dtype)


def s5_output(y, proj, ssm_d, w_glu, b_glu, gn, *, rows, tm):
    ch = D_SSM
    vec = lambda: pl.BlockSpec((1, ch), lambda i: (0, 0))
    return pl.pallas_call(
        _s5_out_kernel,
        grid=(rows // tm,),
        in_specs=[
            pl.BlockSpec((tm, ch), lambda i: (i, 0)),
            pl.BlockSpec((tm, ch), lambda i: (i, COL_U // ch)),
            vec(),
            pl.BlockSpec((ch, ch), lambda i: (0, 0)),
            vec(), vec(),
        ],
        out_specs=pl.BlockSpec((tm, ch), lambda i: (i, 0)),
        out_shape=jax.ShapeDtypeStruct((rows, ch), BF16),
        compiler_params=_cparams(("arbitrary",), 32),
    )(y, proj, ssm_d.reshape(1, ch), w_glu, b_glu.reshape(1, ch), gn.reshape(1, ch))


def _final_norm_kernel(x_ref, g_ref, o_ref):
    o_ref[...] = _rms(x_ref[...], g_ref[...])


def final_rms_norm(x, g, *, rows, tm):
    d = x.shape[1]
    return pl.pallas_call(
        _final_norm_kernel,
        grid=(rows // tm,),
        in_specs=[pl.BlockSpec((tm, d), lambda i: (i, 0)), pl.BlockSpec((1, d), lambda i: (0, 0))],
        out_specs=pl.BlockSpec((tm, d), lambda i: (i, 0)),
        out_shape=jax.ShapeDtypeStruct((rows, d), F32),
        compiler_params=_cparams(("arbitrary",), 32),
    )(x, g.reshape(1, d))


def _rope_partner_perm():
    idx = []
    for i in range(QK_ROPE):
        idx.append(i + 16 if (i % 32) < 16 else i - 16)
    return jnp.asarray(idx, jnp.int32)


def rope_tables(seq, tab_tile):
    rows = seq // GRID_W
    row = jnp.repeat(jnp.arange(rows, dtype=F32), GRID_W)
    col = jnp.tile(jnp.arange(GRID_W, dtype=F32), rows)
    n_freq = QK_ROPE // 4
    inv = ROPE_BASE ** (-jnp.arange(n_freq, dtype=F32) / n_freq)
    ar = row[:, None] * inv
    ac = col[:, None] * inv
    cos = jnp.concatenate([jnp.cos(ar), jnp.cos(ar), jnp.cos(ac), jnp.cos(ac)], axis=1)
    sin = jnp.concatenate([-jnp.sin(ar), jnp.sin(ar), -jnp.sin(ac), jnp.sin(ac)], axis=1)
    cos = jnp.concatenate([cos, jnp.ones((tab_tile, QK_ROPE), F32)], axis=0)
    sin = jnp.concatenate([sin, jnp.zeros((tab_tile, QK_ROPE), F32)], axis=0)
    pad = jnp.zeros((seq + tab_tile, LANE - QK_ROPE), F32)
    return jnp.concatenate([cos, pad], axis=1), jnp.concatenate([sin, pad], axis=1)


def prep_w_in(w_in):
    d = w_in.shape[0]
    s = [0, 512, 1024, 1536, 2048, 2304, 2368, 2880]
    hconv, bg, cg, cq, ckv, kr, u = [w_in[:, s[i]:s[i + 1]] for i in range(7)]
    krp = kr[:, _rope_partner_perm()]
    pad = jnp.zeros((d, N_PROJ - (COL_KR + LANE)), w_in.dtype)
    return jnp.concatenate([hconv, bg, cg, cq, u, ckv, kr, krp, pad], axis=1).astype(BF16)


def prep_w_uq(w_uq):
    r = w_uq.shape[0]
    w = w_uq.reshape(r, MLA_HEADS, QK_NOPE + QK_ROPE)
    rope = w[:, :, QK_NOPE:]
    w = jnp.concatenate([w, rope[:, :, _rope_partner_perm()]], axis=-1)
    return w.reshape(r, MLA_HEADS * QK_PAD).astype(BF16)


def prep_w_ukv(w_ukv):
    r = w_ukv.shape[0]
    w = w_ukv.reshape(r, MLA_HEADS, QK_NOPE + V_HEAD)
    kn = w[:, :, :QK_NOPE].reshape(r, MLA_HEADS * QK_NOPE)
    v = w[:, :, QK_NOPE:].reshape(r, MLA_HEADS * V_HEAD)
    return jnp.concatenate([kn, v], axis=1).astype(BF16)


def kernel(x, c, ctx, c_ctx, w_ada, b_ada, norm1_g, norm2_g, w_in, conv_w, mla_q_norm, w_uq, mla_kv_norm, w_ukv, ssm_a_re, ssm_a_im, ssm_log_dt, ssm_b_re, ssm_b_im, ssm_c_re, ssm_c_im, ssm_d, w_glu, b_glu, mix_norm, w_o, w_gate, w_up, w_down, final_norm):
    nb, seq, d = x.shape
    ctx_len = ctx.shape[1]
    depth = w_ada.shape[0]
    r_lat = nb * seq
    r_ctx = nb * ctx_len
    r_all = r_lat + r_ctx
    assert seq % TM == 0 and r_ctx == TM and seq % CHUNK == 0 and ctx_len % CHUNK == 0
    tiles_per_seq = seq // TM
    lat_tiles_m = r_lat // TM
    d_ff = w_gate.shape[2]

    xs = jnp.concatenate([x.reshape(r_lat, d), ctx.reshape(r_ctx, d)], axis=0)

    cvec = jnp.concatenate([c, c_ctx[None, :], jnp.zeros((8 - nb - 1, d), F32)], axis=0)
    mod = ada_modulation(cvec, w_ada, b_ada)
    mod4 = mod.reshape(depth, 8, 1, 6 * d)

    tp = 512
    cos_t, sin_t = rope_tables(seq, tp)
    abar_re, abar_im, coef_re, coef_im = s5_discretise(ssm_a_re, ssm_a_im, ssm_log_dt)
    ma, mi, mo, a16 = s5_matrices(abar_re, abar_im, coef_re, coef_im,
                                  ssm_b_re, ssm_b_im, ssm_c_re, ssm_c_im)
    lat_chunks = seq // CHUNK
    ctx_chunks = ctx_len // CHUNK
    tw = CHUNK * SSM_GROUP

    for i in range(depth):
        ctx_out = i < depth - 1
        m_tiles = lat_tiles_m + (1 if ctx_out else 0)
        rows_out = r_all if ctx_out else r_lat
        tile_kw = dict(m_tiles=m_tiles, tiles_per_seq=tiles_per_seq, n_batch=nb)

        proj = norm_mod_matmul(xs, norm1_g[i], mod4, i, 0, 1, [prep_w_in(w_in[i])],
                               m_tiles=lat_tiles_m + 1, tiles_per_seq=tiles_per_seq, n_batch=nb,
                               tn=512, out_dtype=BF16)

        y_conv = conv_mixer(proj, conv_w[i], mix_norm[i, :D_CONV], seq=seq, n_seq=nb,
                            row_block0=0, rows_out=rows_out)
        if ctx_out:
            y_conv = conv_mixer(proj, conv_w[i], mix_norm[i, :D_CONV], seq=ctx_len, n_seq=nb,
                                row_block0=r_lat // ctx_len, rows_out=rows_out, into=y_conv)

        q_rows = rows_out
        q = q_projection(proj, mla_q_norm[i], prep_w_uq(w_uq[i]), cos_t, sin_t, rows=q_rows, tm=tp,
                         lat_tiles=r_lat // tp, tiles_per_seq=seq // tp)
        k, v = kv_projection(proj, mla_kv_norm[i], prep_w_ukv(w_ukv[i]), cos_t, sin_t, rows=r_all,
                             tm=tp, lat_tiles=r_lat // tp, tiles_per_seq=seq // tp)
        y_att = latent_attention(q, k, v, mix_norm[i, D_CONV:D_CONV + D_ATTN], n_batch=nb, seq=seq,
                                 ctx_len=ctx_len, tq=ctx_len, ctx_queries=ctx_out, rows_out=rows_out)

        y_t = s5_chunked(proj, ma, mi, mo, a16, i, lat_chunks=lat_chunks, ctx_chunks=ctx_chunks, nb=nb)
        y_tok = y_t.transpose(1, 0, 2).reshape(r_all, D_SSM)
        y_ssm = s5_output(y_tok, proj, ssm_d[i], w_glu[i].astype(BF16), b_glu[i],
                          mix_norm[i, D_CONV + D_ATTN:], rows=rows_out, tm=tp)

        xs = matmul_gated_residual([y_conv, y_att, y_ssm], w_o[i].astype(BF16), xs, mod4, i, 2,
                                   tn=512, **tile_kw)
        hidden = norm_mod_matmul(xs, norm2_g[i], mod4, i, 3, 4,
                                 [w_gate[i].astype(BF16), w_up[i].astype(BF16)],
                                 tn=512, out_dtype=BF16, **tile_kw)
        xs = matmul_gated_residual([hidden], w_down[i].astype(BF16), xs, mod4, i, 5,
                                   tn=512, **tile_kw)

    out = final_rms_norm(xs, final_norm, rows=r_lat, tm=tp)
    return out.reshape(nb, seq, d)
```

```python
import functools
import math

import jax
import jax.numpy as jnp
from jax import lax
from jax.experimental import pallas as pl
from jax.experimental.pallas import tpu as pltpu

F32 = jnp.float32
BF16 = jnp.bfloat16

EPS = 1e-6
GRID_W = 64
CONV_W = 3
D_CONV = 512
D_SSM = 512
D_ATTN = 1024
MLA_HEADS = 8
QK_NOPE = 128
QK_ROPE = 64
V_HEAD = 128
Q_RANK = 512
KV_RANK = 256
ROPE_BASE = 10000.0
MLA_SCALE = (QK_NOPE + QK_ROPE) ** -0.5
SSM_GROUP = 16
SSM_GROUPS = 32
SSM_STATE = 64
CHUNK = 16

QK_PAD = 256
LANE = 128
TM = 1024
MIB = 1024 * 1024

COL_CONV = 0
COL_CQ = 3 * D_CONV
COL_U = COL_CQ + Q_RANK
COL_CKV = COL_U + D_SSM
COL_KR = COL_CKV + KV_RANK
N_PROJ = 3072


def _cparams(sem, vmem_mib):
    return pltpu.CompilerParams(dimension_semantics=sem, vmem_limit_bytes=vmem_mib * MIB)


def _rms(x, g):
    return x * lax.rsqrt(jnp.mean(x * x, axis=-1, keepdims=True) + EPS) * g


def _ada_kernel(c_ref, w_ref, b_ref, o_ref):
    cv = c_ref[...]
    s = (cv * jax.nn.sigmoid(cv)).astype(BF16)
    o_ref[...] = jnp.dot(s, w_ref[...].astype(BF16), preferred_element_type=F32) + b_ref[...]


def ada_modulation(cvec, w_ada, b_ada, tn=1024):
    depth, d, n = w_ada.shape
    rows = cvec.shape[0]
    return pl.pallas_call(
        _ada_kernel,
        grid=(depth, n // tn),
        in_specs=[
            pl.BlockSpec((rows, d), lambda l, j: (0, 0)),
            pl.BlockSpec((None, d, tn), lambda l, j: (l, 0, j)),
            pl.BlockSpec((None, 1, tn), lambda l, j: (l, 0, j)),
        ],
        out_specs=pl.BlockSpec((None, rows, tn), lambda l, j: (l, 0, j)),
        out_shape=jax.ShapeDtypeStruct((depth, rows, n), F32),
        compiler_params=_cparams(("arbitrary", "arbitrary"), 40),
    )(cvec, w_ada, b_ada.reshape(depth, 1, n))


def _in_proj_kernel(x_ref, g_ref, sh_ref, sc_ref, w_ref, o_ref, u_ref, a_scr, *, row_chunk, u_tile):
    tm = x_ref.shape[0]

    @pl.when(pl.program_id(1) == 0)
    def _():
        g = g_ref[...]
        sh = sh_ref[...]
        sc1 = 1.0 + sc_ref[...]
        for r in range(0, tm, row_chunk):
            x = x_ref[r:r + row_chunk, :]
            a_scr[r:r + row_chunk, :] = (_rms(x, g) * sc1 + sh).astype(BF16)

    acc = jnp.dot(a_scr[...], w_ref[...], preferred_element_type=F32)
    o_ref[...] = acc.astype(o_ref.dtype)

    @pl.when(pl.program_id(1) == u_tile)
    def _():
        u_ref[...] = acc


def input_projection(x, gain, mod4, layer, w, *, tiles_per_seq, n_batch, tn=512, tm=TM):
    m, d = x.shape
    n = w.shape[1]
    assert tn == D_SSM and COL_U % tn == 0

    def mod_spec(col):
        return pl.BlockSpec((None, None, 1, d),
                            lambda i, j: (layer, jnp.minimum(i // tiles_per_seq, n_batch), 0, col))

    return pl.pallas_call(
        functools.partial(_in_proj_kernel, row_chunk=min(256, tm), u_tile=COL_U // tn),
        grid=(m // tm, n // tn),
        in_specs=[
            pl.BlockSpec((tm, d), lambda i, j: (i, 0)),
            pl.BlockSpec((1, d), lambda i, j: (0, 0)),
            mod_spec(0),
            mod_spec(1),
            pl.BlockSpec((d, tn), lambda i, j: (0, j)),
        ],
        out_specs=[pl.BlockSpec((tm, tn), lambda i, j: (i, j)),
                   pl.BlockSpec((tm, tn), lambda i, j: (i, 0))],
        out_shape=[jax.ShapeDtypeStruct((m, n), BF16), jax.ShapeDtypeStruct((m, tn), F32)],
        scratch_shapes=[pltpu.VMEM((tm, d), BF16)],
        compiler_params=_cparams(("arbitrary", "arbitrary"), 48),
    )(x, gain.reshape(1, d), mod4, mod4, w)


def _merge_kernel(yc_ref, ya_ref, ys_ref, w_ref, x_ref, gate_ref, g_ref, sh_ref, sc_ref,
                  xo_ref, ho_ref, *, row_chunk):
    tm = x_ref.shape[0]
    kc, ka = yc_ref.shape[1], ya_ref.shape[1]
    gate = gate_ref[...]
    g = g_ref[...]
    sh = sh_ref[...]
    sc1 = 1.0 + sc_ref[...]
    for r in range(0, tm, row_chunk):
        rows = slice(r, r + row_chunk)
        acc = jnp.dot(yc_ref[rows, :], w_ref[0:kc, :], preferred_element_type=F32)
        acc = acc + jnp.dot(ya_ref[rows, :], w_ref[kc:kc + ka, :], preferred_element_type=F32)
        acc = acc + jnp.dot(ys_ref[rows, :], w_ref[kc + ka:, :], preferred_element_type=F32)
        x2 = x_ref[rows, :] + gate * acc
        xo_ref[rows, :] = x2
        ho_ref[rows, :] = (_rms(x2, g) * sc1 + sh).astype(ho_ref.dtype)


def merge_projection(y_conv, y_att, y_ssm, w_o, x, gain2, mod4, layer, *, m_tiles, tiles_per_seq,
                     n_batch, tm):
    d = x.shape[1]
    rows = m_tiles * tm

    def mod_spec(col):
        return pl.BlockSpec((None, None, 1, d),
                            lambda i: (layer, jnp.minimum(i // tiles_per_seq, n_batch), 0, col))

    a_spec = lambda a: pl.BlockSpec((tm, a.shape[1]), lambda i: (i, 0))
    return pl.pallas_call(
        functools.partial(_merge_kernel, row_chunk=min(256, tm)),
        grid=(m_tiles,),
        in_specs=[
            a_spec(y_conv), a_spec(y_att), a_spec(y_ssm),
            pl.BlockSpec((d, d), lambda i: (0, 0)),
            pl.BlockSpec((tm, d), lambda i: (i, 0)),
            mod_spec(2),
            pl.BlockSpec((1, d), lambda i: (0, 0)),
            mod_spec(3),
            mod_spec(4),
        ],
        out_specs=[pl.BlockSpec((tm, d), lambda i: (i, 0)), pl.BlockSpec((tm, d), lambda i: (i, 0))],
        out_shape=[jax.ShapeDtypeStruct((rows, d), F32), jax.ShapeDtypeStruct((rows, d), BF16)],
        compiler_params=_cparams(("arbitrary",), 52),
    )(y_conv, y_att, y_ssm, w_o, x, mod4, gain2.reshape(1, d), mod4, mod4)


def _cast_rows(src_ref, dst_ref, row_chunk):
    for r in range(0, src_ref.shape[0], row_chunk):
        dst_ref[r:r + row_chunk, :] = src_ref[r:r + row_chunk, :].astype(dst_ref.dtype)


def _ffn_up_kernel(h_ref, wg_ref, wu_ref, o_ref, wg_scr, wu_scr):
    @pl.when(pl.program_id(1) == 0)
    def _():
        _cast_rows(wg_ref, wg_scr, 512)
        _cast_rows(wu_ref, wu_scr, 512)

    h = h_ref[...]
    gt = jnp.dot(h, wg_scr[...], preferred_element_type=F32)
    up = jnp.dot(h, wu_scr[...], preferred_element_type=F32)
    o_ref[...] = (gt * jax.nn.sigmoid(gt) * up).astype(o_ref.dtype)


def ffn_up(h, w_gate, w_up, layer, *, m_tiles, tm, tn):
    d = h.shape[1]
    f = w_gate.shape[2]
    w_spec = pl.BlockSpec((None, d, tn), lambda j, i: (layer, 0, j))
    return pl.pallas_call(
        _ffn_up_kernel,
        grid=(f // tn, m_tiles),
        in_specs=[pl.BlockSpec((tm, d), lambda j, i: (i, 0)), w_spec, w_spec],
        out_specs=pl.BlockSpec((tm, tn), lambda j, i: (i, j)),
        out_shape=jax.ShapeDtypeStruct((m_tiles * tm, f), BF16),
        scratch_shapes=[pltpu.VMEM((d, tn), BF16)] * 2,
        compiler_params=_cparams(("arbitrary", "arbitrary"), 48),
    )(h, w_gate, w_up)


def _ffn_down_kernel(a_ref, w_ref, x_ref, gate_ref, o_ref, w_scr):
    @pl.when(pl.program_id(1) == 0)
    def _():
        _cast_rows(w_ref, w_scr, 512)

    acc = jnp.dot(a_ref[...], w_scr[...], preferred_element_type=F32)
    o_ref[...] = x_ref[...] + gate_ref[...] * acc


def ffn_down(a, w_down, x, mod4, layer, *, m_tiles, tiles_per_seq, n_batch, tm, tn):
    f = a.shape[1]
    d = x.shape[1]
    gate_blocks = d // tn
    return pl.pallas_call(
        _ffn_down_kernel,
        grid=(d // tn, m_tiles),
        in_specs=[
            pl.BlockSpec((tm, f), lambda j, i: (i, 0)),
            pl.BlockSpec((None, f, tn), lambda j, i: (layer, 0, j)),
            pl.BlockSpec((tm, tn), lambda j, i: (i, j)),
            pl.BlockSpec((None, None, 1, tn),
                         lambda j, i: (layer, jnp.minimum(i // tiles_per_seq, n_batch), 0,
                                       5 * gate_blocks + j)),
        ],
        out_specs=pl.BlockSpec((tm, tn), lambda j, i: (i, j)),
        out_shape=jax.ShapeDtypeStruct((m_tiles * tm, d), F32),
        scratch_shapes=[pltpu.VMEM((f, tn), BF16)],
        compiler_params=_cparams(("arbitrary", "arbitrary"), 56),
    )(a, w_down, x, mod4)


def _conv_kernel(h_ref, bg_ref, cg_ref, w_ref, gn_ref, *rest, row_chunk):
    o_ref, z_scr = rest[-2:]
    seq, ch = h_ref.shape
    zeros8 = jnp.zeros((8, ch), F32)
    z_scr[0:8, :] = zeros8
    z_scr[seq + 8:seq + 16, :] = zeros8
    for r in range(0, seq, row_chunk):
        z_scr[8 + r:8 + r + row_chunk, :] = (cg_ref[r:r + row_chunk, :].astype(F32)
                                             * h_ref[r:r + row_chunk, :].astype(F32))
    w0 = w_ref[0:1, :]
    w1 = w_ref[1:2, :]
    w2 = w_ref[2:3, :]
    gn = gn_ref[...]
    for r in range(0, seq, row_chunk):
        zp = z_scr[7 + r:7 + r + row_chunk, :]
        zc = z_scr[8 + r:8 + r + row_chunk, :]
        zn = z_scr[9 + r:9 + r + row_chunk, :]
        y = bg_ref[r:r + row_chunk, :].astype(F32) * (w0 * zp + w1 * zc + w2 * zn)
        o_ref[r:r + row_chunk, :] = _rms(y, gn).astype(o_ref.dtype)


def conv_mixer(proj, conv_w, gn, *, seq, n_seq, row_block0, rows_out, into=None):
    ch = D_CONV
    in_specs = [pl.BlockSpec((seq, ch), functools.partial(lambda s, c: (row_block0 + s, c), c=c))
                for c in range(3)]
    in_specs += [pl.BlockSpec((CONV_W, ch), lambda s: (0, 0)), pl.BlockSpec((1, ch), lambda s: (0, 0))]
    args = [proj, proj, proj, conv_w, gn.reshape(1, ch)]
    aliases = {}
    if into is not None:
        in_specs.append(pl.BlockSpec(memory_space=pl.ANY))
        args.append(into)
        aliases = {len(args) - 1: 0}
    return pl.pallas_call(
        functools.partial(_conv_kernel, row_chunk=min(256, seq)),
        grid=(n_seq,),
        in_specs=in_specs,
        out_specs=pl.BlockSpec((seq, ch), lambda s: (row_block0 + s, 0)),
        out_shape=jax.ShapeDtypeStruct((rows_out, ch), BF16),
        scratch_shapes=[pltpu.VMEM((seq + 16, ch), F32)],
        input_output_aliases=aliases,
        compiler_params=_cparams(("arbitrary",), 48),
    )(*args)


def _q_proj_kernel(cq_ref, g_ref, w_ref, cos_ref, sin_ref, o_ref):
    a = _rms(cq_ref[...].astype(F32), g_ref[...]).astype(BF16)
    q = jnp.dot(a, w_ref[...], preferred_element_type=F32)
    cos = cos_ref[...]
    sin = sin_ref[...]
    for h in range(MLA_HEADS):
        c0 = h * QK_PAD
        o_ref[:, c0:c0 + QK_NOPE] = (q[:, c0:c0 + QK_NOPE] * MLA_SCALE).astype(o_ref.dtype)
        blk = q[:, c0 + QK_NOPE:c0 + QK_PAD]
        rot = blk * cos + pltpu.roll(blk, QK_ROPE, axis=1) * sin
        o_ref[:, c0 + QK_NOPE:c0 + QK_PAD] = (rot * MLA_SCALE).astype(o_ref.dtype)


def q_projection(proj, q_norm, w_q, cos_t, sin_t, *, rows, tm, lat_tiles, tiles_per_seq):
    tab_map = lambda i: (jnp.where(i < lat_tiles, i % tiles_per_seq, tiles_per_seq), 0)
    n = MLA_HEADS * QK_PAD
    return pl.pallas_call(
        _q_proj_kernel,
        grid=(rows // tm,),
        in_specs=[
            pl.BlockSpec((tm, Q_RANK), lambda i: (i, COL_CQ // Q_RANK)),
            pl.BlockSpec((1, Q_RANK), lambda i: (0, 0)),
            pl.BlockSpec((Q_RANK, n), lambda i: (0, 0)),
            pl.BlockSpec((tm, LANE), tab_map),
            pl.BlockSpec((tm, LANE), tab_map),
        ],
        out_specs=pl.BlockSpec((tm, n), lambda i: (i, 0)),
        out_shape=jax.ShapeDtypeStruct((rows, n), BF16),
        compiler_params=_cparams(("arbitrary",), 40),
    )(proj, q_norm.reshape(1, Q_RANK), w_q, cos_t, sin_t)


def _kv_proj_kernel(ckv_ref, kr_ref, g_ref, w_ref, cos_ref, sin_ref, k_ref, v_ref):
    a = _rms(ckv_ref[...].astype(F32), g_ref[...]).astype(BF16)
    kv = jnp.dot(a, w_ref[...], preferred_element_type=F32)
    blk = kr_ref[...].astype(F32)
    rot = (blk * cos_ref[...] + pltpu.roll(blk, QK_ROPE, axis=1) * sin_ref[...]).astype(k_ref.dtype)
    nk = MLA_HEADS * QK_NOPE
    for h in range(MLA_HEADS):
        c0 = h * QK_PAD
        k_ref[:, c0:c0 + QK_NOPE] = kv[:, h * QK_NOPE:(h + 1) * QK_NOPE].astype(k_ref.dtype)
        k_ref[:, c0 + QK_NOPE:c0 + QK_PAD] = rot
    v_ref[...] = kv[:, nk:].astype(v_ref.dtype)


def kv_projection(proj, kv_norm, w_kv, cos_t, sin_t, *, rows, tm, lat_tiles, tiles_per_seq):
    tab_map = lambda i: (jnp.where(i < lat_tiles, i % tiles_per_seq, tiles_per_seq), 0)
    nk = MLA_HEADS * QK_PAD
    nv = MLA_HEADS * V_HEAD
    return pl.pallas_call(
        _kv_proj_kernel,
        grid=(rows // tm,),
        in_specs=[
            pl.BlockSpec((tm, KV_RANK), lambda i: (i, COL_CKV // KV_RANK)),
            pl.BlockSpec((tm, LANE), lambda i: (i, COL_KR // LANE)),
            pl.BlockSpec((1, KV_RANK), lambda i: (0, 0)),
            pl.BlockSpec((KV_RANK, MLA_HEADS * (QK_NOPE + V_HEAD)), lambda i: (0, 0)),
            pl.BlockSpec((tm, LANE), tab_map),
            pl.BlockSpec((tm, LANE), tab_map),
        ],
        out_specs=[pl.BlockSpec((tm, nk), lambda i: (i, 0)), pl.BlockSpec((tm, nv), lambda i: (i, 0))],
        out_shape=[jax.ShapeDtypeStruct((rows, nk), BF16), jax.ShapeDtypeStruct((rows, nv), BF16)],
        compiler_params=_cparams(("arbitrary",), 40),
    )(proj, proj, kv_norm.reshape(1, KV_RANK), w_kv, cos_t, sin_t)


_NT = (((1,), (1,)), ((), ()))


def _attn_kernel(q_ref, kl_ref, kc_ref, vl_ref, vc_ref, gn_ref, o_ref, acc_scr, *, lat_tiles):
    t = pl.program_id(1)

    def heads(with_latent):
        for h in range(MLA_HEADS):
            q = q_ref[:, h * QK_PAD:(h + 1) * QK_PAD]
            sc = lax.dot_general(q, kc_ref[:, h * QK_PAD:(h + 1) * QK_PAD], _NT,
                                 preferred_element_type=F32)
            m = jnp.max(sc, axis=-1, keepdims=True)
            if with_latent:
                sl = lax.dot_general(q, kl_ref[:, h * QK_PAD:(h + 1) * QK_PAD], _NT,
                                     preferred_element_type=F32)
                m = jnp.maximum(m, jnp.max(sl, axis=-1, keepdims=True))
                pl_ = jnp.exp(sl - m)
            pc = jnp.exp(sc - m)
            den = jnp.sum(pc, axis=-1, keepdims=True)
            o = jnp.dot(pc.astype(BF16), vc_ref[:, h * V_HEAD:(h + 1) * V_HEAD],
                        preferred_element_type=F32)
            if with_latent:
                den = den + jnp.sum(pl_, axis=-1, keepdims=True)
                o = o + jnp.dot(pl_.astype(BF16), vl_ref[:, h * V_HEAD:(h + 1) * V_HEAD],
                                preferred_element_type=F32)
            acc_scr[:, h * V_HEAD:(h + 1) * V_HEAD] = o / den

    @pl.when(t < lat_tiles)
    def _():
        heads(True)

    @pl.when(t >= lat_tiles)
    def _():
        heads(False)

    o_ref[...] = _rms(acc_scr[...], gn_ref[...]).astype(o_ref.dtype)


def latent_attention(q, k, v, gn, *, n_batch, seq, ctx_len, tq, ctx_queries, rows_out):
    lat_tiles = seq // tq
    assert ctx_len == tq
    q_tiles = lat_tiles + (1 if ctx_queries else 0)
    ctx_blk0 = n_batch * seq // ctx_len
    nq = MLA_HEADS * QK_PAD
    nv = MLA_HEADS * V_HEAD
    qmap = lambda b, t: (jnp.where(t < lat_tiles, b * lat_tiles + t, ctx_blk0 + b), 0)
    return pl.pallas_call(
        functools.partial(_attn_kernel, lat_tiles=lat_tiles),
        grid=(n_batch, q_tiles),
        in_specs=[
            pl.BlockSpec((tq, nq), qmap),
            pl.BlockSpec((seq, nq), lambda b, t: (b, 0)),
            pl.BlockSpec((ctx_len, nq), lambda b, t: (ctx_blk0 + b, 0)),
            pl.BlockSpec((seq, nv), lambda b, t: (b, 0)),
            pl.BlockSpec((ctx_len, nv), lambda b, t: (ctx_blk0 + b, 0)),
            pl.BlockSpec((1, nv), lambda b, t: (0, 0)),
        ],
        out_specs=pl.BlockSpec((tq, nv), qmap),
        out_shape=jax.ShapeDtypeStruct((rows_out, nv), BF16),
        scratch_shapes=[pltpu.VMEM((tq, nv), F32)],
        compiler_params=_cparams(("arbitrary", "arbitrary"), 56),
    )(q, k, k, v, v, gn.reshape(1, nv))


def _s5_disc_kernel(are_ref, aim_ref, ldt_ref, abr_ref, abi_ref, cfr_ref, cfi_ref):
    ar = are_ref[...]
    ai = aim_ref[...]
    dt = jnp.exp(ldt_ref[...])
    mag = jnp.exp(ar * dt)
    th = ai * dt
    br = mag * jnp.cos(th)
    bi = mag * jnp.sin(th)
    nr = br - 1.0
    den = ar * ar + ai * ai
    abr_ref[...] = br
    abi_ref[...] = bi
    cfr_ref[...] = (nr * ar + bi * ai) / den
    cfi_ref[...] = (bi * ar - nr * ai) / den


def s5_discretise(a_re, a_im, log_dt):
    shp = a_re.shape
    rows = shp[0] * shp[1] * shp[2]
    flat = lambda t: t.reshape(rows, shp[3])
    ldt = jnp.broadcast_to(log_dt[..., None], shp)
    outs = pl.pallas_call(
        _s5_disc_kernel,
        out_shape=[jax.ShapeDtypeStruct((rows, shp[3]), F32)] * 4,
    )(flat(a_re), flat(a_im), flat(ldt))
    return [o.reshape(shp) for o in outs]


def _cmul(xr, xi, yr, yi):
    return xr * yr - xi * yi, xr * yi + xi * yr


def _s5_mats_kernel(abc_re, abc_im, cex_re, cex_im, afb_re, afb_im, cfb_re, cfb_im, bfb_re, bfb_im,
                    cpp_re, cpp_im, bx_ref, by_ref, ma_ref, mi_ref, mo_ref, a16_ref):
    tw = CHUNK * SSM_GROUP
    p = SSM_STATE
    lane = lax.broadcasted_iota(jnp.int32, (p, tw), 1)
    tblk = lane // SSM_GROUP
    lane16 = lax.broadcasted_iota(jnp.int32, (SSM_GROUP, tw), 1)
    lane128 = lax.broadcasted_iota(jnp.int32, (1, 2 * p), 1)
    fwd_lanes = lane128 < p
    sgn = jnp.where(lax.broadcasted_iota(jnp.int32, (SSM_GROUP, 2 * p), 1) < p, 1.0, -1.0)
    zeros = jnp.zeros((p, tw), F32)
    mi_blocks = []
    for d in range(2):
        ar = jnp.broadcast_to(abc_re[d], (p, tw))
        ai = jnp.broadcast_to(abc_im[d], (p, tw))
        sel_r = tblk if d == 0 else (CHUNK - 1) - tblk
        sel_e = tblk + 1 if d == 0 else CHUNK - tblk
        pr, pi = jnp.ones((p, tw), F32), zeros
        e_re = e_im = r_re = r_im = zeros
        for k in range(CHUNK + 1):
            if k < CHUNK:
                r_re = jnp.where(sel_r == k, pr, r_re)
                r_im = jnp.where(sel_r == k, pi, r_im)
            if k >= 1:
                e_re = jnp.where(sel_e == k, pr, e_re)
                e_im = jnp.where(sel_e == k, pi, e_im)
            if k < CHUNK:
                pr, pi = _cmul(pr, pi, ar, ai)
        cr = cex_re[d]
        ci = cex_im[d]
        w_re, w_im = _cmul(cr, ci, e_re, e_im)
        base = d * 4 * p
        mdt = mo_ref.dtype
        zeros_m = zeros.astype(mdt)
        if d == 0:
            mo_ref[base:base + p, :] = w_re.astype(mdt)
            mo_ref[base + p:base + 2 * p, :] = zeros_m
            mo_ref[base + 2 * p:base + 3 * p, :] = (-w_im).astype(mdt)
            mo_ref[base + 3 * p:base + 4 * p, :] = zeros_m
        else:
            mo_ref[base:base + p, :] = zeros_m
            mo_ref[base + p:base + 2 * p, :] = w_re.astype(mdt)
            mo_ref[base + 2 * p:base + 3 * p, :] = zeros_m
            mo_ref[base + 3 * p:base + 4 * p, :] = (-w_im).astype(mdt)
        rr_re, rr_im = _cmul(cr, ci, r_re, r_im)
        stacked = jnp.concatenate([rr_re, rr_im], axis=0)
        lm = sgn * cpp_re[d] * bx_ref[d] - cpp_im[d] * by_ref[d]
        kall = jnp.dot(lm, stacked, preferred_element_type=F32, precision=lax.Precision.HIGHEST)
        for s in range(CHUNK):
            if d == 0:
                shift = SSM_GROUP * s
                keep = lane16 >= SSM_GROUP * s
            else:
                shift = (SSM_GROUP * (s + 1)) % tw
                keep = lane16 < SSM_GROUP * (s + 1)
            rolled = pltpu.roll(kall, shift, axis=1) if shift else kall
            blk = jnp.where(keep, rolled, 0.0)
            if d == 0:
                mi_blocks.append(blk)
            else:
                mi_ref[SSM_GROUP * s:SSM_GROUP * (s + 1), :] = (mi_blocks[s] + blk).astype(mi_ref.dtype)

    a_re = afb_re[...]
    a_im = afb_im[...]
    cf_re = cfb_re[...]
    cf_im = cfb_im[...]
    b_re = bfb_re[...]
    b_im = bfb_im[...]
    powers = []
    qr, qi = jnp.ones((1, 2 * p), F32), jnp.zeros((1, 2 * p), F32)
    for k in range(CHUNK + 1):
        powers.append((qr, qi))
        if k < CHUNK:
            qr, qi = _cmul(qr, qi, a_re, a_im)
    for s in range(CHUNK):
        g_re = jnp.where(fwd_lanes, powers[CHUNK - 1 - s][0], powers[s][0])
        g_im = jnp.where(fwd_lanes, powers[CHUNK - 1 - s][1], powers[s][1])
        g_re, g_im = _cmul(g_re, g_im, cf_re, cf_im)
        rows = slice(SSM_GROUP * s, SSM_GROUP * (s + 1))
        ma_ref[rows, 0:2 * p] = (g_re * b_re - g_im * b_im).astype(ma_ref.dtype)
        ma_ref[rows, 2 * p:4 * p] = (g_re * b_im + g_im * b_re).astype(ma_ref.dtype)
    a16_ref[0:1, :] = powers[CHUNK][0]
    a16_ref[1:2, :] = powers[CHUNK][1]


def s5_matrices(abar_re, abar_im, coef_re, coef_im, b_re, b_im, c_re, c_im):
    depth, _, g, p = abar_re.shape
    hg = b_re.shape[-1]
    tw = CHUNK * hg
    t0213 = lambda t: t.transpose(0, 2, 1, 3)
    abc = [t0213(t)[..., None] for t in (abar_re, abar_im)]
    cex = [jnp.tile(t.transpose(0, 2, 1, 4, 3), (1, 1, 1, 1, CHUNK)) for t in (c_re, c_im)]
    fb = lambda t: t0213(t).reshape(depth, g, 1, 2 * p)
    afb = [fb(t) for t in (abar_re, abar_im)]
    cfb = [fb(t) for t in (coef_re, coef_im)]
    bfb = [t.transpose(0, 2, 4, 1, 3).reshape(depth, g, hg, 2 * p) for t in (b_re, b_im)]
    cpp = [t0213(jnp.concatenate([t, t], axis=-1))[:, :, :, None, :] for t in (coef_re, coef_im)]
    bt_re = b_re.transpose(0, 2, 1, 4, 3)
    bt_im = b_im.transpose(0, 2, 1, 4, 3)
    bx = jnp.concatenate([bt_re, bt_im], axis=-1)
    by = jnp.concatenate([bt_im, bt_re], axis=-1)
    ins = [*abc, *cex, *afb, *cfb, *bfb, *cpp, bx, by]

    def spec(t):
        blk = (None, None) + t.shape[2:]
        nz = len(t.shape) - 2
        return pl.BlockSpec(blk, lambda l, gi: (l, gi) + (0,) * nz)

    def ospec(r, c):
        return pl.BlockSpec((None, None, r, c), lambda l, gi: (l, gi, 0, 0))

    return pl.pallas_call(
        _s5_mats_kernel,
        grid=(depth, g),
        in_specs=[spec(t) for t in ins],
        out_specs=[ospec(tw, 4 * p), ospec(tw, tw), ospec(8 * p, tw), ospec(2, 2 * p)],
        out_shape=[
            jax.ShapeDtypeStruct((depth, g, tw, 4 * p), BF16),
            jax.ShapeDtypeStruct((depth, g, tw, tw), BF16),
            jax.ShapeDtypeStruct((depth, g, 8 * p, tw), BF16),
            jax.ShapeDtypeStruct((depth, g, 2, 2 * p), F32),
        ],
        compiler_params=_cparams(("arbitrary", "arbitrary"), 32),
    )(*ins)


GROUP_BLOCK = LANE // SSM_GROUP


def lane_swap_matrix():
    idx = jnp.arange(GROUP_BLOCK * LANE)
    a, b, c = idx // LANE, (idx // SSM_GROUP) % GROUP_BLOCK, idx % SSM_GROUP
    dst = b * LANE + a * SSM_GROUP + c
    return (dst[:, None] == idx[None, :]).astype(BF16)


def _s5_main_kernel(*refs, order_f, order_b, nb, lat_chunks, ctx_chunks):
    (u_ref, p_ref, ma_ref, mi_ref, mo_ref, a_ref, y_ref, ug_scr, yg_scr,
     xr_scr, xi_scr, fr_scr, fi_scr, br_scr, bi_scr) = refs
    p2 = 2 * SSM_STATE
    half = CHUNK // 2
    rows = u_ref.shape[0] // CHUNK
    n_sub = xr_scr.shape[0]

    for th in range(2):
        slab = jnp.concatenate(
            [u_ref[pl.ds(half * th + tl, rows, stride=CHUNK), :].astype(BF16) for tl in range(half)], axis=1)
        perm = jnp.dot(slab, p_ref[...], preferred_element_type=F32).astype(BF16)
        for g in range(GROUP_BLOCK):
            ug_scr[g, :, th * LANE:(th + 1) * LANE] = perm[:, g * LANE:(g + 1) * LANE]

    def rows_of(j):
        if j < lat_chunks:
            return pl.ds(j, nb, stride=lat_chunks)
        return pl.ds(nb * lat_chunks + (j - lat_chunks), nb, stride=ctx_chunks)

    fwd_lanes = lax.broadcasted_iota(jnp.int32, (nb, p2), 1) < SSM_STATE
    for g0 in range(0, GROUP_BLOCK, n_sub):
        for k in range(n_sub):
            x = jnp.dot(ug_scr[g0 + k], ma_ref[g0 + k], preferred_element_type=F32)
            xr_scr[k] = x[:, 0:p2]
            xi_scr[k] = x[:, p2:2 * p2]

        a_re = [a_ref[g0 + k, 0:1, :] for k in range(n_sub)]
        a_im = [a_ref[g0 + k, 1:2, :] for k in range(n_sub)]
        s_re = [jnp.zeros((nb, p2), F32) for _ in range(n_sub)]
        s_im = [jnp.zeros((nb, p2), F32) for _ in range(n_sub)]
        for jf, jb in zip(order_f, order_b):
            rf = rows_of(jf)
            rb = rows_of(jb)
            for k in range(n_sub):
                fr_scr[k, rf, :] = s_re[k]
                fi_scr[k, rf, :] = s_im[k]
                br_scr[k, rb, :] = s_re[k]
                bi_scr[k, rb, :] = s_im[k]
                xr = jnp.where(fwd_lanes, xr_scr[k, rf, :], xr_scr[k, rb, :])
                xi = jnp.where(fwd_lanes, xi_scr[k, rf, :], xi_scr[k, rb, :])
                s_re[k], s_im[k] = (a_re[k] * s_re[k] - a_im[k] * s_im[k] + xr,
                                    a_re[k] * s_im[k] + a_im[k] * s_re[k] + xi)

        for k in range(n_sub):
            g = g0 + k
            y = jnp.dot(ug_scr[g], mi_ref[g], preferred_element_type=F32)
            sp_f = jnp.concatenate([fr_scr[k], fi_scr[k]], axis=1).astype(BF16)
            sp_b = jnp.concatenate([br_scr[k], bi_scr[k]], axis=1).astype(BF16)
            y = y + jnp.dot(sp_f, mo_ref[g, 0:2 * p2, :], preferred_element_type=F32)
            y = y + jnp.dot(sp_b, mo_ref[g, 2 * p2:4 * p2, :], preferred_element_type=F32)
            yg_scr[g] = y.astype(BF16)

    for th in range(2):
        slab = jnp.concatenate([yg_scr[g, :, th * LANE:(th + 1) * LANE] for g in range(GROUP_BLOCK)], axis=1)
        perm = jnp.dot(slab, p_ref[...], preferred_element_type=F32)
        for tl in range(half):
            y_ref[pl.ds(half * th + tl, rows, stride=CHUNK), :] = perm[:, tl * LANE:(tl + 1) * LANE]


def s5_chunked(u, ma, mi, mo, a16, layer, *, lat_chunks, ctx_chunks, nb):
    r_all = u.shape[0]
    rows = r_all // CHUNK
    tw = CHUNK * SSM_GROUP
    j_tot = lat_chunks + ctx_chunks
    order_f = tuple(range(lat_chunks, j_tot)) + tuple(range(lat_chunks))
    order_b = tuple(range(j_tot - 1, lat_chunks - 1, -1)) + tuple(range(lat_chunks - 1, -1, -1))
    p4 = 4 * SSM_STATE
    n_gb = SSM_GROUPS // GROUP_BLOCK
    n_sub = GROUP_BLOCK // 2

    def wspec(r, c):
        return pl.BlockSpec((None, GROUP_BLOCK, r, c), lambda gb: (layer, gb, 0, 0))

    return pl.pallas_call(
        functools.partial(_s5_main_kernel, order_f=order_f, order_b=order_b, nb=nb,
                          lat_chunks=lat_chunks, ctx_chunks=ctx_chunks),
        grid=(n_gb,),
        in_specs=[
            pl.BlockSpec((r_all, LANE), lambda gb: (0, gb)),
            pl.BlockSpec((GROUP_BLOCK * LANE, GROUP_BLOCK * LANE), lambda gb: (0, 0)),
            wspec(tw, p4), wspec(tw, tw), wspec(2 * p4, tw), wspec(2, 2 * SSM_STATE),
        ],
        out_specs=pl.BlockSpec((r_all, LANE), lambda gb: (0, gb)),
        out_shape=jax.ShapeDtypeStruct((r_all, D_SSM), F32),
        scratch_shapes=[
            pltpu.VMEM((GROUP_BLOCK, rows, tw), BF16),
            pltpu.VMEM((GROUP_BLOCK, rows, tw), BF16),
        ] + [pltpu.VMEM((n_sub, rows, 2 * SSM_STATE), F32)] * 6,
        compiler_params=_cparams(("arbitrary",), 56),
    )(u, lane_swap_matrix(), ma, mi, mo, a16)


def _s5_out_kernel(y_ref, u_ref, d_ref, w_ref, b_ref, gn_ref, o_ref):
    y = y_ref[...].astype(F32) + d_ref[...] * u_ref[...].astype(F32)
    g = jax.nn.gelu(y)
    z = jnp.dot(g.astype(BF16), w_ref[...], preferred_element_type=F32) + b_ref[...]
    o_ref[...] = _rms(g * jax.nn.sigmoid(z), gn_ref[...]).astype(o_ref.dtype)


def s5_output(y, u, ssm_d, w_glu, b_glu, gn, *, rows, tm):
    ch = D_SSM
    vec = lambda: pl.BlockSpec((1, ch), lambda i: (0, 0))
    return pl.pallas_call(
        _s5_out_kernel,
        grid=(rows // tm,),
        in_specs=[
            pl.BlockSpec((tm, ch), lambda i: (i, 0)),
            pl.BlockSpec((tm, ch), lambda i: (i, 0)),
            vec(),
            pl.BlockSpec((ch, ch), lambda i: (0, 0)),
            vec(), vec(),
        ],
        out_specs=pl.BlockSpec((tm, ch), lambda i: (i, 0)),
        out_shape=jax.ShapeDtypeStruct((rows, ch), BF16),
        compiler_params=_cparams(("arbitrary",), 32),
    )(y, u, ssm_d.reshape(1, ch), w_glu, b_glu.reshape(1, ch), gn.reshape(1, ch))


def _final_norm_kernel(x_ref, g_ref, o_ref):
    o_ref[...] = _rms(x_ref[...], g_ref[...])


def final_rms_norm(x, g, *, rows, tm):
    d = x.shape[1]
    return pl.pallas_call(
        _final_norm_kernel,
        grid=(rows // tm,),
        in_specs=[pl.BlockSpec((tm, d), lambda i: (i, 0)), pl.BlockSpec((1, d), lambda i: (0, 0))],
        out_specs=pl.BlockSpec((tm, d), lambda i: (i, 0)),
        out_shape=jax.ShapeDtypeStruct((rows, d), F32),
        compiler_params=_cparams(("arbitrary",), 32),
    )(x, g.reshape(1, d))


def _rope_partner_perm():
    idx = []
    for i in range(QK_ROPE):
        idx.append(i + 16 if (i % 32) < 16 else i - 16)
    return jnp.asarray(idx, jnp.int32)


def rope_tables(seq, tab_tile):
    rows = seq // GRID_W
    row = jnp.repeat(jnp.arange(rows, dtype=F32), GRID_W)
    col = jnp.tile(jnp.arange(GRID_W, dtype=F32), rows)
    n_freq = QK_ROPE // 4
    inv = ROPE_BASE ** (-jnp.arange(n_freq, dtype=F32) / n_freq)
    ar = row[:, None] * inv
    ac = col[:, None] * inv
    cos = jnp.concatenate([jnp.cos(ar), jnp.cos(ar), jnp.cos(ac), jnp.cos(ac)], axis=1)
    sin = jnp.concatenate([-jnp.sin(ar), jnp.sin(ar), -jnp.sin(ac), jnp.sin(ac)], axis=1)
    cos = jnp.concatenate([cos, jnp.ones((tab_tile, QK_ROPE), F32)], axis=0)
    sin = jnp.concatenate([sin, jnp.zeros((tab_tile, QK_ROPE), F32)], axis=0)
    pad = jnp.zeros((seq + tab_tile, LANE - QK_ROPE), F32)
    return jnp.concatenate([cos, pad], axis=1), jnp.concatenate([sin, pad], axis=1)


def prep_w_in(w_in):
    d = w_in.shape[0]
    s = [0, 512, 1024, 1536, 2048, 2304, 2368, 2880]
    hconv, bg, cg, cq, ckv, kr, u = [w_in[:, s[i]:s[i + 1]] for i in range(7)]
    krp = kr[:, _rope_partner_perm()]
    pad = jnp.zeros((d, N_PROJ - (COL_KR + LANE)), w_in.dtype)
    return jnp.concatenate([hconv, bg, cg, cq, u, ckv, kr, krp, pad], axis=1).astype(BF16)


def prep_w_uq(w_uq):
    r = w_uq.shape[0]
    w = w_uq.reshape(r, MLA_HEADS, QK_NOPE + QK_ROPE)
    rope = w[:, :, QK_NOPE:]
    w = jnp.concatenate([w, rope[:, :, _rope_partner_perm()]], axis=-1)
    return w.reshape(r, MLA_HEADS * QK_PAD).astype(BF16)


def prep_w_ukv(w_ukv):
    r = w_ukv.shape[0]
    w = w_ukv.reshape(r, MLA_HEADS, QK_NOPE + V_HEAD)
    kn = w[:, :, :QK_NOPE].reshape(r, MLA_HEADS * QK_NOPE)
    v = w[:, :, QK_NOPE:].reshape(r, MLA_HEADS * V_HEAD)
    return jnp.concatenate([kn, v], axis=1).astype(BF16)


def kernel(x, c, ctx, c_ctx, w_ada, b_ada, norm1_g, norm2_g, w_in, conv_w, mla_q_norm, w_uq, mla_kv_norm, w_ukv, ssm_a_re, ssm_a_im, ssm_log_dt, ssm_b_re, ssm_b_im, ssm_c_re, ssm_c_im, ssm_d, w_glu, b_glu, mix_norm, w_o, w_gate, w_up, w_down, final_norm):
    nb, seq, d = x.shape
    ctx_len = ctx.shape[1]
    depth = w_ada.shape[0]
    r_lat = nb * seq
    r_ctx = nb * ctx_len
    r_all = r_lat + r_ctx
    assert seq % TM == 0 and r_ctx == TM and seq % CHUNK == 0 and ctx_len % CHUNK == 0
    tiles_per_seq = seq // TM
    lat_tiles_m = r_lat // TM
    d_ff = w_gate.shape[2]

    xs = jnp.concatenate([x.reshape(r_lat, d), ctx.reshape(r_ctx, d)], axis=0)

    cvec = jnp.concatenate([c, c_ctx[None, :], jnp.zeros((8 - nb - 1, d), F32)], axis=0)
    mod = ada_modulation(cvec, w_ada, b_ada)
    mod4 = mod.reshape(depth, 8, 1, 6 * d)

    tp = 512
    cos_t, sin_t = rope_tables(seq, tp)
    abar_re, abar_im, coef_re, coef_im = s5_discretise(ssm_a_re, ssm_a_im, ssm_log_dt)
    ma, mi, mo, a16 = s5_matrices(abar_re, abar_im, coef_re, coef_im,
                                  ssm_b_re, ssm_b_im, ssm_c_re, ssm_c_im)
    lat_chunks = seq // CHUNK
    ctx_chunks = ctx_len // CHUNK
    tw = CHUNK * SSM_GROUP

    for i in range(depth):
        ctx_out = i < depth - 1
        m_tiles = lat_tiles_m + (1 if ctx_out else 0)
        rows_out = r_all if ctx_out else r_lat
        tf = 512

        proj, u32 = input_projection(xs, norm1_g[i], mod4, i, prep_w_in(w_in[i]),
                                     tiles_per_seq=tiles_per_seq, n_batch=nb)

        y_conv = conv_mixer(proj, conv_w[i], mix_norm[i, :D_CONV], seq=seq, n_seq=nb,
                            row_block0=0, rows_out=rows_out)
        if ctx_out:
            y_conv = conv_mixer(proj, conv_w[i], mix_norm[i, :D_CONV], seq=ctx_len, n_seq=nb,
                                row_block0=r_lat // ctx_len, rows_out=rows_out, into=y_conv)

        q_rows = rows_out
        q = q_projection(proj, mla_q_norm[i], prep_w_uq(w_uq[i]), cos_t, sin_t, rows=q_rows, tm=tp,
                         lat_tiles=r_lat // tp, tiles_per_seq=seq // tp)
        k, v = kv_projection(proj, mla_kv_norm[i], prep_w_ukv(w_ukv[i]), cos_t, sin_t, rows=r_all,
                             tm=tp, lat_tiles=r_lat // tp, tiles_per_seq=seq // tp)
        y_att = latent_attention(q, k, v, mix_norm[i, D_CONV:D_CONV + D_ATTN], n_batch=nb, seq=seq,
                                 ctx_len=ctx_len, tq=ctx_len, ctx_queries=ctx_out, rows_out=rows_out)

        y_tok = s5_chunked(u32, ma, mi, mo, a16, i, lat_chunks=lat_chunks, ctx_chunks=ctx_chunks, nb=nb)
        y_ssm = s5_output(y_tok, u32, ssm_d[i], w_glu[i].astype(BF16), b_glu[i],
                          mix_norm[i, D_CONV + D_ATTN:], rows=rows_out, tm=tp)

        x2, h2 = merge_projection(y_conv, y_att, y_ssm, w_o[i].astype(BF16), xs, norm2_g[i], mod4, i,
                                  m_tiles=rows_out // tf, tiles_per_seq=seq // tf, n_batch=nb, tm=tf)
        hidden = ffn_up(h2, w_gate, w_up, i, m_tiles=m_tiles, tm=TM, tn=512)
        xs = ffn_down(hidden, w_down, x2, mod4, i, m_tiles=rows_out // tf, tiles_per_seq=seq // tf,
                      n_batch=nb, tm=tf, tn=512)

    out = final_rms_norm(xs, final_norm, rows=r_lat, tm=tp)
    return out.reshape(nb, seq, d)
```

```python
import functools
import math

import jax
import jax.numpy as jnp
from jax import lax
from jax.experimental import pallas as pl
from jax.experimental.pallas import tpu as pltpu

F32 = jnp.float32
BF16 = jnp.bfloat16

EPS = 1e-6
GRID_W = 64
CONV_W = 3
D_CONV = 512
D_SSM = 512
D_ATTN = 1024
MLA_HEADS = 8
QK_NOPE = 128
QK_ROPE = 64
V_HEAD = 128
Q_RANK = 512
KV_RANK = 256
ROPE_BASE = 10000.0
MLA_SCALE = (QK_NOPE + QK_ROPE) ** -0.5
SSM_GROUP = 16
SSM_GROUPS = 32
SSM_STATE = 64
CHUNK = 16

QK_PAD = 256
V_PAD = 256
LOG2E = math.log2(math.e)
LANE = 128
TM = 1024
MIB = 1024 * 1024

COL_CONV = 0
COL_CQ = 3 * D_CONV
COL_U = COL_CQ + Q_RANK
COL_CKV = COL_U + D_SSM
COL_KR = COL_CKV + KV_RANK
N_PROJ = 3072


def _cparams(sem, vmem_mib):
    return pltpu.CompilerParams(dimension_semantics=sem, vmem_limit_bytes=vmem_mib * MIB)


def _rms(x, g):
    return x * lax.rsqrt(jnp.mean(x * x, axis=-1, keepdims=True) + EPS) * g


def _ada_kernel(c_ref, w_ref, b_ref, o_ref):
    cv = c_ref[...]
    s = (cv * jax.nn.sigmoid(cv)).astype(BF16)
    o_ref[...] = jnp.dot(s, w_ref[...].astype(BF16), preferred_element_type=F32) + b_ref[...]


def ada_modulation(cvec, w_ada, b_ada, tn=1024):
    depth, d, n = w_ada.shape
    rows = cvec.shape[0]
    return pl.pallas_call(
        _ada_kernel,
        grid=(depth, n // tn),
        in_specs=[
            pl.BlockSpec((rows, d), lambda l, j: (0, 0)),
            pl.BlockSpec((None, d, tn), lambda l, j: (l, 0, j)),
            pl.BlockSpec((None, 1, tn), lambda l, j: (l, 0, j)),
        ],
        out_specs=pl.BlockSpec((None, rows, tn), lambda l, j: (l, 0, j)),
        out_shape=jax.ShapeDtypeStruct((depth, rows, n), F32),
        compiler_params=_cparams(("arbitrary", "arbitrary"), 40),
    )(cvec, w_ada, b_ada.reshape(depth, 1, n))


def _in_proj_kernel(x_ref, g_ref, sh_ref, sc_ref, w_ref, o_ref, u_ref, a_scr, *, row_chunk, u_tile):
    tm = x_ref.shape[0]

    @pl.when(pl.program_id(1) == 0)
    def _():
        g = g_ref[...]
        sh = sh_ref[...]
        sc1 = 1.0 + sc_ref[...]
        for r in range(0, tm, row_chunk):
            x = x_ref[r:r + row_chunk, :]
            a_scr[r:r + row_chunk, :] = (_rms(x, g) * sc1 + sh).astype(BF16)

    acc = jnp.dot(a_scr[...], w_ref[...], preferred_element_type=F32)
    o_ref[...] = acc.astype(o_ref.dtype)

    @pl.when(pl.program_id(1) == u_tile)
    def _():
        u_ref[...] = acc


def input_projection(x, gain, mod4, layer, w, *, tiles_per_seq, n_batch, tn=512, tm=TM):
    m, d = x.shape
    n = w.shape[2]
    assert tn == D_SSM and COL_U % tn == 0

    def mod_spec(col):
        return pl.BlockSpec((None, None, 1, d),
                            lambda i, j: (layer, jnp.minimum(i // tiles_per_seq, n_batch), 0, col))

    return pl.pallas_call(
        functools.partial(_in_proj_kernel, row_chunk=min(256, tm), u_tile=COL_U // tn),
        grid=(m // tm, n // tn),
        in_specs=[
            pl.BlockSpec((tm, d), lambda i, j: (i, 0)),
            pl.BlockSpec((1, d), lambda i, j: (0, 0)),
            mod_spec(0),
            mod_spec(1),
            pl.BlockSpec((None, d, tn), lambda i, j: (layer, 0, j)),
        ],
        out_specs=[pl.BlockSpec((tm, tn), lambda i, j: (i, j)),
                   pl.BlockSpec((tm, tn), lambda i, j: (i, 0))],
        out_shape=[jax.ShapeDtypeStruct((m, n), BF16), jax.ShapeDtypeStruct((m, tn), F32)],
        scratch_shapes=[pltpu.VMEM((tm, d), BF16)],
        compiler_params=_cparams(("arbitrary", "arbitrary"), 48),
    )(x, gain.reshape(1, d), mod4, mod4, w)


def _merge_kernel(yc_ref, ya_ref, ys_ref, gna_ref, w_ref, x_ref, gate_ref, g_ref, sh_ref, sc_ref,
                  xo_ref, ho_ref, *, row_chunk):
    tm = x_ref.shape[0]
    kc, ka = yc_ref.shape[1], ya_ref.shape[1]
    gna = gna_ref[...]
    gate = gate_ref[...]
    g = g_ref[...]
    sh = sh_ref[...]
    sc1 = 1.0 + sc_ref[...]
    for r in range(0, tm, row_chunk):
        rows = slice(r, r + row_chunk)
        acc = jnp.dot(yc_ref[rows, :], w_ref[0:kc, :], preferred_element_type=F32)
        ya = _rms(ya_ref[rows, :].astype(F32), gna).astype(BF16)
        acc = acc + jnp.dot(ya, w_ref[kc:kc + ka, :], preferred_element_type=F32)
        acc = acc + jnp.dot(ys_ref[rows, :], w_ref[kc + ka:, :], preferred_element_type=F32)
        x2 = x_ref[rows, :] + gate * acc
        xo_ref[rows, :] = x2
        ho_ref[rows, :] = (_rms(x2, g) * sc1 + sh).astype(ho_ref.dtype)


def merge_projection(y_conv, y_att, y_ssm, gn_att, w_o, x, gain2, mod4, layer, *, m_tiles, tiles_per_seq,
                     n_batch, tm):
    d = x.shape[1]
    rows = m_tiles * tm

    def mod_spec(col):
        return pl.BlockSpec((None, None, 1, d),
                            lambda i: (layer, jnp.minimum(i // tiles_per_seq, n_batch), 0, col))

    a_spec = lambda a: pl.BlockSpec((tm, a.shape[1]), lambda i: (i, 0))
    return pl.pallas_call(
        functools.partial(_merge_kernel, row_chunk=min(256, tm)),
        grid=(m_tiles,),
        in_specs=[
            a_spec(y_conv), a_spec(y_att), a_spec(y_ssm),
            pl.BlockSpec((1, y_att.shape[1]), lambda i: (0, 0)),
            pl.BlockSpec((None, d, d), lambda i: (layer, 0, 0)),
            pl.BlockSpec((tm, d), lambda i: (i, 0)),
            mod_spec(2),
            pl.BlockSpec((1, d), lambda i: (0, 0)),
            mod_spec(3),
            mod_spec(4),
        ],
        out_specs=[pl.BlockSpec((tm, d), lambda i: (i, 0)), pl.BlockSpec((tm, d), lambda i: (i, 0))],
        out_shape=[jax.ShapeDtypeStruct((rows, d), F32), jax.ShapeDtypeStruct((rows, d), BF16)],
        compiler_params=_cparams(("arbitrary",), 52),
    )(y_conv, y_att, y_ssm, gn_att.reshape(1, -1), w_o, x, mod4, gain2.reshape(1, d), mod4, mod4)


def _cast_rows(src_ref, dst_ref, row_chunk):
    for r in range(0, src_ref.shape[0], row_chunk):
        dst_ref[r:r + row_chunk, :] = src_ref[r:r + row_chunk, :].astype(dst_ref.dtype)


def _ffn_up_kernel(h_ref, wg_ref, wu_ref, o_ref, wg_scr, wu_scr):
    @pl.when(pl.program_id(1) == 0)
    def _():
        _cast_rows(wg_ref, wg_scr, 512)
        _cast_rows(wu_ref, wu_scr, 512)

    h = h_ref[...]
    gt = jnp.dot(h, wg_scr[...], preferred_element_type=F32)
    up = jnp.dot(h, wu_scr[...], preferred_element_type=F32)
    o_ref[...] = (gt * jax.nn.sigmoid(gt) * up).astype(o_ref.dtype)


def ffn_up(h, w_gate, w_up, layer, *, m_tiles, tm, tn):
    d = h.shape[1]
    f = w_gate.shape[2]
    w_spec = pl.BlockSpec((None, d, tn), lambda j, i: (layer, 0, j))
    return pl.pallas_call(
        _ffn_up_kernel,
        grid=(f // tn, m_tiles),
        in_specs=[pl.BlockSpec((tm, d), lambda j, i: (i, 0)), w_spec, w_spec],
        out_specs=pl.BlockSpec((tm, tn), lambda j, i: (i, j)),
        out_shape=jax.ShapeDtypeStruct((m_tiles * tm, f), BF16),
        scratch_shapes=[pltpu.VMEM((d, tn), BF16)] * 2,
        compiler_params=_cparams(("arbitrary", "arbitrary"), 56),
    )(h, w_gate, w_up)


def _ffn_down_kernel(a_ref, w_ref, x_ref, gate_ref, o_ref, w_scr):
    @pl.when(pl.program_id(1) == 0)
    def _():
        _cast_rows(w_ref, w_scr, 512)

    acc = jnp.dot(a_ref[...], w_scr[...], preferred_element_type=F32)
    o_ref[...] = x_ref[...] + gate_ref[...] * acc


def ffn_down(a, w_down, x, mod4, layer, *, m_tiles, tiles_per_seq, n_batch, tm, tn):
    f = a.shape[1]
    d = x.shape[1]
    gate_blocks = d // tn
    return pl.pallas_call(
        _ffn_down_kernel,
        grid=(d // tn, m_tiles),
        in_specs=[
            pl.BlockSpec((tm, f), lambda j, i: (i, 0)),
            pl.BlockSpec((None, f, tn), lambda j, i: (layer, 0, j)),
            pl.BlockSpec((tm, tn), lambda j, i: (i, j)),
            pl.BlockSpec((None, None, 1, tn),
                         lambda j, i: (layer, jnp.minimum(i // tiles_per_seq, n_batch), 0,
                                       5 * gate_blocks + j)),
        ],
        out_specs=pl.BlockSpec((tm, tn), lambda j, i: (i, j)),
        out_shape=jax.ShapeDtypeStruct((m_tiles * tm, d), F32),
        scratch_shapes=[pltpu.VMEM((f, tn), BF16)],
        compiler_params=_cparams(("arbitrary", "arbitrary"), 56),
    )(a, w_down, x, mod4)


HALO = 16


def _conv_kernel(h_ref, bg_ref, cg_ref, hp_ref, cp_ref, hn_ref, cn_ref, w_ref, gn_ref, o_ref, z_scr, *,
                 tiles_per_seq, lat_tiles):
    i = pl.program_id(0)
    t = h_ref.shape[0]
    whole_seq = i >= lat_tiles
    first = jnp.logical_or(i % tiles_per_seq == 0, whole_seq)
    last = jnp.logical_or(i % tiles_per_seq == tiles_per_seq - 1, whole_seq)
    z_prev = cp_ref[HALO - 1:HALO, :].astype(F32) * hp_ref[HALO - 1:HALO, :].astype(F32)
    z_next = cn_ref[0:1, :].astype(F32) * hn_ref[0:1, :].astype(F32)
    z_scr[7:8, :] = jnp.where(first, 0.0, z_prev)
    z_scr[8:8 + t, :] = cg_ref[...].astype(F32) * h_ref[...].astype(F32)
    z_scr[8 + t:9 + t, :] = jnp.where(last, 0.0, z_next)
    y = bg_ref[...].astype(F32) * (w_ref[0:1, :] * z_scr[7:7 + t, :] + w_ref[1:2, :] * z_scr[8:8 + t, :]
                                   + w_ref[2:3, :] * z_scr[9:9 + t, :])
    o_ref[...] = _rms(y, gn_ref[...]).astype(o_ref.dtype)


def conv_mixer(proj, conv_w, gn, *, rows, tile, tiles_per_seq, lat_tiles):
    ch = D_CONV
    n_tiles = rows // tile
    hb = tile // HALO
    last_blk = proj.shape[0] // HALO - 1
    main = lambda c: pl.BlockSpec((tile, ch), lambda i: (i, c))
    prev = lambda c: pl.BlockSpec((HALO, ch), lambda i: (jnp.maximum(i * hb - 1, 0), c))
    nxt = lambda c: pl.BlockSpec((HALO, ch), lambda i: (jnp.minimum((i + 1) * hb, last_blk), c))
    return pl.pallas_call(
        functools.partial(_conv_kernel, tiles_per_seq=tiles_per_seq, lat_tiles=lat_tiles),
        grid=(n_tiles,),
        in_specs=[main(0), main(1), main(2), prev(0), prev(2), nxt(0), nxt(2),
                  pl.BlockSpec((CONV_W, ch), lambda i: (0, 0)), pl.BlockSpec((1, ch), lambda i: (0, 0))],
        out_specs=pl.BlockSpec((tile, ch), lambda i: (i, 0)),
        out_shape=jax.ShapeDtypeStruct((rows, ch), BF16),
        scratch_shapes=[pltpu.VMEM((tile + 16, ch), F32)],
        compiler_params=_cparams(("arbitrary",), 32),
    )(proj, proj, proj, proj, proj, proj, proj, conv_w, gn.reshape(1, ch))


def _q_proj_kernel(cq_ref, g_ref, w_ref, cos_ref, sin_ref, o_ref):
    a = _rms(cq_ref[...].astype(F32), g_ref[...]).astype(BF16)
    q = jnp.dot(a, w_ref[...], preferred_element_type=F32)
    cos = cos_ref[...]
    sin = sin_ref[...]
    scale = MLA_SCALE * LOG2E
    for h in range(MLA_HEADS):
        c0 = h * QK_PAD
        o_ref[:, c0:c0 + QK_NOPE] = (q[:, c0:c0 + QK_NOPE] * scale).astype(o_ref.dtype)
        blk = q[:, c0 + QK_NOPE:c0 + QK_PAD]
        rot = blk * cos + pltpu.roll(blk, QK_ROPE, axis=1) * sin
        o_ref[:, c0 + QK_NOPE:c0 + QK_PAD] = (rot * scale).astype(o_ref.dtype)


def q_projection(proj, q_norm, w_q, cos_t, sin_t, *, rows, tm, lat_tiles, tiles_per_seq):
    tab_map = lambda i: (jnp.where(i < lat_tiles, i % tiles_per_seq, tiles_per_seq), 0)
    n = MLA_HEADS * QK_PAD
    return pl.pallas_call(
        _q_proj_kernel,
        grid=(rows // tm,),
        in_specs=[
            pl.BlockSpec((tm, Q_RANK), lambda i: (i, COL_CQ // Q_RANK)),
            pl.BlockSpec((1, Q_RANK), lambda i: (0, 0)),
            pl.BlockSpec((Q_RANK, n), lambda i: (0, 0)),
            pl.BlockSpec((tm, LANE), tab_map),
            pl.BlockSpec((tm, LANE), tab_map),
        ],
        out_specs=pl.BlockSpec((tm, n), lambda i: (i, 0)),
        out_shape=jax.ShapeDtypeStruct((rows, n), BF16),
        compiler_params=_cparams(("arbitrary",), 40),
    )(proj, q_norm.reshape(1, Q_RANK), w_q, cos_t, sin_t)


def _kv_proj_kernel(ckv_ref, kr_ref, g_ref, w_ref, cos_ref, sin_ref, k_ref, v_ref):
    a = _rms(ckv_ref[...].astype(F32), g_ref[...]).astype(BF16)
    kv = jnp.dot(a, w_ref[...], preferred_element_type=F32)
    blk = kr_ref[...].astype(F32)
    rot = (blk * cos_ref[...] + pltpu.roll(blk, QK_ROPE, axis=1) * sin_ref[...]).astype(k_ref.dtype)
    nk = MLA_HEADS * QK_NOPE
    for h in range(MLA_HEADS):
        c0 = h * QK_PAD
        k_ref[:, c0:c0 + QK_NOPE] = kv[:, h * QK_NOPE:(h + 1) * QK_NOPE].astype(k_ref.dtype)
        k_ref[:, c0 + QK_NOPE:c0 + QK_PAD] = rot
    ones = jnp.ones((v_ref.shape[0], V_PAD - V_HEAD), v_ref.dtype)
    for h in range(MLA_HEADS):
        c0 = h * V_PAD
        v_ref[:, c0:c0 + V_HEAD] = kv[:, nk + h * V_HEAD:nk + (h + 1) * V_HEAD].astype(v_ref.dtype)
        v_ref[:, c0 + V_HEAD:c0 + V_PAD] = ones


def kv_projection(proj, kv_norm, w_kv, cos_t, sin_t, *, rows, tm, lat_tiles, tiles_per_seq):
    tab_map = lambda i: (jnp.where(i < lat_tiles, i % tiles_per_seq, tiles_per_seq), 0)
    nk = MLA_HEADS * QK_PAD
    nv = MLA_HEADS * V_PAD
    return pl.pallas_call(
        _kv_proj_kernel,
        grid=(rows // tm,),
        in_specs=[
            pl.BlockSpec((tm, KV_RANK), lambda i: (i, COL_CKV // KV_RANK)),
            pl.BlockSpec((tm, LANE), lambda i: (i, COL_KR // LANE)),
            pl.BlockSpec((1, KV_RANK), lambda i: (0, 0)),
            pl.BlockSpec((KV_RANK, MLA_HEADS * (QK_NOPE + V_HEAD)), lambda i: (0, 0)),
            pl.BlockSpec((tm, LANE), tab_map),
            pl.BlockSpec((tm, LANE), tab_map),
        ],
        out_specs=[pl.BlockSpec((tm, nk), lambda i: (i, 0)), pl.BlockSpec((tm, nv), lambda i: (i, 0))],
        out_shape=[jax.ShapeDtypeStruct((rows, nk), BF16), jax.ShapeDtypeStruct((rows, nv), BF16)],
        compiler_params=_cparams(("arbitrary",), 40),
    )(proj, proj, kv_norm.reshape(1, KV_RANK), w_kv, cos_t, sin_t)


_NT = (((1,), (1,)), ((), ()))


HEAD_SPLIT = 2


def _attn_kernel(q_ref, kl_ref, kc_ref, vl_ref, vc_ref, o_ref, *, lat_steps, steps_per_batch, ctx_len):
    s = pl.program_id(0)
    hpg = MLA_HEADS // HEAD_SPLIT

    def heads(q_rows, c_rows, with_latent):
        for h in range(hpg):
            q = q_ref[q_rows, h * QK_PAD:(h + 1) * QK_PAD]
            sc = lax.dot_general(q, kc_ref[c_rows, h * QK_PAD:(h + 1) * QK_PAD], _NT,
                                 preferred_element_type=F32)
            m = jnp.max(sc, axis=-1, keepdims=True)
            if with_latent:
                sl = lax.dot_general(q, kl_ref[:, h * QK_PAD:(h + 1) * QK_PAD], _NT,
                                     preferred_element_type=F32)
                m = jnp.maximum(m, jnp.max(sl, axis=-1, keepdims=True))
            o = jnp.dot(jnp.exp2(sc - m).astype(BF16), vc_ref[c_rows, h * V_PAD:(h + 1) * V_PAD],
                        preferred_element_type=F32)
            if with_latent:
                o = o + jnp.dot(jnp.exp2(sl - m).astype(BF16), vl_ref[:, h * V_PAD:(h + 1) * V_PAD],
                                preferred_element_type=F32)
            o_ref[q_rows, h * V_HEAD:(h + 1) * V_HEAD] = (o[:, :V_HEAD] / o[:, V_HEAD:]).astype(o_ref.dtype)

    @pl.when(s < lat_steps)
    def _():
        b = s // steps_per_batch
        c0 = pl.multiple_of((b % 2) * ctx_len, ctx_len)
        heads(slice(None), pl.ds(c0, ctx_len), True)

    @pl.when(s >= lat_steps)
    def _():
        for hb in range(2):
            rows = slice(hb * ctx_len, (hb + 1) * ctx_len)
            heads(rows, rows, False)


def latent_attention(q, k, v, *, n_batch, seq, ctx_len, tq, ctx_queries, rows_out):
    assert tq == 2 * ctx_len and n_batch % 2 == 0 and seq % tq == 0
    lat_tiles = seq // tq
    spb = HEAD_SPLIT * lat_tiles
    lat_steps = n_batch * spb
    ctx_steps = (n_batch // 2) * HEAD_SPLIT if ctx_queries else 0
    ctx_blk0 = n_batch * seq // tq
    nq = MLA_HEADS * QK_PAD // HEAD_SPLIT
    nv = MLA_HEADS * V_HEAD // HEAD_SPLIT
    nvp = MLA_HEADS * V_PAD // HEAD_SPLIT

    def split(s):
        lat = s < lat_steps
        cs = s - lat_steps
        b = jnp.where(lat, s // spb, n_batch - 1)
        hh = jnp.where(lat, (s // lat_tiles) % HEAD_SPLIT, cs % HEAD_SPLIT)
        q_blk = jnp.where(lat, b * lat_tiles + s % lat_tiles, ctx_blk0 + cs // HEAD_SPLIT)
        c_blk = jnp.where(lat, ctx_blk0 + b // 2, ctx_blk0 + cs // HEAD_SPLIT)
        return b, hh, q_blk, c_blk

    def lat_map(s):
        b, hh, _, _ = split(s)
        return (b, jnp.where(s < lat_steps, hh, HEAD_SPLIT - 1))

    q_map = lambda s: (split(s)[2], split(s)[1])
    c_map = lambda s: (split(s)[3], split(s)[1])
    return pl.pallas_call(
        functools.partial(_attn_kernel, lat_steps=lat_steps, steps_per_batch=spb, ctx_len=ctx_len),
        grid=(lat_steps + ctx_steps,),
        in_specs=[
            pl.BlockSpec((tq, nq), q_map),
            pl.BlockSpec((seq, nq), lat_map),
            pl.BlockSpec((tq, nq), c_map),
            pl.BlockSpec((seq, nvp), lat_map),
            pl.BlockSpec((tq, nvp), c_map),
        ],
        out_specs=pl.BlockSpec((tq, nv), q_map),
        out_shape=jax.ShapeDtypeStruct((rows_out, MLA_HEADS * V_HEAD), BF16),
        compiler_params=_cparams(("arbitrary",), 48),
    )(q, k, k, v, v)


def _s5_disc_kernel(are_ref, aim_ref, ldt_ref, abr_ref, abi_ref, cfr_ref, cfi_ref):
    ar = are_ref[...]
    ai = aim_ref[...]
    dt = jnp.exp(ldt_ref[...])
    mag = jnp.exp(ar * dt)
    th = ai * dt
    br = mag * jnp.cos(th)
    bi = mag * jnp.sin(th)
    nr = br - 1.0
    den = ar * ar + ai * ai
    abr_ref[...] = br
    abi_ref[...] = bi
    cfr_ref[...] = (nr * ar + bi * ai) / den
    cfi_ref[...] = (bi * ar - nr * ai) / den


def s5_discretise(a_re, a_im, log_dt):
    shp = a_re.shape
    rows = shp[0] * shp[1] * shp[2]
    flat = lambda t: t.reshape(rows, shp[3])
    ldt = jnp.broadcast_to(log_dt[..., None], shp)
    outs = pl.pallas_call(
        _s5_disc_kernel,
        out_shape=[jax.ShapeDtypeStruct((rows, shp[3]), F32)] * 4,
    )(flat(a_re), flat(a_im), flat(ldt))
    return [o.reshape(shp) for o in outs]


def _cmul(xr, xi, yr, yi):
    return xr * yr - xi * yi, xr * yi + xi * yr


def _s5_mats_kernel(abc_re, abc_im, cex_re, cex_im, afb_re, afb_im, cfb_re, cfb_im, bfb_re, bfb_im,
                    cpp_re, cpp_im, bx_ref, by_ref, ma_ref, mi_ref, mo_ref, a16_ref):
    tw = CHUNK * SSM_GROUP
    p = SSM_STATE
    lane = lax.broadcasted_iota(jnp.int32, (p, tw), 1)
    tblk = lane // SSM_GROUP
    lane16 = lax.broadcasted_iota(jnp.int32, (SSM_GROUP, tw), 1)
    lane128 = lax.broadcasted_iota(jnp.int32, (1, 2 * p), 1)
    fwd_lanes = lane128 < p
    sgn = jnp.where(lax.broadcasted_iota(jnp.int32, (SSM_GROUP, 2 * p), 1) < p, 1.0, -1.0)
    zeros = jnp.zeros((p, tw), F32)
    mi_blocks = []
    for d in range(2):
        ar = jnp.broadcast_to(abc_re[d], (p, tw))
        ai = jnp.broadcast_to(abc_im[d], (p, tw))
        sel_r = tblk if d == 0 else (CHUNK - 1) - tblk
        sq_re, sq_im = ar, ai
        r_re, r_im = jnp.ones((p, tw), F32), zeros
        for b in range((CHUNK - 1).bit_length()):
            bit = ((sel_r >> b) & 1) == 1
            n_re, n_im = _cmul(r_re, r_im, sq_re, sq_im)
            r_re = jnp.where(bit, n_re, r_re)
            r_im = jnp.where(bit, n_im, r_im)
            sq_re, sq_im = _cmul(sq_re, sq_im, sq_re, sq_im)
        e_re, e_im = _cmul(r_re, r_im, ar, ai)
        cr = cex_re[d]
        ci = cex_im[d]
        w_re, w_im = _cmul(cr, ci, e_re, e_im)
        base = d * 4 * p
        mdt = mo_ref.dtype
        zeros_m = zeros.astype(mdt)
        if d == 0:
            mo_ref[base:base + p, :] = w_re.astype(mdt)
            mo_ref[base + p:base + 2 * p, :] = zeros_m
            mo_ref[base + 2 * p:base + 3 * p, :] = (-w_im).astype(mdt)
            mo_ref[base + 3 * p:base + 4 * p, :] = zeros_m
        else:
            mo_ref[base:base + p, :] = zeros_m
            mo_ref[base + p:base + 2 * p, :] = w_re.astype(mdt)
            mo_ref[base + 2 * p:base + 3 * p, :] = zeros_m
            mo_ref[base + 3 * p:base + 4 * p, :] = (-w_im).astype(mdt)
        rr_re, rr_im = _cmul(cr, ci, r_re, r_im)
        stacked = jnp.concatenate([rr_re, rr_im], axis=0)
        lm = sgn * cpp_re[d] * bx_ref[d] - cpp_im[d] * by_ref[d]
        kall = jnp.dot(lm, stacked, preferred_element_type=F32, precision=lax.Precision.HIGHEST)
        for s in range(CHUNK):
            if d == 0:
                shift = SSM_GROUP * s
                keep = lane16 >= SSM_GROUP * s
            else:
                shift = (SSM_GROUP * (s + 1)) % tw
                keep = lane16 < SSM_GROUP * (s + 1)
            rolled = pltpu.roll(kall, shift, axis=1) if shift else kall
            blk = jnp.where(keep, rolled, 0.0)
            if d == 0:
                mi_blocks.append(blk)
            else:
                mi_ref[SSM_GROUP * s:SSM_GROUP * (s + 1), :] = (mi_blocks[s] + blk).astype(mi_ref.dtype)

    a_re = afb_re[...]
    a_im = afb_im[...]
    cf_re = cfb_re[...]
    cf_im = cfb_im[...]
    b_re = bfb_re[...]
    b_im = bfb_im[...]
    powers = []
    qr, qi = jnp.ones((1, 2 * p), F32), jnp.zeros((1, 2 * p), F32)
    for k in range(CHUNK + 1):
        powers.append((qr, qi))
        if k < CHUNK:
            qr, qi = _cmul(qr, qi, a_re, a_im)
    for s in range(CHUNK):
        g_re = jnp.where(fwd_lanes, powers[CHUNK - 1 - s][0], powers[s][0])
        g_im = jnp.where(fwd_lanes, powers[CHUNK - 1 - s][1], powers[s][1])
        g_re, g_im = _cmul(g_re, g_im, cf_re, cf_im)
        rows = slice(SSM_GROUP * s, SSM_GROUP * (s + 1))
        ma_ref[rows, 0:2 * p] = (g_re * b_re - g_im * b_im).astype(ma_ref.dtype)
        ma_ref[rows, 2 * p:4 * p] = (g_re * b_im + g_im * b_re).astype(ma_ref.dtype)
    a16_ref[0:1, :] = powers[CHUNK][0]
    a16_ref[1:2, :] = powers[CHUNK][1]


def s5_matrices(abar_re, abar_im, coef_re, coef_im, b_re, b_im, c_re, c_im):
    depth, _, g, p = abar_re.shape
    hg = b_re.shape[-1]
    tw = CHUNK * hg
    t0213 = lambda t: t.transpose(0, 2, 1, 3)
    abc = [t0213(t)[..., None] for t in (abar_re, abar_im)]
    cex = [jnp.tile(t.transpose(0, 2, 1, 4, 3), (1, 1, 1, 1, CHUNK)) for t in (c_re, c_im)]
    fb = lambda t: t0213(t).reshape(depth, g, 1, 2 * p)
    afb = [fb(t) for t in (abar_re, abar_im)]
    cfb = [fb(t) for t in (coef_re, coef_im)]
    bfb = [t.transpose(0, 2, 4, 1, 3).reshape(depth, g, hg, 2 * p) for t in (b_re, b_im)]
    cpp = [t0213(jnp.concatenate([t, t], axis=-1))[:, :, :, None, :] for t in (coef_re, coef_im)]
    bt_re = b_re.transpose(0, 2, 1, 4, 3)
    bt_im = b_im.transpose(0, 2, 1, 4, 3)
    bx = jnp.concatenate([bt_re, bt_im], axis=-1)
    by = jnp.concatenate([bt_im, bt_re], axis=-1)
    ins = [*abc, *cex, *afb, *cfb, *bfb, *cpp, bx, by]

    def spec(t):
        blk = (None, None) + t.shape[2:]
        nz = len(t.shape) - 2
        return pl.BlockSpec(blk, lambda l, gi: (l, gi) + (0,) * nz)

    def ospec(r, c):
        return pl.BlockSpec((None, None, r, c), lambda l, gi: (l, gi, 0, 0))

    return pl.pallas_call(
        _s5_mats_kernel,
        grid=(depth, g),
        in_specs=[spec(t) for t in ins],
        out_specs=[ospec(tw, 4 * p), ospec(tw, tw), ospec(8 * p, tw), ospec(2, 2 * p)],
        out_shape=[
            jax.ShapeDtypeStruct((depth, g, tw, 4 * p), BF16),
            jax.ShapeDtypeStruct((depth, g, tw, tw), BF16),
            jax.ShapeDtypeStruct((depth, g, 8 * p, tw), BF16),
            jax.ShapeDtypeStruct((depth, g, 2, 2 * p), F32),
        ],
        compiler_params=_cparams(("arbitrary", "arbitrary"), 32),
    )(*ins)


GROUP_BLOCK = LANE // SSM_GROUP


def lane_swap_matrix():
    idx = jnp.arange(GROUP_BLOCK * LANE)
    a, b, c = idx // LANE, (idx // SSM_GROUP) % GROUP_BLOCK, idx % SSM_GROUP
    dst = b * LANE + a * SSM_GROUP + c
    return (dst[:, None] == idx[None, :]).astype(BF16)


def _s5_main_kernel(*refs, order_f, order_b, nb, lat_chunks, ctx_chunks):
    (u_ref, p_ref, ma_ref, mi_ref, mo_ref, a_ref, y_ref, ug_scr, yg_scr,
     xr_scr, xi_scr, fr_scr, fi_scr, br_scr, bi_scr) = refs
    p2 = 2 * SSM_STATE
    half = CHUNK // 2
    rows = u_ref.shape[0] // CHUNK
    n_sub = xr_scr.shape[0]

    for th in range(2):
        slab = jnp.concatenate(
            [u_ref[pl.ds(half * th + tl, rows, stride=CHUNK), :].astype(BF16) for tl in range(half)], axis=1)
        perm = jnp.dot(slab, p_ref[...], preferred_element_type=F32).astype(BF16)
        for g in range(GROUP_BLOCK):
            ug_scr[g, :, th * LANE:(th + 1) * LANE] = perm[:, g * LANE:(g + 1) * LANE]

    def rows_of(j):
        if j < lat_chunks:
            return pl.ds(j, nb, stride=lat_chunks)
        return pl.ds(nb * lat_chunks + (j - lat_chunks), nb, stride=ctx_chunks)

    fwd_lanes = lax.broadcasted_iota(jnp.int32, (nb, p2), 1) < SSM_STATE
    for g0 in range(0, GROUP_BLOCK, n_sub):
        for k in range(n_sub):
            x = jnp.dot(ug_scr[g0 + k], ma_ref[g0 + k], preferred_element_type=F32)
            xr_scr[k] = x[:, 0:p2]
            xi_scr[k] = x[:, p2:2 * p2]

        a_re = [a_ref[g0 + k, 0:1, :] for k in range(n_sub)]
        a_im = [a_ref[g0 + k, 1:2, :] for k in range(n_sub)]
        s_re = [jnp.zeros((nb, p2), F32) for _ in range(n_sub)]
        s_im = [jnp.zeros((nb, p2), F32) for _ in range(n_sub)]
        for jf, jb in zip(order_f, order_b):
            rf = rows_of(jf)
            rb = rows_of(jb)
            for k in range(n_sub):
                fr_scr[k, rf, :] = s_re[k]
                fi_scr[k, rf, :] = s_im[k]
                br_scr[k, rb, :] = s_re[k]
                bi_scr[k, rb, :] = s_im[k]
                xr = jnp.where(fwd_lanes, xr_scr[k, rf, :], xr_scr[k, rb, :])
                xi = jnp.where(fwd_lanes, xi_scr[k, rf, :], xi_scr[k, rb, :])
                s_re[k], s_im[k] = (a_re[k] * s_re[k] - a_im[k] * s_im[k] + xr,
                                    a_re[k] * s_im[k] + a_im[k] * s_re[k] + xi)

        for k in range(n_sub):
            g = g0 + k
            y = jnp.dot(ug_scr[g], mi_ref[g], preferred_element_type=F32)
            sp_f = jnp.concatenate([fr_scr[k], fi_scr[k]], axis=1).astype(BF16)
            sp_b = jnp.concatenate([br_scr[k], bi_scr[k]], axis=1).astype(BF16)
            y = y + jnp.dot(sp_f, mo_ref[g, 0:2 * p2, :], preferred_element_type=F32)
            y = y + jnp.dot(sp_b, mo_ref[g, 2 * p2:4 * p2, :], preferred_element_type=F32)
            yg_scr[g] = y.astype(BF16)

    for th in range(2):
        slab = jnp.concatenate([yg_scr[g, :, th * LANE:(th + 1) * LANE] for g in range(GROUP_BLOCK)], axis=1)
        perm = jnp.dot(slab, p_ref[...], preferred_element_type=F32)
        for tl in range(half):
            y_ref[pl.ds(half * th + tl, rows, stride=CHUNK), :] = perm[:, tl * LANE:(tl + 1) * LANE]


def s5_chunked(u, ma, mi, mo, a16, layer, *, lat_chunks, ctx_chunks, nb):
    r_all = u.shape[0]
    rows = r_all // CHUNK
    tw = CHUNK * SSM_GROUP
    j_tot = lat_chunks + ctx_chunks
    order_f = tuple(range(lat_chunks, j_tot)) + tuple(range(lat_chunks))
    order_b = tuple(range(j_tot - 1, lat_chunks - 1, -1)) + tuple(range(lat_chunks - 1, -1, -1))
    p4 = 4 * SSM_STATE
    n_gb = SSM_GROUPS // GROUP_BLOCK
    n_sub = GROUP_BLOCK // 2

    def wspec(r, c):
        return pl.BlockSpec((None, GROUP_BLOCK, r, c), lambda gb: (layer, gb, 0, 0))

    return pl.pallas_call(
        functools.partial(_s5_main_kernel, order_f=order_f, order_b=order_b, nb=nb,
                          lat_chunks=lat_chunks, ctx_chunks=ctx_chunks),
        grid=(n_gb,),
        in_specs=[
            pl.BlockSpec((r_all, LANE), lambda gb: (0, gb)),
            pl.BlockSpec((GROUP_BLOCK * LANE, GROUP_BLOCK * LANE), lambda gb: (0, 0)),
            wspec(tw, p4), wspec(tw, tw), wspec(2 * p4, tw), wspec(2, 2 * SSM_STATE),
        ],
        out_specs=pl.BlockSpec((r_all, LANE), lambda gb: (0, gb)),
        out_shape=jax.ShapeDtypeStruct((r_all, D_SSM), F32),
        scratch_shapes=[
            pltpu.VMEM((GROUP_BLOCK, rows, tw), BF16),
            pltpu.VMEM((GROUP_BLOCK, rows, tw), BF16),
        ] + [pltpu.VMEM((n_sub, rows, 2 * SSM_STATE), F32)] * 6,
        compiler_params=_cparams(("arbitrary",), 56),
    )(u, lane_swap_matrix(), ma, mi, mo, a16)


def _s5_out_kernel(y_ref, u_ref, d_ref, w_ref, b_ref, gn_ref, o_ref):
    y = y_ref[...].astype(F32) + d_ref[...] * u_ref[...].astype(F32)
    g = jax.nn.gelu(y)
    z = jnp.dot(g.astype(BF16), w_ref[...], preferred_element_type=F32) + b_ref[...]
    o_ref[...] = _rms(g * jax.nn.sigmoid(z), gn_ref[...]).astype(o_ref.dtype)


def s5_output(y, u, ssm_d, w_glu, b_glu, gn, *, rows, tm):
    ch = D_SSM
    vec = lambda: pl.BlockSpec((1, ch), lambda i: (0, 0))
    return pl.pallas_call(
        _s5_out_kernel,
        grid=(rows // tm,),
        in_specs=[
            pl.BlockSpec((tm, ch), lambda i: (i, 0)),
            pl.BlockSpec((tm, ch), lambda i: (i, 0)),
            vec(),
            pl.BlockSpec((ch, ch), lambda i: (0, 0)),
            vec(), vec(),
        ],
        out_specs=pl.BlockSpec((tm, ch), lambda i: (i, 0)),
        out_shape=jax.ShapeDtypeStruct((rows, ch), BF16),
        compiler_params=_cparams(("arbitrary",), 32),
    )(y, u, ssm_d.reshape(1, ch), w_glu, b_glu.reshape(1, ch), gn.reshape(1, ch))


def _final_norm_kernel(x_ref, g_ref, o_ref):
    o_ref[...] = _rms(x_ref[...], g_ref[...])


def final_rms_norm(x, g, *, rows, tm):
    d = x.shape[1]
    return pl.pallas_call(
        _final_norm_kernel,
        grid=(rows // tm,),
        in_specs=[pl.BlockSpec((tm, d), lambda i: (i, 0)), pl.BlockSpec((1, d), lambda i: (0, 0))],
        out_specs=pl.BlockSpec((tm, d), lambda i: (i, 0)),
        out_shape=jax.ShapeDtypeStruct((rows, d), F32),
        compiler_params=_cparams(("arbitrary",), 32),
    )(x, g.reshape(1, d))


def _rope_partner_perm():
    idx = []
    for i in range(QK_ROPE):
        idx.append(i + 16 if (i % 32) < 16 else i - 16)
    return jnp.asarray(idx, jnp.int32)


def rope_tables(seq, tab_tile):
    rows = seq // GRID_W
    row = jnp.repeat(jnp.arange(rows, dtype=F32), GRID_W)
    col = jnp.tile(jnp.arange(GRID_W, dtype=F32), rows)
    n_freq = QK_ROPE // 4
    inv = ROPE_BASE ** (-jnp.arange(n_freq, dtype=F32) / n_freq)
    ar = row[:, None] * inv
    ac = col[:, None] * inv
    cos = jnp.concatenate([jnp.cos(ar), jnp.cos(ar), jnp.cos(ac), jnp.cos(ac)], axis=1)
    sin = jnp.concatenate([-jnp.sin(ar), jnp.sin(ar), -jnp.sin(ac), jnp.sin(ac)], axis=1)
    cos = jnp.concatenate([cos, jnp.ones((tab_tile, QK_ROPE), F32)], axis=0)
    sin = jnp.concatenate([sin, jnp.zeros((tab_tile, QK_ROPE), F32)], axis=0)
    pad = jnp.zeros((seq + tab_tile, LANE - QK_ROPE), F32)
    return jnp.concatenate([cos, pad], axis=1), jnp.concatenate([sin, pad], axis=1)


def prep_w_in(w_in):
    s = [0, 512, 1024, 1536, 2048, 2304, 2368, 2880]
    hconv, bg, cg, cq, ckv, kr, u = [w_in[..., s[i]:s[i + 1]] for i in range(7)]
    krp = kr[..., _rope_partner_perm()]
    pad = jnp.zeros(w_in.shape[:-1] + (N_PROJ - (COL_KR + LANE),), w_in.dtype)
    return jnp.concatenate([hconv, bg, cg, cq, u, ckv, kr, krp, pad], axis=-1).astype(BF16)


def prep_w_uq(w_uq):
    r = w_uq.shape[0]
    w = w_uq.reshape(r, MLA_HEADS, QK_NOPE + QK_ROPE)
    rope = w[:, :, QK_NOPE:]
    w = jnp.concatenate([w, rope[:, :, _rope_partner_perm()]], axis=-1)
    return w.reshape(r, MLA_HEADS * QK_PAD).astype(BF16)


def prep_w_ukv(w_ukv):
    r = w_ukv.shape[0]
    w = w_ukv.reshape(r, MLA_HEADS, QK_NOPE + V_HEAD)
    kn = w[:, :, :QK_NOPE].reshape(r, MLA_HEADS * QK_NOPE)
    v = w[:, :, QK_NOPE:].reshape(r, MLA_HEADS * V_HEAD)
    return jnp.concatenate([kn, v], axis=1).astype(BF16)


def kernel(x, c, ctx, c_ctx, w_ada, b_ada, norm1_g, norm2_g, w_in, conv_w, mla_q_norm, w_uq, mla_kv_norm, w_ukv, ssm_a_re, ssm_a_im, ssm_log_dt, ssm_b_re, ssm_b_im, ssm_c_re, ssm_c_im, ssm_d, w_glu, b_glu, mix_norm, w_o, w_gate, w_up, w_down, final_norm):
    nb, seq, d = x.shape
    ctx_len = ctx.shape[1]
    depth = w_ada.shape[0]
    r_lat = nb * seq
    r_ctx = nb * ctx_len
    r_all = r_lat + r_ctx
    assert seq % TM == 0 and r_ctx == TM and seq % CHUNK == 0 and ctx_len % CHUNK == 0
    tiles_per_seq = seq // TM
    lat_tiles_m = r_lat // TM
    d_ff = w_gate.shape[2]

    xs = jnp.concatenate([x.reshape(r_lat, d), ctx.reshape(r_ctx, d)], axis=0)

    cvec = jnp.concatenate([c, c_ctx[None, :], jnp.zeros((8 - nb - 1, d), F32)], axis=0)
    mod = ada_modulation(cvec, w_ada, b_ada)
    mod4 = mod.reshape(depth, 8, 1, 6 * d)

    tp = 512
    cos_t, sin_t = rope_tables(seq, tp)
    abar_re, abar_im, coef_re, coef_im = s5_discretise(ssm_a_re, ssm_a_im, ssm_log_dt)
    ma, mi, mo, a16 = s5_matrices(abar_re, abar_im, coef_re, coef_im,
                                  ssm_b_re, ssm_b_im, ssm_c_re, ssm_c_im)
    w_in_b = prep_w_in(w_in)
    w_o_b = w_o.astype(BF16)
    lat_chunks = seq // CHUNK
    ctx_chunks = ctx_len // CHUNK
    tw = CHUNK * SSM_GROUP

    for i in range(depth):
        ctx_out = i < depth - 1
        m_tiles = lat_tiles_m + (1 if ctx_out else 0)
        rows_out = r_all if ctx_out else r_lat
        tf = 512

        proj, u32 = input_projection(xs, norm1_g[i], mod4, i, w_in_b,
                                     tiles_per_seq=tiles_per_seq, n_batch=nb)

        y_conv = conv_mixer(proj, conv_w[i], mix_norm[i, :D_CONV], rows=rows_out, tile=ctx_len,
                            tiles_per_seq=seq // ctx_len, lat_tiles=r_lat // ctx_len)

        q_rows = rows_out
        q = q_projection(proj, mla_q_norm[i], prep_w_uq(w_uq[i]), cos_t, sin_t, rows=q_rows, tm=tp,
                         lat_tiles=r_lat // tp, tiles_per_seq=seq // tp)
        k, v = kv_projection(proj, mla_kv_norm[i], prep_w_ukv(w_ukv[i]), cos_t, sin_t, rows=r_all,
                             tm=tp, lat_tiles=r_lat // tp, tiles_per_seq=seq // tp)
        y_att = latent_attention(q, k, v, n_batch=nb, seq=seq, ctx_len=ctx_len, tq=2 * ctx_len,
                                 ctx_queries=ctx_out, rows_out=rows_out)

        y_tok = s5_chunked(u32, ma, mi, mo, a16, i, lat_chunks=lat_chunks, ctx_chunks=ctx_chunks, nb=nb)
        y_ssm = s5_output(y_tok, u32, ssm_d[i], w_glu[i].astype(BF16), b_glu[i],
                          mix_norm[i, D_CONV + D_ATTN:], rows=rows_out, tm=tp)

        x2, h2 = merge_projection(y_conv, y_att, y_ssm, mix_norm[i, D_CONV:D_CONV + D_ATTN], w_o_b, xs,
                                  norm2_g[i], mod4, i,
                                  m_tiles=rows_out // tf, tiles_per_seq=seq // tf, n_batch=nb, tm=tf)
        tu = next(t for t in (2048, 1536, 1024) if rows_out % t == 0)
        hidden = ffn_up(h2, w_gate, w_up, i, m_tiles=rows_out // tu, tm=tu, tn=512)
        xs = ffn_down(hidden, w_down, x2, mod4, i, m_tiles=rows_out // tf, tiles_per_seq=seq // tf,
                      n_batch=nb, tm=tf, tn=512)

    out = final_rms_norm(xs, final_norm, rows=r_lat, tm=tp)
    return out.reshape(nb, seq, d)
```

```python
import functools
import math

import jax
import jax.numpy as jnp
from jax import lax
from jax.experimental import pallas as pl
from jax.experimental.pallas import tpu as pltpu

F32 = jnp.float32
BF16 = jnp.bfloat16

EPS = 1e-6
GRID_W = 64
CONV_W = 3
D_CONV = 512
D_SSM = 512
D_ATTN = 1024
MLA_HEADS = 8
QK_NOPE = 128
QK_ROPE = 64
V_HEAD = 128
Q_RANK = 512
KV_RANK = 256
ROPE_BASE = 10000.0
MLA_SCALE = (QK_NOPE + QK_ROPE) ** -0.5
SSM_GROUP = 16
SSM_GROUPS = 32
SSM_STATE = 64
CHUNK = 16

QK_PAD = 256
V_PAD = 256
LOG2E = math.log2(math.e)
LANE = 128
TM = 1024
MIB = 1024 * 1024

COL_CONV = 0
COL_CQ = 3 * D_CONV
COL_U = COL_CQ + Q_RANK
COL_CKV = COL_U + D_SSM
COL_KR = COL_CKV + KV_RANK
N_PROJ = 3072


def _cparams(sem, vmem_mib):
    return pltpu.CompilerParams(dimension_semantics=sem, vmem_limit_bytes=vmem_mib * MIB)


def _rms(x, g):
    return x * lax.rsqrt(jnp.mean(x * x, axis=-1, keepdims=True) + EPS) * g


def _ada_kernel(c_ref, w_ref, b_ref, o_ref):
    cv = c_ref[...]
    s = (cv * jax.nn.sigmoid(cv)).astype(BF16)
    o_ref[...] = jnp.dot(s, w_ref[...].astype(BF16), preferred_element_type=F32) + b_ref[...]


def ada_modulation(cvec, w_ada, b_ada, tn=1024):
    depth, d, n = w_ada.shape
    rows = cvec.shape[0]
    return pl.pallas_call(
        _ada_kernel,
        grid=(depth, n // tn),
        in_specs=[
            pl.BlockSpec((rows, d), lambda l, j: (0, 0)),
            pl.BlockSpec((None, d, tn), lambda l, j: (l, 0, j)),
            pl.BlockSpec((None, 1, tn), lambda l, j: (l, 0, j)),
        ],
        out_specs=pl.BlockSpec((None, rows, tn), lambda l, j: (l, 0, j)),
        out_shape=jax.ShapeDtypeStruct((depth, rows, n), F32),
        compiler_params=_cparams(("arbitrary", "arbitrary"), 40),
    )(cvec, w_ada, b_ada.reshape(depth, 1, n))


def _in_proj_kernel(x_ref, g_ref, sh_ref, sc_ref, wm_ref, wt_ref, o_ref, u_ref, a_scr, *, row_chunk,
                    main_tiles):
    tm = x_ref.shape[0]
    j = pl.program_id(1)

    @pl.when(j == 0)
    def _():
        g = g_ref[...]
        sh = sh_ref[...]
        sc1 = 1.0 + sc_ref[...]
        for r in range(0, tm, row_chunk):
            x = x_ref[r:r + row_chunk, :]
            a_scr[r:r + row_chunk, :] = (_rms(x, g) * sc1 + sh).astype(BF16)

    @pl.when(j < main_tiles)
    def _():
        o_ref[...] = jnp.dot(a_scr[...], wm_ref[...], preferred_element_type=F32).astype(o_ref.dtype)

    @pl.when(j == main_tiles)
    def _():
        acc = jnp.dot(a_scr[...], wt_ref[...], preferred_element_type=F32)
        o_ref[...] = acc.astype(o_ref.dtype)
        u_ref[...] = acc[:, :u_ref.shape[1]]


def input_projection(x, gain, mod4, layer, w_main, w_tail, *, tiles_per_seq, n_batch, tm=TM):
    m, d = x.shape
    tn = w_tail.shape[2]
    n = N_PROJ
    main_tiles = COL_U // tn
    assert COL_U % tn == 0 and n - COL_U == tn

    def mod_spec(col):
        return pl.BlockSpec((None, None, 1, d),
                            lambda i, j: (layer, jnp.minimum(i // tiles_per_seq, n_batch), 0, col))

    return pl.pallas_call(
        functools.partial(_in_proj_kernel, row_chunk=min(256, tm), main_tiles=main_tiles),
        grid=(m // tm, n // tn),
        in_specs=[
            pl.BlockSpec((tm, d), lambda i, j: (i, 0)),
            pl.BlockSpec((1, d), lambda i, j: (0, 0)),
            mod_spec(0),
            mod_spec(1),
            pl.BlockSpec((None, d, tn), lambda i, j: (layer, 0, jnp.minimum(j, main_tiles - 1))),
            pl.BlockSpec((None, d, tn), lambda i, j: (layer, 0, 0)),
        ],
        out_specs=[pl.BlockSpec((tm, tn), lambda i, j: (i, j)),
                   pl.BlockSpec((tm, D_SSM), lambda i, j: (i, 0))],
        out_shape=[jax.ShapeDtypeStruct((m, n), BF16), jax.ShapeDtypeStruct((m, D_SSM), F32)],
        scratch_shapes=[pltpu.VMEM((tm, d), BF16)],
        compiler_params=_cparams(("arbitrary", "arbitrary"), 52),
    )(x, gain.reshape(1, d), mod4, mod4, w_main, w_tail)


def _merge_kernel(yc_ref, ya_ref, ys_ref, gna_ref, w_ref, x_ref, gate_ref, g_ref, sh_ref, sc_ref,
                  xo_ref, ho_ref, *, row_chunk):
    tm = x_ref.shape[0]
    kc, ka = yc_ref.shape[1], ya_ref.shape[1]
    gna = gna_ref[...]
    gate = gate_ref[...]
    g = g_ref[...]
    sh = sh_ref[...]
    sc1 = 1.0 + sc_ref[...]
    for r in range(0, tm, row_chunk):
        rows = slice(r, r + row_chunk)
        acc = jnp.dot(yc_ref[rows, :], w_ref[0:kc, :], preferred_element_type=F32)
        ya = _rms(ya_ref[rows, :].astype(F32), gna).astype(BF16)
        acc = acc + jnp.dot(ya, w_ref[kc:kc + ka, :], preferred_element_type=F32)
        acc = acc + jnp.dot(ys_ref[rows, :], w_ref[kc + ka:, :], preferred_element_type=F32)
        x2 = x_ref[rows, :] + gate * acc
        xo_ref[rows, :] = x2
        ho_ref[rows, :] = (_rms(x2, g) * sc1 + sh).astype(ho_ref.dtype)


def merge_projection(y_conv, y_att, y_ssm, gn_att, w_o, x, gain2, mod4, layer, *, m_tiles, tiles_per_seq,
                     n_batch, tm):
    d = x.shape[1]
    rows = m_tiles * tm

    def mod_spec(col):
        return pl.BlockSpec((None, None, 1, d),
                            lambda i: (layer, jnp.minimum(i // tiles_per_seq, n_batch), 0, col))

    a_spec = lambda a: pl.BlockSpec((tm, a.shape[1]), lambda i: (i, 0))
    return pl.pallas_call(
        functools.partial(_merge_kernel, row_chunk=min(256, tm)),
        grid=(m_tiles,),
        in_specs=[
            a_spec(y_conv), a_spec(y_att), a_spec(y_ssm),
            pl.BlockSpec((1, y_att.shape[1]), lambda i: (0, 0)),
            pl.BlockSpec((None, d, d), lambda i: (layer, 0, 0)),
            pl.BlockSpec((tm, d), lambda i: (i, 0)),
            mod_spec(2),
            pl.BlockSpec((1, d), lambda i: (0, 0)),
            mod_spec(3),
            mod_spec(4),
        ],
        out_specs=[pl.BlockSpec((tm, d), lambda i: (i, 0)), pl.BlockSpec((tm, d), lambda i: (i, 0))],
        out_shape=[jax.ShapeDtypeStruct((rows, d), F32), jax.ShapeDtypeStruct((rows, d), BF16)],
        compiler_params=_cparams(("arbitrary",), 52),
    )(y_conv, y_att, y_ssm, gn_att.reshape(1, -1), w_o, x, mod4, gain2.reshape(1, d), mod4, mod4)


def _cast_rows(src_ref, dst_ref, row_chunk):
    for r in range(0, src_ref.shape[0], row_chunk):
        dst_ref[r:r + row_chunk, :] = src_ref[r:r + row_chunk, :].astype(dst_ref.dtype)


def _ffn_up_kernel(h_ref, wg_ref, wu_ref, o_ref, wg_scr, wu_scr):
    @pl.when(pl.program_id(1) == 0)
    def _():
        _cast_rows(wg_ref, wg_scr, 512)
        _cast_rows(wu_ref, wu_scr, 512)

    h = h_ref[...]
    gt = jnp.dot(h, wg_scr[...], preferred_element_type=F32)
    up = jnp.dot(h, wu_scr[...], preferred_element_type=F32)
    o_ref[...] = (gt * jax.nn.sigmoid(gt) * up).astype(o_ref.dtype)


def ffn_up(h, w_gate, w_up, layer, *, m_tiles, tm, tn):
    d = h.shape[1]
    f = w_gate.shape[2]
    w_spec = pl.BlockSpec((None, d, tn), lambda j, i: (layer, 0, j))
    return pl.pallas_call(
        _ffn_up_kernel,
        grid=(f // tn, m_tiles),
        in_specs=[pl.BlockSpec((tm, d), lambda j, i: (i, 0)), w_spec, w_spec],
        out_specs=pl.BlockSpec((tm, tn), lambda j, i: (i, j)),
        out_shape=jax.ShapeDtypeStruct((m_tiles * tm, f), BF16),
        scratch_shapes=[pltpu.VMEM((d, tn), BF16)] * 2,
        compiler_params=_cparams(("arbitrary", "arbitrary"), 56),
    )(h, w_gate, w_up)


def _ffn_down_kernel(a_ref, w_ref, x_ref, gate_ref, o_ref, w_scr):
    @pl.when(pl.program_id(1) == 0)
    def _():
        _cast_rows(w_ref, w_scr, 512)

    acc = jnp.dot(a_ref[...], w_scr[...], preferred_element_type=F32)
    o_ref[...] = x_ref[...] + gate_ref[...] * acc


def ffn_down(a, w_down, x, mod4, layer, *, m_tiles, tiles_per_seq, n_batch, tm, tn):
    f = a.shape[1]
    d = x.shape[1]
    gate_blocks = d // tn
    return pl.pallas_call(
        _ffn_down_kernel,
        grid=(d // tn, m_tiles),
        in_specs=[
            pl.BlockSpec((tm, f), lambda j, i: (i, 0)),
            pl.BlockSpec((None, f, tn), lambda j, i: (layer, 0, j)),
            pl.BlockSpec((tm, tn), lambda j, i: (i, j)),
            pl.BlockSpec((None, None, 1, tn),
                         lambda j, i: (layer, jnp.minimum(i // tiles_per_seq, n_batch), 0,
                                       5 * gate_blocks + j)),
        ],
        out_specs=pl.BlockSpec((tm, tn), lambda j, i: (i, j)),
        out_shape=jax.ShapeDtypeStruct((m_tiles * tm, d), F32),
        scratch_shapes=[pltpu.VMEM((f, tn), BF16)],
        compiler_params=_cparams(("arbitrary", "arbitrary"), 56),
    )(a, w_down, x, mod4)


HALO = 16


def _conv_kernel(h_ref, bg_ref, cg_ref, hp_ref, cp_ref, hn_ref, cn_ref, w_ref, gn_ref, o_ref, z_scr, *,
                 tiles_per_seq, lat_tiles):
    i = pl.program_id(0)
    t = h_ref.shape[0]
    whole_seq = i >= lat_tiles
    first = jnp.logical_or(i % tiles_per_seq == 0, whole_seq)
    last = jnp.logical_or(i % tiles_per_seq == tiles_per_seq - 1, whole_seq)
    z_prev = cp_ref[HALO - 1:HALO, :].astype(F32) * hp_ref[HALO - 1:HALO, :].astype(F32)
    z_next = cn_ref[0:1, :].astype(F32) * hn_ref[0:1, :].astype(F32)
    z_scr[7:8, :] = jnp.where(first, 0.0, z_prev)
    z_scr[8:8 + t, :] = cg_ref[...].astype(F32) * h_ref[...].astype(F32)
    z_scr[8 + t:9 + t, :] = jnp.where(last, 0.0, z_next)
    y = bg_ref[...].astype(F32) * (w_ref[0:1, :] * z_scr[7:7 + t, :] + w_ref[1:2, :] * z_scr[8:8 + t, :]
                                   + w_ref[2:3, :] * z_scr[9:9 + t, :])
    o_ref[...] = _rms(y, gn_ref[...]).astype(o_ref.dtype)


def conv_mixer(proj, conv_w, gn, *, rows, tile, tiles_per_seq, lat_tiles):
    ch = D_CONV
    n_tiles = rows // tile
    hb = tile // HALO
    last_blk = proj.shape[0] // HALO - 1
    main = lambda c: pl.BlockSpec((tile, ch), lambda i: (i, c))
    prev = lambda c: pl.BlockSpec((HALO, ch), lambda i: (jnp.maximum(i * hb - 1, 0), c))
    nxt = lambda c: pl.BlockSpec((HALO, ch), lambda i: (jnp.minimum((i + 1) * hb, last_blk), c))
    return pl.pallas_call(
        functools.partial(_conv_kernel, tiles_per_seq=tiles_per_seq, lat_tiles=lat_tiles),
        grid=(n_tiles,),
        in_specs=[main(0), main(1), main(2), prev(0), prev(2), nxt(0), nxt(2),
                  pl.BlockSpec((CONV_W, ch), lambda i: (0, 0)), pl.BlockSpec((1, ch), lambda i: (0, 0))],
        out_specs=pl.BlockSpec((tile, ch), lambda i: (i, 0)),
        out_shape=jax.ShapeDtypeStruct((rows, ch), BF16),
        scratch_shapes=[pltpu.VMEM((tile + 16, ch), F32)],
        compiler_params=_cparams(("arbitrary",), 32),
    )(proj, proj, proj, proj, proj, proj, proj, conv_w, gn.reshape(1, ch))


def _q_proj_kernel(cq_ref, g_ref, w_ref, cos_ref, sin_ref, o_ref):
    a = _rms(cq_ref[...].astype(F32), g_ref[...]).astype(BF16)
    q = jnp.dot(a, w_ref[...], preferred_element_type=F32)
    cos = cos_ref[...]
    sin = sin_ref[...]
    scale = MLA_SCALE * LOG2E
    for h in range(MLA_HEADS):
        c0 = h * QK_PAD
        o_ref[:, c0:c0 + QK_NOPE] = (q[:, c0:c0 + QK_NOPE] * scale).astype(o_ref.dtype)
        blk = q[:, c0 + QK_NOPE:c0 + QK_PAD]
        rot = blk * cos + pltpu.roll(blk, QK_ROPE, axis=1) * sin
        o_ref[:, c0 + QK_NOPE:c0 + QK_PAD] = (rot * scale).astype(o_ref.dtype)


def q_projection(proj, q_norm, w_q, cos_t, sin_t, *, rows, tm, lat_tiles, tiles_per_seq):
    tab_map = lambda i: (jnp.where(i < lat_tiles, i % tiles_per_seq, tiles_per_seq), 0)
    n = MLA_HEADS * QK_PAD
    return pl.pallas_call(
        _q_proj_kernel,
        grid=(rows // tm,),
        in_specs=[
            pl.BlockSpec((tm, Q_RANK), lambda i: (i, COL_CQ // Q_RANK)),
            pl.BlockSpec((1, Q_RANK), lambda i: (0, 0)),
            pl.BlockSpec((Q_RANK, n), lambda i: (0, 0)),
            pl.BlockSpec((tm, LANE), tab_map),
            pl.BlockSpec((tm, LANE), tab_map),
        ],
        out_specs=pl.BlockSpec((tm, n), lambda i: (i, 0)),
        out_shape=jax.ShapeDtypeStruct((rows, n), BF16),
        compiler_params=_cparams(("arbitrary",), 40),
    )(proj, q_norm.reshape(1, Q_RANK), w_q, cos_t, sin_t)


def _kv_proj_kernel(ckv_ref, kr_ref, g_ref, w_ref, cos_ref, sin_ref, k_ref, v_ref):
    a = _rms(ckv_ref[...].astype(F32), g_ref[...]).astype(BF16)
    kv = jnp.dot(a, w_ref[...], preferred_element_type=F32)
    blk = kr_ref[...].astype(F32)
    rot = (blk * cos_ref[...] + pltpu.roll(blk, QK_ROPE, axis=1) * sin_ref[...]).astype(k_ref.dtype)
    nk = MLA_HEADS * QK_NOPE
    for h in range(MLA_HEADS):
        c0 = h * QK_PAD
        k_ref[:, c0:c0 + QK_NOPE] = kv[:, h * QK_NOPE:(h + 1) * QK_NOPE].astype(k_ref.dtype)
        k_ref[:, c0 + QK_NOPE:c0 + QK_PAD] = rot
    ones = jnp.ones((v_ref.shape[0], V_PAD - V_HEAD), v_ref.dtype)
    for h in range(MLA_HEADS):
        c0 = h * V_PAD
        v_ref[:, c0:c0 + V_HEAD] = kv[:, nk + h * V_HEAD:nk + (h + 1) * V_HEAD].astype(v_ref.dtype)
        v_ref[:, c0 + V_HEAD:c0 + V_PAD] = ones


def kv_projection(proj, kv_norm, w_kv, cos_t, sin_t, *, rows, tm, lat_tiles, tiles_per_seq):
    tab_map = lambda i: (jnp.where(i < lat_tiles, i % tiles_per_seq, tiles_per_seq), 0)
    nk = MLA_HEADS * QK_PAD
    nv = MLA_HEADS * V_PAD
    return pl.pallas_call(
        _kv_proj_kernel,
        grid=(rows // tm,),
        in_specs=[
            pl.BlockSpec((tm, KV_RANK), lambda i: (i, COL_CKV // KV_RANK)),
            pl.BlockSpec((tm, LANE), lambda i: (i, COL_KR // LANE)),
            pl.BlockSpec((1, KV_RANK), lambda i: (0, 0)),
            pl.BlockSpec((KV_RANK, MLA_HEADS * (QK_NOPE + V_HEAD)), lambda i: (0, 0)),
            pl.BlockSpec((tm, LANE), tab_map),
            pl.BlockSpec((tm, LANE), tab_map),
        ],
        out_specs=[pl.BlockSpec((tm, nk), lambda i: (i, 0)), pl.BlockSpec((tm, nv), lambda i: (i, 0))],
        out_shape=[jax.ShapeDtypeStruct((rows, nk), BF16), jax.ShapeDtypeStruct((rows, nv), BF16)],
        compiler_params=_cparams(("arbitrary",), 40),
    )(proj, proj, kv_norm.reshape(1, KV_RANK), w_kv, cos_t, sin_t)


_NT = (((1,), (1,)), ((), ()))


HEAD_SPLIT = 2


def _attn_kernel(q_ref, kl_ref, kc_ref, vl_ref, vc_ref, o_ref, *, lat_steps, steps_per_batch, ctx_len):
    s = pl.program_id(0)
    hpg = MLA_HEADS // HEAD_SPLIT

    def heads(q_rows, c_rows, with_latent):
        for h in range(hpg):
            q = q_ref[q_rows, h * QK_PAD:(h + 1) * QK_PAD]
            sc = lax.dot_general(q, kc_ref[c_rows, h * QK_PAD:(h + 1) * QK_PAD], _NT,
                                 preferred_element_type=F32)
            m = jnp.max(sc, axis=-1, keepdims=True)
            if with_latent:
                sl = lax.dot_general(q, kl_ref[:, h * QK_PAD:(h + 1) * QK_PAD], _NT,
                                     preferred_element_type=F32)
                m = jnp.maximum(m, jnp.max(sl, axis=-1, keepdims=True))
            o = jnp.dot(jnp.exp2(sc - m).astype(BF16), vc_ref[c_rows, h * V_PAD:(h + 1) * V_PAD],
                        preferred_element_type=F32)
            if with_latent:
                o = o + jnp.dot(jnp.exp2(sl - m).astype(BF16), vl_ref[:, h * V_PAD:(h + 1) * V_PAD],
                                preferred_element_type=F32)
            o_ref[q_rows, h * V_HEAD:(h + 1) * V_HEAD] = (o[:, :V_HEAD] / o[:, V_HEAD:]).astype(o_ref.dtype)

    @pl.when(s < lat_steps)
    def _():
        b = s // steps_per_batch
        c0 = pl.multiple_of((b % 2) * ctx_len, ctx_len)
        heads(slice(None), pl.ds(c0, ctx_len), True)

    @pl.when(s >= lat_steps)
    def _():
        for hb in range(2):
            rows = slice(hb * ctx_len, (hb + 1) * ctx_len)
            heads(rows, rows, False)


def latent_attention(q, k, v, *, n_batch, seq, ctx_len, tq, ctx_queries, rows_out):
    assert tq == 2 * ctx_len and n_batch % 2 == 0 and seq % tq == 0
    lat_tiles = seq // tq
    spb = HEAD_SPLIT * lat_tiles
    lat_steps = n_batch * spb
    ctx_steps = (n_batch // 2) * HEAD_SPLIT if ctx_queries else 0
    ctx_blk0 = n_batch * seq // tq
    nq = MLA_HEADS * QK_PAD // HEAD_SPLIT
    nv = MLA_HEADS * V_HEAD // HEAD_SPLIT
    nvp = MLA_HEADS * V_PAD // HEAD_SPLIT

    def split(s):
        lat = s < lat_steps
        cs = s - lat_steps
        b = jnp.where(lat, s // spb, n_batch - 1)
        hh = jnp.where(lat, (s // lat_tiles) % HEAD_SPLIT, cs % HEAD_SPLIT)
        q_blk = jnp.where(lat, b * lat_tiles + s % lat_tiles, ctx_blk0 + cs // HEAD_SPLIT)
        c_blk = jnp.where(lat, ctx_blk0 + b // 2, ctx_blk0 + cs // HEAD_SPLIT)
        return b, hh, q_blk, c_blk

    def lat_map(s):
        b, hh, _, _ = split(s)
        return (b, jnp.where(s < lat_steps, hh, HEAD_SPLIT - 1))

    q_map = lambda s: (split(s)[2], split(s)[1])
    c_map = lambda s: (split(s)[3], split(s)[1])
    return pl.pallas_call(
        functools.partial(_attn_kernel, lat_steps=lat_steps, steps_per_batch=spb, ctx_len=ctx_len),
        grid=(lat_steps + ctx_steps,),
        in_specs=[
            pl.BlockSpec((tq, nq), q_map),
            pl.BlockSpec((seq, nq), lat_map),
            pl.BlockSpec((tq, nq), c_map),
            pl.BlockSpec((seq, nvp), lat_map),
            pl.BlockSpec((tq, nvp), c_map),
        ],
        out_specs=pl.BlockSpec((tq, nv), q_map),
        out_shape=jax.ShapeDtypeStruct((rows_out, MLA_HEADS * V_HEAD), BF16),
        compiler_params=_cparams(("arbitrary",), 48),
    )(q, k, k, v, v)


def _s5_disc_kernel(are_ref, aim_ref, ldt_ref, abr_ref, abi_ref, cfr_ref, cfi_ref):
    ar = are_ref[...]
    ai = aim_ref[...]
    dt = jnp.exp(ldt_ref[...])
    mag = jnp.exp(ar * dt)
    th = ai * dt
    br = mag * jnp.cos(th)
    bi = mag * jnp.sin(th)
    nr = br - 1.0
    den = ar * ar + ai * ai
    abr_ref[...] = br
    abi_ref[...] = bi
    cfr_ref[...] = (nr * ar + bi * ai) / den
    cfi_ref[...] = (bi * ar - nr * ai) / den


def s5_discretise(a_re, a_im, log_dt):
    shp = a_re.shape
    rows = shp[0] * shp[1] * shp[2]
    flat = lambda t: t.reshape(rows, shp[3])
    ldt = jnp.broadcast_to(log_dt[..., None], shp)
    outs = pl.pallas_call(
        _s5_disc_kernel,
        out_shape=[jax.ShapeDtypeStruct((rows, shp[3]), F32)] * 4,
    )(flat(a_re), flat(a_im), flat(ldt))
    return [o.reshape(shp) for o in outs]


def _cmul(xr, xi, yr, yi):
    return xr * yr - xi * yi, xr * yi + xi * yr


def _s5_mats_kernel(abc_re, abc_im, cex_re, cex_im, afb_re, afb_im, cfb_re, cfb_im, bfb_re, bfb_im,
                    cpp_re, cpp_im, bx_ref, by_ref, ma_ref, mi_ref, mo_ref, a16_ref):
    tw = CHUNK * SSM_GROUP
    p = SSM_STATE
    lane = lax.broadcasted_iota(jnp.int32, (p, tw), 1)
    tblk = lane // SSM_GROUP
    lane16 = lax.broadcasted_iota(jnp.int32, (SSM_GROUP, tw), 1)
    lane128 = lax.broadcasted_iota(jnp.int32, (1, 2 * p), 1)
    fwd_lanes = lane128 < p
    sgn = jnp.where(lax.broadcasted_iota(jnp.int32, (SSM_GROUP, 2 * p), 1) < p, 1.0, -1.0)
    zeros = jnp.zeros((p, tw), F32)
    mi_blocks = []
    for d in range(2):
        ar = jnp.broadcast_to(abc_re[d], (p, tw))
        ai = jnp.broadcast_to(abc_im[d], (p, tw))
        sel_r = tblk if d == 0 else (CHUNK - 1) - tblk
        sq_re, sq_im = ar, ai
        r_re, r_im = jnp.ones((p, tw), F32), zeros
        for b in range((CHUNK - 1).bit_length()):
            bit = ((sel_r >> b) & 1) == 1
            n_re, n_im = _cmul(r_re, r_im, sq_re, sq_im)
            r_re = jnp.where(bit, n_re, r_re)
            r_im = jnp.where(bit, n_im, r_im)
            sq_re, sq_im = _cmul(sq_re, sq_im, sq_re, sq_im)
        e_re, e_im = _cmul(r_re, r_im, ar, ai)
        cr = cex_re[d]
        ci = cex_im[d]
        w_re, w_im = _cmul(cr, ci, e_re, e_im)
        base = d * 4 * p
        mdt = mo_ref.dtype
        zeros_m = zeros.astype(mdt)
        if d == 0:
            mo_ref[base:base + p, :] = w_re.astype(mdt)
            mo_ref[base + p:base + 2 * p, :] = zeros_m
            mo_ref[base + 2 * p:base + 3 * p, :] = (-w_im).astype(mdt)
            mo_ref[base + 3 * p:base + 4 * p, :] = zeros_m
        else:
            mo_ref[base:base + p, :] = zeros_m
            mo_ref[base + p:base + 2 * p, :] = w_re.astype(mdt)
            mo_ref[base + 2 * p:base + 3 * p, :] = zeros_m
            mo_ref[base + 3 * p:base + 4 * p, :] = (-w_im).astype(mdt)
        rr_re, rr_im = _cmul(cr, ci, r_re, r_im)
        stacked = jnp.concatenate([rr_re, rr_im], axis=0)
        lm = sgn * cpp_re[d] * bx_ref[d] - cpp_im[d] * by_ref[d]
        kall = jnp.dot(lm, stacked, preferred_element_type=F32, precision=lax.Precision.HIGHEST)
        for s in range(CHUNK):
            if d == 0:
                shift = SSM_GROUP * s
                keep = lane16 >= SSM_GROUP * s
            else:
                shift = (SSM_GROUP * (s + 1)) % tw
                keep = lane16 < SSM_GROUP * (s + 1)
            rolled = pltpu.roll(kall, shift, axis=1) if shift else kall
            blk = jnp.where(keep, rolled, 0.0)
            if d == 0:
                mi_blocks.append(blk)
            else:
                mi_ref[SSM_GROUP * s:SSM_GROUP * (s + 1), :] = (mi_blocks[s] + blk).astype(mi_ref.dtype)

    a_re = afb_re[...]
    a_im = afb_im[...]
    cf_re = cfb_re[...]
    cf_im = cfb_im[...]
    b_re = bfb_re[...]
    b_im = bfb_im[...]
    powers = []
    qr, qi = jnp.ones((1, 2 * p), F32), jnp.zeros((1, 2 * p), F32)
    for k in range(CHUNK + 1):
        powers.append((qr, qi))
        if k < CHUNK:
            qr, qi = _cmul(qr, qi, a_re, a_im)
    for s in range(CHUNK):
        g_re = jnp.where(fwd_lanes, powers[CHUNK - 1 - s][0], powers[s][0])
        g_im = jnp.where(fwd_lanes, powers[CHUNK - 1 - s][1], powers[s][1])
        g_re, g_im = _cmul(g_re, g_im, cf_re, cf_im)
        rows = slice(SSM_GROUP * s, SSM_GROUP * (s + 1))
        ma_ref[rows, 0:2 * p] = (g_re * b_re - g_im * b_im).astype(ma_ref.dtype)
        ma_ref[rows, 2 * p:4 * p] = (g_re * b_im + g_im * b_re).astype(ma_ref.dtype)
    a16_ref[0:1, :] = powers[CHUNK][0]
    a16_ref[1:2, :] = powers[CHUNK][1]


def s5_matrices(abar_re, abar_im, coef_re, coef_im, b_re, b_im, c_re, c_im):
    depth, _, g, p = abar_re.shape
    hg = b_re.shape[-1]
    tw = CHUNK * hg
    t0213 = lambda t: t.transpose(0, 2, 1, 3)
    abc = [t0213(t)[..., None] for t in (abar_re, abar_im)]
    cex = [jnp.tile(t.transpose(0, 2, 1, 4, 3), (1, 1, 1, 1, CHUNK)) for t in (c_re, c_im)]
    fb = lambda t: t0213(t).reshape(depth, g, 1, 2 * p)
    afb = [fb(t) for t in (abar_re, abar_im)]
    cfb = [fb(t) for t in (coef_re, coef_im)]
    bfb = [t.transpose(0, 2, 4, 1, 3).reshape(depth, g, hg, 2 * p) for t in (b_re, b_im)]
    cpp = [t0213(jnp.concatenate([t, t], axis=-1))[:, :, :, None, :] for t in (coef_re, coef_im)]
    bt_re = b_re.transpose(0, 2, 1, 4, 3)
    bt_im = b_im.transpose(0, 2, 1, 4, 3)
    bx = jnp.concatenate([bt_re, bt_im], axis=-1)
    by = jnp.concatenate([bt_im, bt_re], axis=-1)
    ins = [*abc, *cex, *afb, *cfb, *bfb, *cpp, bx, by]

    def spec(t):
        blk = (None, None) + t.shape[2:]
        nz = len(t.shape) - 2
        return pl.BlockSpec(blk, lambda l, gi: (l, gi) + (0,) * nz)

    def ospec(r, c):
        return pl.BlockSpec((None, None, r, c), lambda l, gi: (l, gi, 0, 0))

    return pl.pallas_call(
        _s5_mats_kernel,
        grid=(depth, g),
        in_specs=[spec(t) for t in ins],
        out_specs=[ospec(tw, 4 * p), ospec(tw, tw), ospec(8 * p, tw), ospec(2, 2 * p)],
        out_shape=[
            jax.ShapeDtypeStruct((depth, g, tw, 4 * p), BF16),
            jax.ShapeDtypeStruct((depth, g, tw, tw), BF16),
            jax.ShapeDtypeStruct((depth, g, 8 * p, tw), BF16),
            jax.ShapeDtypeStruct((depth, g, 2, 2 * p), F32),
        ],
        compiler_params=_cparams(("arbitrary", "arbitrary"), 32),
    )(*ins)


GROUP_BLOCK = LANE // SSM_GROUP


def lane_swap_matrix():
    idx = jnp.arange(GROUP_BLOCK * LANE)
    a, b, c = idx // LANE, (idx // SSM_GROUP) % GROUP_BLOCK, idx % SSM_GROUP
    dst = b * LANE + a * SSM_GROUP + c
    return (dst[:, None] == idx[None, :]).astype(BF16)


def _s5_main_kernel(*refs, order_f, order_b, nb, lat_chunks, ctx_chunks):
    (u_ref, p_ref, ma_ref, mi_ref, mo_ref, a_ref, y_ref, ug_scr, yg_scr,
     xr_scr, xi_scr, fr_scr, fi_scr, br_scr, bi_scr) = refs
    p2 = 2 * SSM_STATE
    half = CHUNK // 2
    rows = u_ref.shape[0] // CHUNK
    n_sub = xr_scr.shape[0]

    for th in range(2):
        slab = jnp.concatenate(
            [u_ref[pl.ds(half * th + tl, rows, stride=CHUNK), :].astype(BF16) for tl in range(half)], axis=1)
        perm = jnp.dot(slab, p_ref[...], preferred_element_type=F32).astype(BF16)
        for g in range(GROUP_BLOCK):
            ug_scr[g, :, th * LANE:(th + 1) * LANE] = perm[:, g * LANE:(g + 1) * LANE]

    def rows_of(j):
        if j < lat_chunks:
            return pl.ds(j, nb, stride=lat_chunks)
        return pl.ds(nb * lat_chunks + (j - lat_chunks), nb, stride=ctx_chunks)

    fwd_lanes = lax.broadcasted_iota(jnp.int32, (nb, p2), 1) < SSM_STATE
    for g0 in range(0, GROUP_BLOCK, n_sub):
        for k in range(n_sub):
            x = jnp.dot(ug_scr[g0 + k], ma_ref[g0 + k], preferred_element_type=F32)
            xr_scr[k] = x[:, 0:p2]
            xi_scr[k] = x[:, p2:2 * p2]

        a_re = [a_ref[g0 + k, 0:1, :] for k in range(n_sub)]
        a_im = [a_ref[g0 + k, 1:2, :] for k in range(n_sub)]
        s_re = [jnp.zeros((nb, p2), F32) for _ in range(n_sub)]
        s_im = [jnp.zeros((nb, p2), F32) for _ in range(n_sub)]
        for jf, jb in zip(order_f, order_b):
            rf = rows_of(jf)
            rb = rows_of(jb)
            for k in range(n_sub):
                fr_scr[k, rf, :] = s_re[k]
                fi_scr[k, rf, :] = s_im[k]
                br_scr[k, rb, :] = s_re[k]
                bi_scr[k, rb, :] = s_im[k]
                xr = jnp.where(fwd_lanes, xr_scr[k, rf, :], xr_scr[k, rb, :])
                xi = jnp.where(fwd_lanes, xi_scr[k, rf, :], xi_scr[k, rb, :])
                s_re[k], s_im[k] = (a_re[k] * s_re[k] - a_im[k] * s_im[k] + xr,
                                    a_re[k] * s_im[k] + a_im[k] * s_re[k] + xi)

        for k in range(n_sub):
            g = g0 + k
            y = jnp.dot(ug_scr[g], mi_ref[g], preferred_element_type=F32)
            sp_f = jnp.concatenate([fr_scr[k], fi_scr[k]], axis=1).astype(BF16)
            sp_b = jnp.concatenate([br_scr[k], bi_scr[k]], axis=1).astype(BF16)
            y = y + jnp.dot(sp_f, mo_ref[g, 0:2 * p2, :], preferred_element_type=F32)
            y = y + jnp.dot(sp_b, mo_ref[g, 2 * p2:4 * p2, :], preferred_element_type=F32)
            yg_scr[g] = y.astype(BF16)

    for th in range(2):
        slab = jnp.concatenate([yg_scr[g, :, th * LANE:(th + 1) * LANE] for g in range(GROUP_BLOCK)], axis=1)
        perm = jnp.dot(slab, p_ref[...], preferred_element_type=F32)
        for tl in range(half):
            y_ref[pl.ds(half * th + tl, rows, stride=CHUNK), :] = perm[:, tl * LANE:(tl + 1) * LANE]


def s5_chunked(u, ma, mi, mo, a16, layer, *, lat_chunks, ctx_chunks, nb):
    r_all = u.shape[0]
    rows = r_all // CHUNK
    tw = CHUNK * SSM_GROUP
    j_tot = lat_chunks + ctx_chunks
    order_f = tuple(range(lat_chunks, j_tot)) + tuple(range(lat_chunks))
    order_b = tuple(range(j_tot - 1, lat_chunks - 1, -1)) + tuple(range(lat_chunks - 1, -1, -1))
    p4 = 4 * SSM_STATE
    n_gb = SSM_GROUPS // GROUP_BLOCK
    n_sub = GROUP_BLOCK // 2

    def wspec(r, c):
        return pl.BlockSpec((None, GROUP_BLOCK, r, c), lambda gb: (layer, gb, 0, 0))

    return pl.pallas_call(
        functools.partial(_s5_main_kernel, order_f=order_f, order_b=order_b, nb=nb,
                          lat_chunks=lat_chunks, ctx_chunks=ctx_chunks),
        grid=(n_gb,),
        in_specs=[
            pl.BlockSpec((r_all, LANE), lambda gb: (0, gb)),
            pl.BlockSpec((GROUP_BLOCK * LANE, GROUP_BLOCK * LANE), lambda gb: (0, 0)),
            wspec(tw, p4), wspec(tw, tw), wspec(2 * p4, tw), wspec(2, 2 * SSM_STATE),
        ],
        out_specs=pl.BlockSpec((r_all, LANE), lambda gb: (0, gb)),
        out_shape=jax.ShapeDtypeStruct((r_all, D_SSM), F32),
        scratch_shapes=[
            pltpu.VMEM((GROUP_BLOCK, rows, tw), BF16),
            pltpu.VMEM((GROUP_BLOCK, rows, tw), BF16),
        ] + [pltpu.VMEM((n_sub, rows, 2 * SSM_STATE), F32)] * 6,
        compiler_params=_cparams(("arbitrary",), 56),
    )(u, lane_swap_matrix(), ma, mi, mo, a16)


def _s5_out_kernel(y_ref, u_ref, d_ref, w_ref, b_ref, gn_ref, o_ref):
    y = y_ref[...].astype(F32) + d_ref[...] * u_ref[...].astype(F32)
    g = jax.nn.gelu(y)
    z = jnp.dot(g.astype(BF16), w_ref[...], preferred_element_type=F32) + b_ref[...]
    o_ref[...] = _rms(g * jax.nn.sigmoid(z), gn_ref[...]).astype(o_ref.dtype)


def s5_output(y, u, ssm_d, w_glu, b_glu, gn, *, rows, tm):
    ch = D_SSM
    vec = lambda: pl.BlockSpec((1, ch), lambda i: (0, 0))
    return pl.pallas_call(
        _s5_out_kernel,
        grid=(rows // tm,),
        in_specs=[
            pl.BlockSpec((tm, ch), lambda i: (i, 0)),
            pl.BlockSpec((tm, ch), lambda i: (i, 0)),
            vec(),
            pl.BlockSpec((ch, ch), lambda i: (0, 0)),
            vec(), vec(),
        ],
        out_specs=pl.BlockSpec((tm, ch), lambda i: (i, 0)),
        out_shape=jax.ShapeDtypeStruct((rows, ch), BF16),
        compiler_params=_cparams(("arbitrary",), 32),
    )(y, u, ssm_d.reshape(1, ch), w_glu, b_glu.reshape(1, ch), gn.reshape(1, ch))


def _final_norm_kernel(x_ref, g_ref, o_ref):
    o_ref[...] = _rms(x_ref[...], g_ref[...])


def final_rms_norm(x, g, *, rows, tm):
    d = x.shape[1]
    return pl.pallas_call(
        _final_norm_kernel,
        grid=(rows // tm,),
        in_specs=[pl.BlockSpec((tm, d), lambda i: (i, 0)), pl.BlockSpec((1, d), lambda i: (0, 0))],
        out_specs=pl.BlockSpec((tm, d), lambda i: (i, 0)),
        out_shape=jax.ShapeDtypeStruct((rows, d), F32),
        compiler_params=_cparams(("arbitrary",), 32),
    )(x, g.reshape(1, d))


def _rope_partner_perm():
    idx = []
    for i in range(QK_ROPE):
        idx.append(i + 16 if (i % 32) < 16 else i - 16)
    return jnp.asarray(idx, jnp.int32)


def rope_tables(seq, tab_tile):
    rows = seq // GRID_W
    row = jnp.repeat(jnp.arange(rows, dtype=F32), GRID_W)
    col = jnp.tile(jnp.arange(GRID_W, dtype=F32), rows)
    n_freq = QK_ROPE // 4
    inv = ROPE_BASE ** (-jnp.arange(n_freq, dtype=F32) / n_freq)
    ar = row[:, None] * inv
    ac = col[:, None] * inv
    cos = jnp.concatenate([jnp.cos(ar), jnp.cos(ar), jnp.cos(ac), jnp.cos(ac)], axis=1)
    sin = jnp.concatenate([-jnp.sin(ar), jnp.sin(ar), -jnp.sin(ac), jnp.sin(ac)], axis=1)
    cos = jnp.concatenate([cos, jnp.ones((tab_tile, QK_ROPE), F32)], axis=0)
    sin = jnp.concatenate([sin, jnp.zeros((tab_tile, QK_ROPE), F32)], axis=0)
    pad = jnp.zeros((seq + tab_tile, LANE - QK_ROPE), F32)
    return jnp.concatenate([cos, pad], axis=1), jnp.concatenate([sin, pad], axis=1)


def prep_w_in(w_in):
    assert COL_U == COL_CQ + Q_RANK
    s = [COL_U, COL_U + KV_RANK, COL_U + KV_RANK + QK_ROPE, COL_U + KV_RANK + QK_ROPE + D_SSM]
    ckv, kr, u = [w_in[..., s[i]:s[i + 1]] for i in range(3)]
    krp = kr[..., _rope_partner_perm()]
    pad = jnp.zeros(w_in.shape[:-1] + (N_PROJ - (COL_KR + LANE),), w_in.dtype)
    tail = jnp.concatenate([u, ckv, kr, krp, pad], axis=-1).astype(BF16)
    return w_in.astype(BF16), tail


def prep_w_uq(w_uq):
    r = w_uq.shape[0]
    w = w_uq.reshape(r, MLA_HEADS, QK_NOPE + QK_ROPE)
    rope = w[:, :, QK_NOPE:]
    w = jnp.concatenate([w, rope[:, :, _rope_partner_perm()]], axis=-1)
    return w.reshape(r, MLA_HEADS * QK_PAD).astype(BF16)


def prep_w_ukv(w_ukv):
    r = w_ukv.shape[0]
    w = w_ukv.reshape(r, MLA_HEADS, QK_NOPE + V_HEAD)
    kn = w[:, :, :QK_NOPE].reshape(r, MLA_HEADS * QK_NOPE)
    v = w[:, :, QK_NOPE:].reshape(r, MLA_HEADS * V_HEAD)
    return jnp.concatenate([kn, v], axis=1).astype(BF16)


def kernel(x, c, ctx, c_ctx, w_ada, b_ada, norm1_g, norm2_g, w_in, conv_w, mla_q_norm, w_uq, mla_kv_norm, w_ukv, ssm_a_re, ssm_a_im, ssm_log_dt, ssm_b_re, ssm_b_im, ssm_c_re, ssm_c_im, ssm_d, w_glu, b_glu, mix_norm, w_o, w_gate, w_up, w_down, final_norm):
    nb, seq, d = x.shape
    ctx_len = ctx.shape[1]
    depth = w_ada.shape[0]
    r_lat = nb * seq
    r_ctx = nb * ctx_len
    r_all = r_lat + r_ctx
    assert seq % TM == 0 and r_ctx == TM and seq % CHUNK == 0 and ctx_len % CHUNK == 0
    tiles_per_seq = seq // TM
    lat_tiles_m = r_lat // TM
    d_ff = w_gate.shape[2]

    xs = jnp.concatenate([x.reshape(r_lat, d), ctx.reshape(r_ctx, d)], axis=0)

    cvec = jnp.concatenate([c, c_ctx[None, :], jnp.zeros((8 - nb - 1, d), F32)], axis=0)
    mod = ada_modulation(cvec, w_ada, b_ada)
    mod4 = mod.reshape(depth, 8, 1, 6 * d)

    tp = 512
    cos_t, sin_t = rope_tables(seq, tp)
    abar_re, abar_im, coef_re, coef_im = s5_discretise(ssm_a_re, ssm_a_im, ssm_log_dt)
    ma, mi, mo, a16 = s5_matrices(abar_re, abar_im, coef_re, coef_im,
                                  ssm_b_re, ssm_b_im, ssm_c_re, ssm_c_im)
    w_in_main, w_in_tail = prep_w_in(w_in)
    w_o_b = w_o.astype(BF16)
    lat_chunks = seq // CHUNK
    ctx_chunks = ctx_len // CHUNK
    tw = CHUNK * SSM_GROUP

    for i in range(depth):
        ctx_out = i < depth - 1
        m_tiles = lat_tiles_m + (1 if ctx_out else 0)
        rows_out = r_all if ctx_out else r_lat
        tf = 512

        proj, u32 = input_projection(xs, norm1_g[i], mod4, i, w_in_main, w_in_tail,
                                     tiles_per_seq=tiles_per_seq, n_batch=nb)

        y_conv = conv_mixer(proj, conv_w[i], mix_norm[i, :D_CONV], rows=rows_out, tile=ctx_len,
                            tiles_per_seq=seq // ctx_len, lat_tiles=r_lat // ctx_len)

        q_rows = rows_out
        q = q_projection(proj, mla_q_norm[i], prep_w_uq(w_uq[i]), cos_t, sin_t, rows=q_rows, tm=tp,
                         lat_tiles=r_lat // tp, tiles_per_seq=seq // tp)
        k, v = kv_projection(proj, mla_kv_norm[i], prep_w_ukv(w_ukv[i]), cos_t, sin_t, rows=r_all,
                             tm=tp, lat_tiles=r_lat // tp, tiles_per_seq=seq // tp)
        y_att = latent_attention(q, k, v, n_batch=nb, seq=seq, ctx_len=ctx_len, tq=2 * ctx_len,
                                 ctx_queries=ctx_out, rows_out=rows_out)

        y_tok = s5_chunked(u32, ma, mi, mo, a16, i, lat_chunks=lat_chunks, ctx_chunks=ctx_chunks, nb=nb)
        y_ssm = s5_output(y_tok, u32, ssm_d[i], w_glu[i].astype(BF16), b_glu[i],
                          mix_norm[i, D_CONV + D_ATTN:], rows=rows_out, tm=tp)

        x2, h2 = merge_projection(y_conv, y_att, y_ssm, mix_norm[i, D_CONV:D_CONV + D_ATTN], w_o_b, xs,
                                  norm2_g[i], mod4, i,
                                  m_tiles=rows_out // tf, tiles_per_seq=seq // tf, n_batch=nb, tm=tf)
        hidden = ffn_up(h2, w_gate, w_up, i, m_tiles=m_tiles, tm=TM, tn=512)
        xs = ffn_down(hidden, w_down, x2, mod4, i, m_tiles=rows_out // tf, tiles_per_seq=seq // tf,
                      n_batch=nb, tm=tf, tn=512)

    out = final_rms_norm(xs, final_norm, rows=r_lat, tm=tp)
    return out.reshape(nb, seq, d)
```

```python
import functools
import math

import jax
import jax.numpy as jnp
from jax import lax
from jax.experimental import pallas as pl
from jax.experimental.pallas import tpu as pltpu

F32 = jnp.float32
BF16 = jnp.bfloat16

EPS = 1e-6
GRID_W = 64
CONV_W = 3
D_CONV = 512
D_SSM = 512
D_ATTN = 1024
MLA_HEADS = 8
QK_NOPE = 128
QK_ROPE = 64
V_HEAD = 128
Q_RANK = 512
KV_RANK = 256
ROPE_BASE = 10000.0
MLA_SCALE = (QK_NOPE + QK_ROPE) ** -0.5
SSM_GROUP = 16
SSM_GROUPS = 32
SSM_STATE = 64
CHUNK = 16

QK_PAD = 256
V_PAD = 256
LOG2E = math.log2(math.e)
LANE = 128
TM = 1024
MIB = 1024 * 1024

COL_CONV = 0
COL_CQ = 3 * D_CONV
COL_U = COL_CQ + Q_RANK
COL_CKV = COL_U + D_SSM
COL_KR = COL_CKV + KV_RANK
N_PROJ = 3072


def _cparams(sem, vmem_mib):
    return pltpu.CompilerParams(dimension_semantics=sem, vmem_limit_bytes=vmem_mib * MIB)


def _rms(x, g):
    return x * lax.rsqrt(jnp.mean(x * x, axis=-1, keepdims=True) + EPS) * g


def _ada_kernel(c_ref, w_ref, b_ref, o_ref):
    cv = c_ref[...]
    s = (cv * jax.nn.sigmoid(cv)).astype(BF16)
    o_ref[...] = jnp.dot(s, w_ref[...].astype(BF16), preferred_element_type=F32) + b_ref[...]


def ada_modulation(cvec, w_ada, b_ada, tn=1024):
    depth, d, n = w_ada.shape
    rows = cvec.shape[0]
    return pl.pallas_call(
        _ada_kernel,
        grid=(depth, n // tn),
        in_specs=[
            pl.BlockSpec((rows, d), lambda l, j: (0, 0)),
            pl.BlockSpec((None, d, tn), lambda l, j: (l, 0, j)),
            pl.BlockSpec((None, 1, tn), lambda l, j: (l, 0, j)),
        ],
        out_specs=pl.BlockSpec((None, rows, tn), lambda l, j: (l, 0, j)),
        out_shape=jax.ShapeDtypeStruct((depth, rows, n), F32),
        compiler_params=_cparams(("arbitrary", "arbitrary"), 40),
    )(cvec, w_ada, b_ada.reshape(depth, 1, n))


def _in_proj_kernel(x_ref, g_ref, sh_ref, sc_ref, wm_ref, wt_ref, o_ref, u_ref, a_scr, *, row_chunk,
                    main_tiles):
    tm = x_ref.shape[0]
    j = pl.program_id(1)

    @pl.when(j == 0)
    def _():
        g = g_ref[...]
        sh = sh_ref[...]
        sc1 = 1.0 + sc_ref[...]
        for r in range(0, tm, row_chunk):
            x = x_ref[r:r + row_chunk, :]
            a_scr[r:r + row_chunk, :] = (_rms(x, g) * sc1 + sh).astype(BF16)

    @pl.when(j < main_tiles)
    def _():
        o_ref[...] = jnp.dot(a_scr[...], wm_ref[...], preferred_element_type=F32).astype(o_ref.dtype)

    @pl.when(j == main_tiles)
    def _():
        acc = jnp.dot(a_scr[...], wt_ref[...], preferred_element_type=F32)
        o_ref[...] = acc.astype(o_ref.dtype)
        u_ref[...] = acc[:, :u_ref.shape[1]]


def input_projection(x, gain, mod4, layer, w_main, w_tail, *, tiles_per_seq, n_batch, tm=TM):
    m, d = x.shape
    tn = w_tail.shape[2]
    n = N_PROJ
    main_tiles = COL_U // tn
    assert COL_U % tn == 0 and n - COL_U == tn

    def mod_spec(col):
        return pl.BlockSpec((None, None, 1, d),
                            lambda i, j: (layer, jnp.minimum(i // tiles_per_seq, n_batch), 0, col))

    return pl.pallas_call(
        functools.partial(_in_proj_kernel, row_chunk=min(256, tm), main_tiles=main_tiles),
        grid=(m // tm, n // tn),
        in_specs=[
            pl.BlockSpec((tm, d), lambda i, j: (i, 0)),
            pl.BlockSpec((1, d), lambda i, j: (0, 0)),
            mod_spec(0),
            mod_spec(1),
            pl.BlockSpec((None, d, tn), lambda i, j: (layer, 0, jnp.minimum(j, main_tiles - 1))),
            pl.BlockSpec((None, d, tn), lambda i, j: (layer, 0, 0)),
        ],
        out_specs=[pl.BlockSpec((tm, tn), lambda i, j: (i, j)),
                   pl.BlockSpec((tm, D_SSM), lambda i, j: (i, 0))],
        out_shape=[jax.ShapeDtypeStruct((m, n), BF16), jax.ShapeDtypeStruct((m, D_SSM), F32)],
        scratch_shapes=[pltpu.VMEM((tm, d), BF16)],
        compiler_params=_cparams(("arbitrary", "arbitrary"), 52),
    )(x, gain.reshape(1, d), mod4, mod4, w_main, w_tail)


def _merge_kernel(yc_ref, ya_ref, ys_ref, gna_ref, w_ref, x_ref, gate_ref, g_ref, sh_ref, sc_ref,
                  xo_ref, ho_ref, *, row_chunk):
    tm = x_ref.shape[0]
    kc, ka = yc_ref.shape[1], ya_ref.shape[1]
    gna = gna_ref[...]
    gate = gate_ref[...]
    g = g_ref[...]
    sh = sh_ref[...]
    sc1 = 1.0 + sc_ref[...]
    for r in range(0, tm, row_chunk):
        rows = slice(r, r + row_chunk)
        acc = jnp.dot(yc_ref[rows, :], w_ref[0:kc, :], preferred_element_type=F32)
        ya = _rms(ya_ref[rows, :].astype(F32), gna).astype(BF16)
        acc = acc + jnp.dot(ya, w_ref[kc:kc + ka, :], preferred_element_type=F32)
        acc = acc + jnp.dot(ys_ref[rows, :], w_ref[kc + ka:, :], preferred_element_type=F32)
        x2 = x_ref[rows, :] + gate * acc
        xo_ref[rows, :] = x2
        ho_ref[rows, :] = (_rms(x2, g) * sc1 + sh).astype(ho_ref.dtype)


def merge_projection(y_conv, y_att, y_ssm, gn_att, w_o, x, gain2, mod4, layer, *, m_tiles, tiles_per_seq,
                     n_batch, tm):
    d = x.shape[1]
    rows = m_tiles * tm

    def mod_spec(col):
        return pl.BlockSpec((None, None, 1, d),
                            lambda i: (layer, jnp.minimum(i // tiles_per_seq, n_batch), 0, col))

    a_spec = lambda a: pl.BlockSpec((tm, a.shape[1]), lambda i: (i, 0))
    return pl.pallas_call(
        functools.partial(_merge_kernel, row_chunk=min(256, tm)),
        grid=(m_tiles,),
        in_specs=[
            a_spec(y_conv), a_spec(y_att), a_spec(y_ssm),
            pl.BlockSpec((1, y_att.shape[1]), lambda i: (0, 0)),
            pl.BlockSpec((None, d, d), lambda i: (layer, 0, 0)),
            pl.BlockSpec((tm, d), lambda i: (i, 0)),
            mod_spec(2),
            pl.BlockSpec((1, d), lambda i: (0, 0)),
            mod_spec(3),
            mod_spec(4),
        ],
        out_specs=[pl.BlockSpec((tm, d), lambda i: (i, 0)), pl.BlockSpec((tm, d), lambda i: (i, 0))],
        out_shape=[jax.ShapeDtypeStruct((rows, d), F32), jax.ShapeDtypeStruct((rows, d), BF16)],
        compiler_params=_cparams(("arbitrary",), 52),
    )(y_conv, y_att, y_ssm, gn_att.reshape(1, -1), w_o, x, mod4, gain2.reshape(1, d), mod4, mod4)


def _cast_rows(src_ref, dst_ref, row_chunk):
    for r in range(0, src_ref.shape[0], row_chunk):
        dst_ref[r:r + row_chunk, :] = src_ref[r:r + row_chunk, :].astype(dst_ref.dtype)


def _ffn_up_kernel(h_ref, wg_ref, wu_ref, o_ref, wg_scr, wu_scr):
    @pl.when(pl.program_id(1) == 0)
    def _():
        _cast_rows(wg_ref, wg_scr, 512)
        _cast_rows(wu_ref, wu_scr, 512)

    h = h_ref[...]
    gt = jnp.dot(h, wg_scr[...], preferred_element_type=F32)
    up = jnp.dot(h, wu_scr[...], preferred_element_type=F32)
    o_ref[...] = (gt * jax.nn.sigmoid(gt) * up).astype(o_ref.dtype)


def ffn_up(h, w_gate, w_up, layer, *, m_tiles, tm, tn):
    d = h.shape[1]
    f = w_gate.shape[2]
    w_spec = pl.BlockSpec((None, d, tn), lambda j, i: (layer, 0, j))
    return pl.pallas_call(
        _ffn_up_kernel,
        grid=(f // tn, m_tiles),
        in_specs=[pl.BlockSpec((tm, d), lambda j, i: (i, 0)), w_spec, w_spec],
        out_specs=pl.BlockSpec((tm, tn), lambda j, i: (i, j)),
        out_shape=jax.ShapeDtypeStruct((m_tiles * tm, f), BF16),
        scratch_shapes=[pltpu.VMEM((d, tn), BF16)] * 2,
        compiler_params=_cparams(("arbitrary", "arbitrary"), 56),
    )(h, w_gate, w_up)


def _ffn_down_kernel(a_ref, w_ref, x_ref, gate_ref, o_ref, w_scr):
    @pl.when(pl.program_id(1) == 0)
    def _():
        _cast_rows(w_ref, w_scr, 512)

    acc = jnp.dot(a_ref[...], w_scr[...], preferred_element_type=F32)
    o_ref[...] = x_ref[...] + gate_ref[...] * acc


def ffn_down(a, w_down, x, mod4, layer, *, m_tiles, tiles_per_seq, n_batch, tm, tn):
    f = a.shape[1]
    d = x.shape[1]
    gate_blocks = d // tn
    return pl.pallas_call(
        _ffn_down_kernel,
        grid=(d // tn, m_tiles),
        in_specs=[
            pl.BlockSpec((tm, f), lambda j, i: (i, 0)),
            pl.BlockSpec((None, f, tn), lambda j, i: (layer, 0, j)),
            pl.BlockSpec((tm, tn), lambda j, i: (i, j)),
            pl.BlockSpec((None, None, 1, tn),
                         lambda j, i: (layer, jnp.minimum(i // tiles_per_seq, n_batch), 0,
                                       5 * gate_blocks + j)),
        ],
        out_specs=pl.BlockSpec((tm, tn), lambda j, i: (i, j)),
        out_shape=jax.ShapeDtypeStruct((m_tiles * tm, d), F32),
        scratch_shapes=[pltpu.VMEM((f, tn), BF16)],
        compiler_params=_cparams(("arbitrary", "arbitrary"), 56),
    )(a, w_down, x, mod4)


HALO = 16


def _conv_kernel(h_ref, bg_ref, cg_ref, hp_ref, cp_ref, hn_ref, cn_ref, w_ref, gn_ref, o_ref, z_scr, *,
                 tiles_per_seq, lat_tiles):
    i = pl.program_id(0)
    t = h_ref.shape[0]
    whole_seq = i >= lat_tiles
    first = jnp.logical_or(i % tiles_per_seq == 0, whole_seq)
    last = jnp.logical_or(i % tiles_per_seq == tiles_per_seq - 1, whole_seq)
    z_prev = cp_ref[HALO - 1:HALO, :].astype(F32) * hp_ref[HALO - 1:HALO, :].astype(F32)
    z_next = cn_ref[0:1, :].astype(F32) * hn_ref[0:1, :].astype(F32)
    z_scr[7:8, :] = jnp.where(first, 0.0, z_prev)
    z_scr[8:8 + t, :] = cg_ref[...].astype(F32) * h_ref[...].astype(F32)
    z_scr[8 + t:9 + t, :] = jnp.where(last, 0.0, z_next)
    y = bg_ref[...].astype(F32) * (w_ref[0:1, :] * z_scr[7:7 + t, :] + w_ref[1:2, :] * z_scr[8:8 + t, :]
                                   + w_ref[2:3, :] * z_scr[9:9 + t, :])
    o_ref[...] = _rms(y, gn_ref[...]).astype(o_ref.dtype)


def conv_mixer(proj, conv_w, gn, *, rows, tile, tiles_per_seq, lat_tiles):
    ch = D_CONV
    n_tiles = rows // tile
    hb = tile // HALO
    last_blk = proj.shape[0] // HALO - 1
    main = lambda c: pl.BlockSpec((tile, ch), lambda i: (i, c))
    prev = lambda c: pl.BlockSpec((HALO, ch), lambda i: (jnp.maximum(i * hb - 1, 0), c))
    nxt = lambda c: pl.BlockSpec((HALO, ch), lambda i: (jnp.minimum((i + 1) * hb, last_blk), c))
    return pl.pallas_call(
        functools.partial(_conv_kernel, tiles_per_seq=tiles_per_seq, lat_tiles=lat_tiles),
        grid=(n_tiles,),
        in_specs=[main(0), main(1), main(2), prev(0), prev(2), nxt(0), nxt(2),
                  pl.BlockSpec((CONV_W, ch), lambda i: (0, 0)), pl.BlockSpec((1, ch), lambda i: (0, 0))],
        out_specs=pl.BlockSpec((tile, ch), lambda i: (i, 0)),
        out_shape=jax.ShapeDtypeStruct((rows, ch), BF16),
        scratch_shapes=[pltpu.VMEM((tile + 16, ch), F32)],
        compiler_params=_cparams(("arbitrary",), 32),
    )(proj, proj, proj, proj, proj, proj, proj, conv_w, gn.reshape(1, ch))


def _qkv_proj_kernel(cq_ref, ckv_ref, kr_ref, gq_ref, gkv_ref, wq_ref, wkv_ref, cos_ref, sin_ref,
                     q_ref, k_ref, v_ref):
    cos = cos_ref[...]
    sin = sin_ref[...]
    a = _rms(cq_ref[...].astype(F32), gq_ref[...]).astype(BF16)
    q = jnp.dot(a, wq_ref[...], preferred_element_type=F32)
    scale = MLA_SCALE * LOG2E
    for h in range(MLA_HEADS):
        c0 = h * QK_PAD
        q_ref[:, c0:c0 + QK_NOPE] = (q[:, c0:c0 + QK_NOPE] * scale).astype(q_ref.dtype)
        blk = q[:, c0 + QK_NOPE:c0 + QK_PAD]
        rot = blk * cos + pltpu.roll(blk, QK_ROPE, axis=1) * sin
        q_ref[:, c0 + QK_NOPE:c0 + QK_PAD] = (rot * scale).astype(q_ref.dtype)

    a = _rms(ckv_ref[...].astype(F32), gkv_ref[...]).astype(BF16)
    kv = jnp.dot(a, wkv_ref[...], preferred_element_type=F32)
    blk = kr_ref[...].astype(F32)
    rot = (blk * cos + pltpu.roll(blk, QK_ROPE, axis=1) * sin).astype(k_ref.dtype)
    nk = MLA_HEADS * QK_NOPE
    for h in range(MLA_HEADS):
        c0 = h * QK_PAD
        k_ref[:, c0:c0 + QK_NOPE] = kv[:, h * QK_NOPE:(h + 1) * QK_NOPE].astype(k_ref.dtype)
        k_ref[:, c0 + QK_NOPE:c0 + QK_PAD] = rot
    ones = jnp.ones((v_ref.shape[0], V_PAD - V_HEAD), v_ref.dtype)
    for h in range(MLA_HEADS):
        c0 = h * V_PAD
        v_ref[:, c0:c0 + V_HEAD] = kv[:, nk + h * V_HEAD:nk + (h + 1) * V_HEAD].astype(v_ref.dtype)
        v_ref[:, c0 + V_HEAD:c0 + V_PAD] = ones


def qkv_projection(proj, q_norm, kv_norm, w_q, w_kv, cos_t, sin_t, *, rows, tm, lat_tiles, tiles_per_seq):
    tab_map = lambda i: (jnp.where(i < lat_tiles, i % tiles_per_seq, tiles_per_seq), 0)
    nk = MLA_HEADS * QK_PAD
    nv = MLA_HEADS * V_PAD
    return pl.pallas_call(
        _qkv_proj_kernel,
        grid=(rows // tm,),
        in_specs=[
            pl.BlockSpec((tm, Q_RANK), lambda i: (i, COL_CQ // Q_RANK)),
            pl.BlockSpec((tm, KV_RANK), lambda i: (i, COL_CKV // KV_RANK)),
            pl.BlockSpec((tm, LANE), lambda i: (i, COL_KR // LANE)),
            pl.BlockSpec((1, Q_RANK), lambda i: (0, 0)),
            pl.BlockSpec((1, KV_RANK), lambda i: (0, 0)),
            pl.BlockSpec((Q_RANK, nk), lambda i: (0, 0)),
            pl.BlockSpec((KV_RANK, MLA_HEADS * (QK_NOPE + V_HEAD)), lambda i: (0, 0)),
            pl.BlockSpec((tm, LANE), tab_map),
            pl.BlockSpec((tm, LANE), tab_map),
        ],
        out_specs=[pl.BlockSpec((tm, nk), lambda i: (i, 0)), pl.BlockSpec((tm, nk), lambda i: (i, 0)),
                   pl.BlockSpec((tm, nv), lambda i: (i, 0))],
        out_shape=[jax.ShapeDtypeStruct((rows, nk), BF16), jax.ShapeDtypeStruct((rows, nk), BF16),
                   jax.ShapeDtypeStruct((rows, nv), BF16)],
        compiler_params=_cparams(("arbitrary",), 40),
    )(proj, proj, proj, q_norm.reshape(1, Q_RANK), kv_norm.reshape(1, KV_RANK), w_q, w_kv, cos_t, sin_t)


_NT = (((1,), (1,)), ((), ()))


HEAD_SPLIT = 2


def _attn_kernel(q_ref, kl_ref, kc_ref, vl_ref, vc_ref, o_ref, *, lat_steps, steps_per_batch, ctx_len):
    s = pl.program_id(0)
    hpg = MLA_HEADS // HEAD_SPLIT

    def heads(q_rows, c_rows, with_latent):
        for h in range(hpg):
            q = q_ref[q_rows, h * QK_PAD:(h + 1) * QK_PAD]
            sc = lax.dot_general(q, kc_ref[c_rows, h * QK_PAD:(h + 1) * QK_PAD], _NT,
                                 preferred_element_type=F32)
            m = jnp.max(sc, axis=-1, keepdims=True)
            if with_latent:
                sl = lax.dot_general(q, kl_ref[:, h * QK_PAD:(h + 1) * QK_PAD], _NT,
                                     preferred_element_type=F32)
                m = jnp.maximum(m, jnp.max(sl, axis=-1, keepdims=True))
            o = jnp.dot(jnp.exp2(sc - m).astype(BF16), vc_ref[c_rows, h * V_PAD:(h + 1) * V_PAD],
                        preferred_element_type=F32)
            if with_latent:
                o = o + jnp.dot(jnp.exp2(sl - m).astype(BF16), vl_ref[:, h * V_PAD:(h + 1) * V_PAD],
                                preferred_element_type=F32)
            o_ref[q_rows, h * V_HEAD:(h + 1) * V_HEAD] = (o[:, :V_HEAD] / o[:, V_HEAD:]).astype(o_ref.dtype)

    @pl.when(s < lat_steps)
    def _():
        b = s // steps_per_batch
        c0 = pl.multiple_of((b % 2) * ctx_len, ctx_len)
        heads(slice(None), pl.ds(c0, ctx_len), True)

    @pl.when(s >= lat_steps)
    def _():
        for hb in range(2):
            rows = slice(hb * ctx_len, (hb + 1) * ctx_len)
            heads(rows, rows, False)


def latent_attention(q, k, v, *, n_batch, seq, ctx_len, tq, ctx_queries, rows_out):
    assert tq == 2 * ctx_len and n_batch % 2 == 0 and seq % tq == 0
    lat_tiles = seq // tq
    spb = HEAD_SPLIT * lat_tiles
    lat_steps = n_batch * spb
    ctx_steps = (n_batch // 2) * HEAD_SPLIT if ctx_queries else 0
    ctx_blk0 = n_batch * seq // tq
    nq = MLA_HEADS * QK_PAD // HEAD_SPLIT
    nv = MLA_HEADS * V_HEAD // HEAD_SPLIT
    nvp = MLA_HEADS * V_PAD // HEAD_SPLIT

    def split(s):
        lat = s < lat_steps
        cs = s - lat_steps
        b = jnp.where(lat, s // spb, n_batch - 1)
        hh = jnp.where(lat, (s // lat_tiles) % HEAD_SPLIT, cs % HEAD_SPLIT)
        q_blk = jnp.where(lat, b * lat_tiles + s % lat_tiles, ctx_blk0 + cs // HEAD_SPLIT)
        c_blk = jnp.where(lat, ctx_blk0 + b // 2, ctx_blk0 + cs // HEAD_SPLIT)
        return b, hh, q_blk, c_blk

    def lat_map(s):
        b, hh, _, _ = split(s)
        return (b, jnp.where(s < lat_steps, hh, HEAD_SPLIT - 1))

    q_map = lambda s: (split(s)[2], split(s)[1])
    c_map = lambda s: (split(s)[3], split(s)[1])
    return pl.pallas_call(
        functools.partial(_attn_kernel, lat_steps=lat_steps, steps_per_batch=spb, ctx_len=ctx_len),
        grid=(lat_steps + ctx_steps,),
        in_specs=[
            pl.BlockSpec((tq, nq), q_map),
            pl.BlockSpec((seq, nq), lat_map),
            pl.BlockSpec((tq, nq), c_map),
            pl.BlockSpec((seq, nvp), lat_map),
            pl.BlockSpec((tq, nvp), c_map),
        ],
        out_specs=pl.BlockSpec((tq, nv), q_map),
        out_shape=jax.ShapeDtypeStruct((rows_out, MLA_HEADS * V_HEAD), BF16),
        compiler_params=_cparams(("arbitrary",), 48),
    )(q, k, k, v, v)


def _s5_disc_kernel(are_ref, aim_ref, ldt_ref, abr_ref, abi_ref, cfr_ref, cfi_ref):
    ar = are_ref[...]
    ai = aim_ref[...]
    dt = jnp.exp(ldt_ref[...])
    mag = jnp.exp(ar * dt)
    th = ai * dt
    br = mag * jnp.cos(th)
    bi = mag * jnp.sin(th)
    nr = br - 1.0
    den = ar * ar + ai * ai
    abr_ref[...] = br
    abi_ref[...] = bi
    cfr_ref[...] = (nr * ar + bi * ai) / den
    cfi_ref[...] = (bi * ar - nr * ai) / den


def s5_discretise(a_re, a_im, log_dt):
    shp = a_re.shape
    rows = shp[0] * shp[1] * shp[2]
    flat = lambda t: t.reshape(rows, shp[3])
    ldt = jnp.broadcast_to(log_dt[..., None], shp)
    outs = pl.pallas_call(
        _s5_disc_kernel,
        out_shape=[jax.ShapeDtypeStruct((rows, shp[3]), F32)] * 4,
    )(flat(a_re), flat(a_im), flat(ldt))
    return [o.reshape(shp) for o in outs]


def _cmul(xr, xi, yr, yi):
    return xr * yr - xi * yi, xr * yi + xi * yr


def _s5_mats_kernel(abc_re, abc_im, cex_re, cex_im, afb_re, afb_im, cfb_re, cfb_im, bfb_re, bfb_im,
                    cpp_re, cpp_im, bx_ref, by_ref, ma_ref, mi_ref, mo_ref, a16_ref):
    tw = CHUNK * SSM_GROUP
    p = SSM_STATE
    lane = lax.broadcasted_iota(jnp.int32, (p, tw), 1)
    tblk = lane // SSM_GROUP
    lane16 = lax.broadcasted_iota(jnp.int32, (SSM_GROUP, tw), 1)
    lane128 = lax.broadcasted_iota(jnp.int32, (1, 2 * p), 1)
    fwd_lanes = lane128 < p
    sgn = jnp.where(lax.broadcasted_iota(jnp.int32, (SSM_GROUP, 2 * p), 1) < p, 1.0, -1.0)
    zeros = jnp.zeros((p, tw), F32)
    mi_blocks = []
    for d in range(2):
        ar = jnp.broadcast_to(abc_re[d], (p, tw))
        ai = jnp.broadcast_to(abc_im[d], (p, tw))
        sel_r = tblk if d == 0 else (CHUNK - 1) - tblk
        sq_re, sq_im = ar, ai
        r_re, r_im = jnp.ones((p, tw), F32), zeros
        for b in range((CHUNK - 1).bit_length()):
            bit = ((sel_r >> b) & 1) == 1
            n_re, n_im = _cmul(r_re, r_im, sq_re, sq_im)
            r_re = jnp.where(bit, n_re, r_re)
            r_im = jnp.where(bit, n_im, r_im)
            sq_re, sq_im = _cmul(sq_re, sq_im, sq_re, sq_im)
        e_re, e_im = _cmul(r_re, r_im, ar, ai)
        cr = cex_re[d]
        ci = cex_im[d]
        w_re, w_im = _cmul(cr, ci, e_re, e_im)
        base = d * 4 * p
        mdt = mo_ref.dtype
        zeros_m = zeros.astype(mdt)
        if d == 0:
            mo_ref[base:base + p, :] = w_re.astype(mdt)
            mo_ref[base + p:base + 2 * p, :] = zeros_m
            mo_ref[base + 2 * p:base + 3 * p, :] = (-w_im).astype(mdt)
            mo_ref[base + 3 * p:base + 4 * p, :] = zeros_m
        else:
            mo_ref[base:base + p, :] = zeros_m
            mo_ref[base + p:base + 2 * p, :] = w_re.astype(mdt)
            mo_ref[base + 2 * p:base + 3 * p, :] = zeros_m
            mo_ref[base + 3 * p:base + 4 * p, :] = (-w_im).astype(mdt)
        rr_re, rr_im = _cmul(cr, ci, r_re, r_im)
        stacked = jnp.concatenate([rr_re, rr_im], axis=0)
        lm = sgn * cpp_re[d] * bx_ref[d] - cpp_im[d] * by_ref[d]
        kall = jnp.dot(lm, stacked, preferred_element_type=F32, precision=lax.Precision.HIGHEST)
        for s in range(CHUNK):
            if d == 0:
                shift = SSM_GROUP * s
                keep = lane16 >= SSM_GROUP * s
            else:
                shift = (SSM_GROUP * (s + 1)) % tw
                keep = lane16 < SSM_GROUP * (s + 1)
            rolled = pltpu.roll(kall, shift, axis=1) if shift else kall
            blk = jnp.where(keep, rolled, 0.0)
            if d == 0:
                mi_blocks.append(blk)
            else:
                mi_ref[SSM_GROUP * s:SSM_GROUP * (s + 1), :] = (mi_blocks[s] + blk).astype(mi_ref.dtype)

    a_re = afb_re[...]
    a_im = afb_im[...]
    cf_re = cfb_re[...]
    cf_im = cfb_im[...]
    b_re = bfb_re[...]
    b_im = bfb_im[...]
    powers = []
    qr, qi = jnp.ones((1, 2 * p), F32), jnp.zeros((1, 2 * p), F32)
    for k in range(CHUNK + 1):
        powers.append((qr, qi))
        if k < CHUNK:
            qr, qi = _cmul(qr, qi, a_re, a_im)
    for s in range(CHUNK):
        g_re = jnp.where(fwd_lanes, powers[CHUNK - 1 - s][0], powers[s][0])
        g_im = jnp.where(fwd_lanes, powers[CHUNK - 1 - s][1], powers[s][1])
        g_re, g_im = _cmul(g_re, g_im, cf_re, cf_im)
        rows = slice(SSM_GROUP * s, SSM_GROUP * (s + 1))
        ma_ref[rows, 0:2 * p] = (g_re * b_re - g_im * b_im).astype(ma_ref.dtype)
        ma_ref[rows, 2 * p:4 * p] = (g_re * b_im + g_im * b_re).astype(ma_ref.dtype)
    a16_ref[0:1, :] = powers[CHUNK][0]
    a16_ref[1:2, :] = powers[CHUNK][1]


def s5_matrices(abar_re, abar_im, coef_re, coef_im, b_re, b_im, c_re, c_im):
    depth, _, g, p = abar_re.shape
    hg = b_re.shape[-1]
    tw = CHUNK * hg
    t0213 = lambda t: t.transpose(0, 2, 1, 3)
    abc = [t0213(t)[..., None] for t in (abar_re, abar_im)]
    cex = [jnp.tile(t.transpose(0, 2, 1, 4, 3), (1, 1, 1, 1, CHUNK)) for t in (c_re, c_im)]
    fb = lambda t: t0213(t).reshape(depth, g, 1, 2 * p)
    afb = [fb(t) for t in (abar_re, abar_im)]
    cfb = [fb(t) for t in (coef_re, coef_im)]
    bfb = [t.transpose(0, 2, 4, 1, 3).reshape(depth, g, hg, 2 * p) for t in (b_re, b_im)]
    cpp = [t0213(jnp.concatenate([t, t], axis=-1))[:, :, :, None, :] for t in (coef_re, coef_im)]
    bt_re = b_re.transpose(0, 2, 1, 4, 3)
    bt_im = b_im.transpose(0, 2, 1, 4, 3)
    bx = jnp.concatenate([bt_re, bt_im], axis=-1)
    by = jnp.concatenate([bt_im, bt_re], axis=-1)
    ins = [*abc, *cex, *afb, *cfb, *bfb, *cpp, bx, by]

    def spec(t):
        blk = (None, None) + t.shape[2:]
        nz = len(t.shape) - 2
        return pl.BlockSpec(blk, lambda l, gi: (l, gi) + (0,) * nz)

    def ospec(r, c):
        return pl.BlockSpec((None, None, r, c), lambda l, gi: (l, gi, 0, 0))

    return pl.pallas_call(
        _s5_mats_kernel,
        grid=(depth, g),
        in_specs=[spec(t) for t in ins],
        out_specs=[ospec(tw, 4 * p), ospec(tw, tw), ospec(8 * p, tw), ospec(2, 2 * p)],
        out_shape=[
            jax.ShapeDtypeStruct((depth, g, tw, 4 * p), BF16),
            jax.ShapeDtypeStruct((depth, g, tw, tw), BF16),
            jax.ShapeDtypeStruct((depth, g, 8 * p, tw), BF16),
            jax.ShapeDtypeStruct((depth, g, 2, 2 * p), F32),
        ],
        compiler_params=_cparams(("arbitrary", "arbitrary"), 32),
    )(*ins)


GROUP_BLOCK = LANE // SSM_GROUP


def lane_swap_matrix():
    idx = jnp.arange(GROUP_BLOCK * LANE)
    a, b, c = idx // LANE, (idx // SSM_GROUP) % GROUP_BLOCK, idx % SSM_GROUP
    dst = b * LANE + a * SSM_GROUP + c
    return (dst[:, None] == idx[None, :]).astype(BF16)


def _s5_main_kernel(*refs, order_f, order_b, nb, lat_chunks, ctx_chunks):
    (u_ref, p_ref, ma_ref, mi_ref, mo_ref, a_ref, y_ref, ug_scr, yg_scr,
     xr_scr, xi_scr, fr_scr, fi_scr, br_scr, bi_scr) = refs
    p2 = 2 * SSM_STATE
    half = CHUNK // 2
    rows = u_ref.shape[0] // CHUNK
    n_sub = xr_scr.shape[0]

    for th in range(2):
        slab = jnp.concatenate(
            [u_ref[pl.ds(half * th + tl, rows, stride=CHUNK), :].astype(BF16) for tl in range(half)], axis=1)
        perm = jnp.dot(slab, p_ref[...], preferred_element_type=F32).astype(BF16)
        for g in range(GROUP_BLOCK):
            ug_scr[g, :, th * LANE:(th + 1) * LANE] = perm[:, g * LANE:(g + 1) * LANE]

    segments = ((0, lat_chunks), (nb * lat_chunks, ctx_chunks))

    def to_chunk_major(dst_ref, k, val):
        for r0, n in segments:
            for b in range(nb):
                dst_ref[k, pl.ds(r0 + b, n, stride=nb), :] = val[r0 + b * n:r0 + (b + 1) * n, :]

    def to_batch_major(src_ref, k):
        return jnp.concatenate([src_ref[k, pl.ds(r0 + b, n, stride=nb), :]
                                for r0, n in segments for b in range(nb)], axis=0)

    def rows_of(j):
        return slice(nb * j, nb * j + nb)

    fwd_lanes = lax.broadcasted_iota(jnp.int32, (nb, p2), 1) < SSM_STATE
    for g0 in range(0, GROUP_BLOCK, n_sub):
        for k in range(n_sub):
            x = jnp.dot(ug_scr[g0 + k], ma_ref[g0 + k], preferred_element_type=F32)
            to_chunk_major(xr_scr, k, x[:, 0:p2])
            to_chunk_major(xi_scr, k, x[:, p2:2 * p2])

        a_re = [a_ref[g0 + k, 0:1, :] for k in range(n_sub)]
        a_im = [a_ref[g0 + k, 1:2, :] for k in range(n_sub)]
        s_re = [jnp.zeros((nb, p2), F32) for _ in range(n_sub)]
        s_im = [jnp.zeros((nb, p2), F32) for _ in range(n_sub)]
        for jf, jb in zip(order_f, order_b):
            rf = rows_of(jf)
            rb = rows_of(jb)
            for k in range(n_sub):
                fr_scr[k, rf, :] = s_re[k]
                fi_scr[k, rf, :] = s_im[k]
                br_scr[k, rb, :] = s_re[k]
                bi_scr[k, rb, :] = s_im[k]
                xr = jnp.where(fwd_lanes, xr_scr[k, rf, :], xr_scr[k, rb, :])
                xi = jnp.where(fwd_lanes, xi_scr[k, rf, :], xi_scr[k, rb, :])
                s_re[k], s_im[k] = (a_re[k] * s_re[k] - a_im[k] * s_im[k] + xr,
                                    a_re[k] * s_im[k] + a_im[k] * s_re[k] + xi)

        for k in range(n_sub):
            g = g0 + k
            y = jnp.dot(ug_scr[g], mi_ref[g], preferred_element_type=F32)
            sp_f = jnp.concatenate([to_batch_major(fr_scr, k), to_batch_major(fi_scr, k)], axis=1).astype(BF16)
            sp_b = jnp.concatenate([to_batch_major(br_scr, k), to_batch_major(bi_scr, k)], axis=1).astype(BF16)
            y = y + jnp.dot(sp_f, mo_ref[g, 0:2 * p2, :], preferred_element_type=F32)
            y = y + jnp.dot(sp_b, mo_ref[g, 2 * p2:4 * p2, :], preferred_element_type=F32)
            yg_scr[g] = y.astype(BF16)

    for th in range(2):
        slab = jnp.concatenate([yg_scr[g, :, th * LANE:(th + 1) * LANE] for g in range(GROUP_BLOCK)], axis=1)
        perm = jnp.dot(slab, p_ref[...], preferred_element_type=F32)
        for tl in range(half):
            y_ref[pl.ds(half * th + tl, rows, stride=CHUNK), :] = perm[:, tl * LANE:(tl + 1) * LANE]


def s5_chunked(u, ma, mi, mo, a16, layer, *, lat_chunks, ctx_chunks, nb):
    r_all = u.shape[0]
    rows = r_all // CHUNK
    tw = CHUNK * SSM_GROUP
    j_tot = lat_chunks + ctx_chunks
    order_f = tuple(range(lat_chunks, j_tot)) + tuple(range(lat_chunks))
    order_b = tuple(range(j_tot - 1, lat_chunks - 1, -1)) + tuple(range(lat_chunks - 1, -1, -1))
    p4 = 4 * SSM_STATE
    n_gb = SSM_GROUPS // GROUP_BLOCK
    n_sub = GROUP_BLOCK // 2

    def wspec(r, c):
        return pl.BlockSpec((None, GROUP_BLOCK, r, c), lambda gb: (layer, gb, 0, 0))

    return pl.pallas_call(
        functools.partial(_s5_main_kernel, order_f=order_f, order_b=order_b, nb=nb,
                          lat_chunks=lat_chunks, ctx_chunks=ctx_chunks),
        grid=(n_gb,),
        in_specs=[
            pl.BlockSpec((r_all, LANE), lambda gb: (0, gb)),
            pl.BlockSpec((GROUP_BLOCK * LANE, GROUP_BLOCK * LANE), lambda gb: (0, 0)),
            wspec(tw, p4), wspec(tw, tw), wspec(2 * p4, tw), wspec(2, 2 * SSM_STATE),
        ],
        out_specs=pl.BlockSpec((r_all, LANE), lambda gb: (0, gb)),
        out_shape=jax.ShapeDtypeStruct((r_all, D_SSM), F32),
        scratch_shapes=[
            pltpu.VMEM((GROUP_BLOCK, rows, tw), BF16),
            pltpu.VMEM((GROUP_BLOCK, rows, tw), BF16),
        ] + [pltpu.VMEM((n_sub, rows, 2 * SSM_STATE), F32)] * 6,
        compiler_params=_cparams(("arbitrary",), 56),
    )(u, lane_swap_matrix(), ma, mi, mo, a16)


def _s5_out_kernel(y_ref, u_ref, d_ref, w_ref, b_ref, gn_ref, o_ref):
    y = y_ref[...].astype(F32) + d_ref[...] * u_ref[...].astype(F32)
    g = jax.nn.gelu(y)
    z = jnp.dot(g.astype(BF16), w_ref[...], preferred_element_type=F32) + b_ref[...]
    o_ref[...] = _rms(g * jax.nn.sigmoid(z), gn_ref[...]).astype(o_ref.dtype)


def s5_output(y, u, ssm_d, w_glu, b_glu, gn, *, rows, tm):
    ch = D_SSM
    vec = lambda: pl.BlockSpec((1, ch), lambda i: (0, 0))
    return pl.pallas_call(
        _s5_out_kernel,
        grid=(rows // tm,),
        in_specs=[
            pl.BlockSpec((tm, ch), lambda i: (i, 0)),
            pl.BlockSpec((tm, ch), lambda i: (i, 0)),
            vec(),
            pl.BlockSpec((ch, ch), lambda i: (0, 0)),
            vec(), vec(),
        ],
        out_specs=pl.BlockSpec((tm, ch), lambda i: (i, 0)),
        out_shape=jax.ShapeDtypeStruct((rows, ch), BF16),
        compiler_params=_cparams(("arbitrary",), 32),
    )(y, u, ssm_d.reshape(1, ch), w_glu, b_glu.reshape(1, ch), gn.reshape(1, ch))


def _final_norm_kernel(x_ref, g_ref, o_ref):
    o_ref[...] = _rms(x_ref[...], g_ref[...])


def final_rms_norm(x, g, *, rows, tm):
    d = x.shape[1]
    return pl.pallas_call(
        _final_norm_kernel,
        grid=(rows // tm,),
        in_specs=[pl.BlockSpec((tm, d), lambda i: (i, 0)), pl.BlockSpec((1, d), lambda i: (0, 0))],
        out_specs=pl.BlockSpec((tm, d), lambda i: (i, 0)),
        out_shape=jax.ShapeDtypeStruct((rows, d), F32),
        compiler_params=_cparams(("arbitrary",), 32),
    )(x, g.reshape(1, d))


def _rope_partner_perm():
    idx = []
    for i in range(QK_ROPE):
        idx.append(i + 16 if (i % 32) < 16 else i - 16)
    return jnp.asarray(idx, jnp.int32)


def rope_tables(seq, tab_tile):
    rows = seq // GRID_W
    row = jnp.repeat(jnp.arange(rows, dtype=F32), GRID_W)
    col = jnp.tile(jnp.arange(GRID_W, dtype=F32), rows)
    n_freq = QK_ROPE // 4
    inv = ROPE_BASE ** (-jnp.arange(n_freq, dtype=F32) / n_freq)
    ar = row[:, None] * inv
    ac = col[:, None] * inv
    cos = jnp.concatenate([jnp.cos(ar), jnp.cos(ar), jnp.cos(ac), jnp.cos(ac)], axis=1)
    sin = jnp.concatenate([-jnp.sin(ar), jnp.sin(ar), -jnp.sin(ac), jnp.sin(ac)], axis=1)
    cos = jnp.concatenate([cos, jnp.ones((tab_tile, QK_ROPE), F32)], axis=0)
    sin = jnp.concatenate([sin, jnp.zeros((tab_tile, QK_ROPE), F32)], axis=0)
    pad = jnp.zeros((seq + tab_tile, LANE - QK_ROPE), F32)
    return jnp.concatenate([cos, pad], axis=1), jnp.concatenate([sin, pad], axis=1)


def prep_w_in(w_in):
    assert COL_U == COL_CQ + Q_RANK
    s = [COL_U, COL_U + KV_RANK, COL_U + KV_RANK + QK_ROPE, COL_U + KV_RANK + QK_ROPE + D_SSM]
    ckv, kr, u = [w_in[..., s[i]:s[i + 1]] for i in range(3)]
    krp = kr[..., _rope_partner_perm()]
    pad = jnp.zeros(w_in.shape[:-1] + (N_PROJ - (COL_KR + LANE),), w_in.dtype)
    tail = jnp.concatenate([u, ckv, kr, krp, pad], axis=-1).astype(BF16)
    return w_in.astype(BF16), tail


def prep_w_uq(w_uq):
    r = w_uq.shape[0]
    w = w_uq.reshape(r, MLA_HEADS, QK_NOPE + QK_ROPE)
    rope = w[:, :, QK_NOPE:]
    w = jnp.concatenate([w, rope[:, :, _rope_partner_perm()]], axis=-1)
    return w.reshape(r, MLA_HEADS * QK_PAD).astype(BF16)


def prep_w_ukv(w_ukv):
    r = w_ukv.shape[0]
    w = w_ukv.reshape(r, MLA_HEADS, QK_NOPE + V_HEAD)
    kn = w[:, :, :QK_NOPE].reshape(r, MLA_HEADS * QK_NOPE)
    v = w[:, :, QK_NOPE:].reshape(r, MLA_HEADS * V_HEAD)
    return jnp.concatenate([kn, v], axis=1).astype(BF16)


def kernel(x, c, ctx, c_ctx, w_ada, b_ada, norm1_g, norm2_g, w_in, conv_w, mla_q_norm, w_uq, mla_kv_norm, w_ukv, ssm_a_re, ssm_a_im, ssm_log_dt, ssm_b_re, ssm_b_im, ssm_c_re, ssm_c_im, ssm_d, w_glu, b_glu, mix_norm, w_o, w_gate, w_up, w_down, final_norm):
    nb, seq, d = x.shape
    ctx_len = ctx.shape[1]
    depth = w_ada.shape[0]
    r_lat = nb * seq
    r_ctx = nb * ctx_len
    r_all = r_lat + r_ctx
    assert seq % TM == 0 and r_ctx == TM and seq % CHUNK == 0 and ctx_len % CHUNK == 0
    tiles_per_seq = seq // TM
    lat_tiles_m = r_lat // TM
    d_ff = w_gate.shape[2]

    xs = jnp.concatenate([x.reshape(r_lat, d), ctx.reshape(r_ctx, d)], axis=0)

    cvec = jnp.concatenate([c, c_ctx[None, :], jnp.zeros((8 - nb - 1, d), F32)], axis=0)
    mod = ada_modulation(cvec, w_ada, b_ada)
    mod4 = mod.reshape(depth, 8, 1, 6 * d)

    tp = 512
    cos_t, sin_t = rope_tables(seq, tp)
    abar_re, abar_im, coef_re, coef_im = s5_discretise(ssm_a_re, ssm_a_im, ssm_log_dt)
    ma, mi, mo, a16 = s5_matrices(abar_re, abar_im, coef_re, coef_im,
                                  ssm_b_re, ssm_b_im, ssm_c_re, ssm_c_im)
    w_in_main, w_in_tail = prep_w_in(w_in)
    w_o_b = w_o.astype(BF16)
    lat_chunks = seq // CHUNK
    ctx_chunks = ctx_len // CHUNK
    tw = CHUNK * SSM_GROUP

    for i in range(depth):
        ctx_out = i < depth - 1
        m_tiles = lat_tiles_m + (1 if ctx_out else 0)
        rows_out = r_all if ctx_out else r_lat
        tf = 512

        proj, u32 = input_projection(xs, norm1_g[i], mod4, i, w_in_main, w_in_tail,
                                     tiles_per_seq=tiles_per_seq, n_batch=nb)

        y_conv = conv_mixer(proj, conv_w[i], mix_norm[i, :D_CONV], rows=rows_out, tile=ctx_len,
                            tiles_per_seq=seq // ctx_len, lat_tiles=r_lat // ctx_len)

        q, k, v = qkv_projection(proj, mla_q_norm[i], mla_kv_norm[i], prep_w_uq(w_uq[i]),
                                 prep_w_ukv(w_ukv[i]), cos_t, sin_t, rows=r_all, tm=tp,
                                 lat_tiles=r_lat // tp, tiles_per_seq=seq // tp)
        y_att = latent_attention(q, k, v, n_batch=nb, seq=seq, ctx_len=ctx_len, tq=2 * ctx_len,
                                 ctx_queries=ctx_out, rows_out=rows_out)

        y_tok = s5_chunked(u32, ma, mi, mo, a16, i, lat_chunks=lat_chunks, ctx_chunks=ctx_chunks, nb=nb)
        y_ssm = s5_output(y_tok, u32, ssm_d[i], w_glu[i].astype(BF16), b_glu[i],
                          mix_norm[i, D_CONV + D_ATTN:], rows=rows_out, tm=tp)

        x2, h2 = merge_projection(y_conv, y_att, y_ssm, mix_norm[i, D_CONV:D_CONV + D_ATTN], w_o_b, xs,
                                  norm2_g[i], mod4, i,
                                  m_tiles=rows_out // tf, tiles_per_seq=seq // tf, n_batch=nb, tm=tf)
        hidden = ffn_up(h2, w_gate, w_up, i, m_tiles=m_tiles, tm=TM, tn=512)
        xs = ffn_down(hidden, w_down, x2, mod4, i, m_tiles=rows_out // tf, tiles_per_seq=seq // tf,
                      n_batch=nb, tm=tf, tn=512)

    out = final_rms_norm(xs, final_norm, rows=r_lat, tm=tp)
    return out.reshape(nb, seq, d)
```

```python
import functools
import math

import jax
import jax.numpy as jnp
from jax import lax
from jax.experimental import pallas as pl
from jax.experimental.pallas import tpu as pltpu

F32 = jnp.float32
BF16 = jnp.bfloat16

EPS = 1e-6
GRID_W = 64
CONV_W = 3
D_CONV = 512
D_SSM = 512
D_ATTN = 1024
MLA_HEADS = 8
QK_NOPE = 128
QK_ROPE = 64
V_HEAD = 128
Q_RANK = 512
KV_RANK = 256
ROPE_BASE = 10000.0
MLA_SCALE = (QK_NOPE + QK_ROPE) ** -0.5
SSM_GROUP = 16
SSM_GROUPS = 32
SSM_STATE = 64
CHUNK = 16

QK_PAD = 256
V_PAD = 256
LOG2E = math.log2(math.e)
LANE = 128
SUBLANE = 8
MIB = 1024 * 1024

TM = 1024
TILE_MERGE_ROWS = 512
TILE_NARROW_ROWS = 512
TILE_FFN_COLS = 512
CAST_ROWS = 512

VMEM_MIB = dict(ada=40, in_proj=52, merge=52, ffn_up=56, ffn_down=56, conv=32, qkv=40, attention=48,
                s5_mats=32, s5_main=56, s5_out=32, final_norm=32)

COL_CQ = 3 * D_CONV
COL_U = COL_CQ + Q_RANK
COL_CKV = COL_U + D_SSM
COL_KR = COL_CKV + KV_RANK
N_PROJ = 3072


def _cparams(sem, call):
    return pltpu.CompilerParams(dimension_semantics=sem, vmem_limit_bytes=VMEM_MIB[call] * MIB)


def _rms(x, g):
    return x * lax.rsqrt(jnp.mean(x * x, axis=-1, keepdims=True) + EPS) * g


def _ada_kernel(c_ref, w_ref, b_ref, o_ref):
    cv = c_ref[...]
    s = (cv * jax.nn.sigmoid(cv)).astype(BF16)
    o_ref[...] = jnp.dot(s, w_ref[...].astype(BF16), preferred_element_type=F32) + b_ref[...]


def ada_modulation(cvec, w_ada, b_ada, tn=1024):
    depth, d, n = w_ada.shape
    rows = cvec.shape[0]
    return pl.pallas_call(
        _ada_kernel,
        grid=(depth, n // tn),
        in_specs=[
            pl.BlockSpec((rows, d), lambda l, j: (0, 0)),
            pl.BlockSpec((None, d, tn), lambda l, j: (l, 0, j)),
            pl.BlockSpec((None, 1, tn), lambda l, j: (l, 0, j)),
        ],
        out_specs=pl.BlockSpec((None, rows, tn), lambda l, j: (l, 0, j)),
        out_shape=jax.ShapeDtypeStruct((depth, rows, n), F32),
        compiler_params=_cparams(("arbitrary", "arbitrary"), "ada"),
    )(cvec, w_ada, b_ada.reshape(depth, 1, n))


def _in_proj_kernel(x_ref, g_ref, sh_ref, sc_ref, wm_ref, wt_ref, o_ref, u_ref, a_scr, *, row_chunk,
                    main_tiles):
    tm = x_ref.shape[0]
    j = pl.program_id(1)

    @pl.when(j == 0)
    def _():
        g = g_ref[...]
        sh = sh_ref[...]
        sc1 = 1.0 + sc_ref[...]
        for r in range(0, tm, row_chunk):
            x = x_ref[r:r + row_chunk, :]
            a_scr[r:r + row_chunk, :] = (_rms(x, g) * sc1 + sh).astype(BF16)

    @pl.when(j < main_tiles)
    def _():
        o_ref[...] = jnp.dot(a_scr[...], wm_ref[...], preferred_element_type=F32).astype(o_ref.dtype)

    @pl.when(j == main_tiles)
    def _():
        acc = jnp.dot(a_scr[...], wt_ref[...], preferred_element_type=F32)
        o_ref[...] = acc.astype(o_ref.dtype)
        u_ref[...] = acc[:, :u_ref.shape[1]]


def input_projection(x, gain, mod4, layer, w_main, w_tail, *, tiles_per_seq, n_batch, tm=TM):
    m, d = x.shape
    tn = w_tail.shape[2]
    n = N_PROJ
    main_tiles = COL_U // tn
    assert COL_U % tn == 0 and n - COL_U == tn

    def mod_spec(col):
        return pl.BlockSpec((None, None, 1, d),
                            lambda i, j: (layer, jnp.minimum(i // tiles_per_seq, n_batch), 0, col))

    return pl.pallas_call(
        functools.partial(_in_proj_kernel, row_chunk=min(256, tm), main_tiles=main_tiles),
        grid=(m // tm, n // tn),
        in_specs=[
            pl.BlockSpec((tm, d), lambda i, j: (i, 0)),
            pl.BlockSpec((1, d), lambda i, j: (0, 0)),
            mod_spec(0),
            mod_spec(1),
            pl.BlockSpec((None, d, tn), lambda i, j: (layer, 0, jnp.minimum(j, main_tiles - 1))),
            pl.BlockSpec((None, d, tn), lambda i, j: (layer, 0, 0)),
        ],
        out_specs=[pl.BlockSpec((tm, tn), lambda i, j: (i, j)),
                   pl.BlockSpec((tm, D_SSM), lambda i, j: (i, 0))],
        out_shape=[jax.ShapeDtypeStruct((m, n), BF16), jax.ShapeDtypeStruct((m, D_SSM), F32)],
        scratch_shapes=[pltpu.VMEM((tm, d), BF16)],
        compiler_params=_cparams(("arbitrary", "arbitrary"), "in_proj"),
    )(x, gain.reshape(1, d), mod4, mod4, w_main, w_tail)


def _merge_kernel(yc_ref, ya_ref, ys_ref, gna_ref, w_ref, x_ref, gate_ref, g_ref, sh_ref, sc_ref,
                  xo_ref, ho_ref, *, row_chunk):
    tm = x_ref.shape[0]
    kc, ka = yc_ref.shape[1], ya_ref.shape[1]
    gna = gna_ref[...]
    gate = gate_ref[...]
    g = g_ref[...]
    sh = sh_ref[...]
    sc1 = 1.0 + sc_ref[...]
    for r in range(0, tm, row_chunk):
        rows = slice(r, r + row_chunk)
        acc = jnp.dot(yc_ref[rows, :], w_ref[0:kc, :], preferred_element_type=F32)
        ya = _rms(ya_ref[rows, :].astype(F32), gna).astype(BF16)
        acc = acc + jnp.dot(ya, w_ref[kc:kc + ka, :], preferred_element_type=F32)
        acc = acc + jnp.dot(ys_ref[rows, :], w_ref[kc + ka:, :], preferred_element_type=F32)
        x2 = x_ref[rows, :] + gate * acc
        xo_ref[rows, :] = x2
        ho_ref[rows, :] = (_rms(x2, g) * sc1 + sh).astype(ho_ref.dtype)


def merge_projection(y_conv, y_att, y_ssm, gn_att, w_o, x, gain2, mod4, layer, *, m_tiles, tiles_per_seq,
                     n_batch, tm):
    d = x.shape[1]
    rows = m_tiles * tm

    def mod_spec(col):
        return pl.BlockSpec((None, None, 1, d),
                            lambda i: (layer, jnp.minimum(i // tiles_per_seq, n_batch), 0, col))

    a_spec = lambda a: pl.BlockSpec((tm, a.shape[1]), lambda i: (i, 0))
    return pl.pallas_call(
        functools.partial(_merge_kernel, row_chunk=min(256, tm)),
        grid=(m_tiles,),
        in_specs=[
            a_spec(y_conv), a_spec(y_att), a_spec(y_ssm),
            pl.BlockSpec((1, y_att.shape[1]), lambda i: (0, 0)),
            pl.BlockSpec((None, d, d), lambda i: (layer, 0, 0)),
            pl.BlockSpec((tm, d), lambda i: (i, 0)),
            mod_spec(2),
            pl.BlockSpec((1, d), lambda i: (0, 0)),
            mod_spec(3),
            mod_spec(4),
        ],
        out_specs=[pl.BlockSpec((tm, d), lambda i: (i, 0)), pl.BlockSpec((tm, d), lambda i: (i, 0))],
        out_shape=[jax.ShapeDtypeStruct((rows, d), F32), jax.ShapeDtypeStruct((rows, d), BF16)],
        compiler_params=_cparams(("arbitrary",), "merge"),
    )(y_conv, y_att, y_ssm, gn_att.reshape(1, -1), w_o, x, mod4, gain2.reshape(1, d), mod4, mod4)


def _cast_rows(src_ref, dst_ref, row_chunk):
    for r in range(0, src_ref.shape[0], row_chunk):
        dst_ref[r:r + row_chunk, :] = src_ref[r:r + row_chunk, :].astype(dst_ref.dtype)


def _ffn_up_kernel(h_ref, wg_ref, wu_ref, o_ref, wg_scr, wu_scr):
    @pl.when(pl.program_id(1) == 0)
    def _():
        _cast_rows(wg_ref, wg_scr, CAST_ROWS)
        _cast_rows(wu_ref, wu_scr, CAST_ROWS)

    h = h_ref[...]
    gt = jnp.dot(h, wg_scr[...], preferred_element_type=F32)
    up = jnp.dot(h, wu_scr[...], preferred_element_type=F32)
    o_ref[...] = (gt * jax.nn.sigmoid(gt) * up).astype(o_ref.dtype)


def ffn_up(h, w_gate, w_up, layer, *, m_tiles, tm, tn):
    d = h.shape[1]
    f = w_gate.shape[2]
    w_spec = pl.BlockSpec((None, d, tn), lambda j, i: (layer, 0, j))
    return pl.pallas_call(
        _ffn_up_kernel,
        grid=(f // tn, m_tiles),
        in_specs=[pl.BlockSpec((tm, d), lambda j, i: (i, 0)), w_spec, w_spec],
        out_specs=pl.BlockSpec((tm, tn), lambda j, i: (i, j)),
        out_shape=jax.ShapeDtypeStruct((m_tiles * tm, f), BF16),
        scratch_shapes=[pltpu.VMEM((d, tn), BF16)] * 2,
        compiler_params=_cparams(("arbitrary", "arbitrary"), "ffn_up"),
    )(h, w_gate, w_up)


def _ffn_down_kernel(a_ref, w_ref, x_ref, gate_ref, o_ref, w_scr):
    @pl.when(pl.program_id(1) == 0)
    def _():
        _cast_rows(w_ref, w_scr, CAST_ROWS)

    acc = jnp.dot(a_ref[...], w_scr[...], preferred_element_type=F32)
    o_ref[...] = x_ref[...] + gate_ref[...] * acc


def ffn_down(a, w_down, x, mod4, layer, *, m_tiles, tiles_per_seq, n_batch, tm, tn):
    f = a.shape[1]
    d = x.shape[1]
    gate_blocks = d // tn
    return pl.pallas_call(
        _ffn_down_kernel,
        grid=(d // tn, m_tiles),
        in_specs=[
            pl.BlockSpec((tm, f), lambda j, i: (i, 0)),
            pl.BlockSpec((None, f, tn), lambda j, i: (layer, 0, j)),
            pl.BlockSpec((tm, tn), lambda j, i: (i, j)),
            pl.BlockSpec((None, None, 1, tn),
                         lambda j, i: (layer, jnp.minimum(i // tiles_per_seq, n_batch), 0,
                                       5 * gate_blocks + j)),
        ],
        out_specs=pl.BlockSpec((tm, tn), lambda j, i: (i, j)),
        out_shape=jax.ShapeDtypeStruct((m_tiles * tm, d), F32),
        scratch_shapes=[pltpu.VMEM((f, tn), BF16)],
        compiler_params=_cparams(("arbitrary", "arbitrary"), "ffn_down"),
    )(a, w_down, x, mod4)


HALO = 16


def _conv_kernel(h_ref, bg_ref, cg_ref, hp_ref, cp_ref, hn_ref, cn_ref, w_ref, gn_ref, o_ref, z_scr, *,
                 tiles_per_seq, lat_tiles):
    i = pl.program_id(0)
    t = h_ref.shape[0]
    whole_seq = i >= lat_tiles
    first = jnp.logical_or(i % tiles_per_seq == 0, whole_seq)
    last = jnp.logical_or(i % tiles_per_seq == tiles_per_seq - 1, whole_seq)
    z_prev = cp_ref[HALO - 1:HALO, :].astype(F32) * hp_ref[HALO - 1:HALO, :].astype(F32)
    z_next = cn_ref[0:1, :].astype(F32) * hn_ref[0:1, :].astype(F32)
    z_scr[7:8, :] = jnp.where(first, 0.0, z_prev)
    z_scr[8:8 + t, :] = cg_ref[...].astype(F32) * h_ref[...].astype(F32)
    z_scr[8 + t:9 + t, :] = jnp.where(last, 0.0, z_next)
    y = bg_ref[...].astype(F32) * (w_ref[0:1, :] * z_scr[7:7 + t, :] + w_ref[1:2, :] * z_scr[8:8 + t, :]
                                   + w_ref[2:3, :] * z_scr[9:9 + t, :])
    o_ref[...] = _rms(y, gn_ref[...]).astype(o_ref.dtype)


def conv_mixer(proj, conv_w, gn, *, rows, tile, tiles_per_seq, lat_tiles):
    ch = D_CONV
    n_tiles = rows // tile
    hb = tile // HALO
    last_blk = proj.shape[0] // HALO - 1
    main = lambda c: pl.BlockSpec((tile, ch), lambda i: (i, c))
    prev = lambda c: pl.BlockSpec((HALO, ch), lambda i: (jnp.maximum(i * hb - 1, 0), c))
    nxt = lambda c: pl.BlockSpec((HALO, ch), lambda i: (jnp.minimum((i + 1) * hb, last_blk), c))
    return pl.pallas_call(
        functools.partial(_conv_kernel, tiles_per_seq=tiles_per_seq, lat_tiles=lat_tiles),
        grid=(n_tiles,),
        in_specs=[main(0), main(1), main(2), prev(0), prev(2), nxt(0), nxt(2),
                  pl.BlockSpec((CONV_W, ch), lambda i: (0, 0)), pl.BlockSpec((1, ch), lambda i: (0, 0))],
        out_specs=pl.BlockSpec((tile, ch), lambda i: (i, 0)),
        out_shape=jax.ShapeDtypeStruct((rows, ch), BF16),
        scratch_shapes=[pltpu.VMEM((tile + 16, ch), F32)],
        compiler_params=_cparams(("arbitrary",), "conv"),
    )(proj, proj, proj, proj, proj, proj, proj, conv_w, gn.reshape(1, ch))


def _qkv_proj_kernel(cq_ref, ckv_ref, kr_ref, gq_ref, gkv_ref, wq_ref, wkv_ref, cos_ref, sin_ref,
                     q_ref, k_ref, v_ref):
    cos = cos_ref[...]
    sin = sin_ref[...]
    a = _rms(cq_ref[...].astype(F32), gq_ref[...]).astype(BF16)
    q = jnp.dot(a, wq_ref[...], preferred_element_type=F32)
    scale = MLA_SCALE * LOG2E
    for h in range(MLA_HEADS):
        c0 = h * QK_PAD
        q_ref[:, c0:c0 + QK_NOPE] = (q[:, c0:c0 + QK_NOPE] * scale).astype(q_ref.dtype)
        blk = q[:, c0 + QK_NOPE:c0 + QK_PAD]
        rot = blk * cos + pltpu.roll(blk, QK_ROPE, axis=1) * sin
        q_ref[:, c0 + QK_NOPE:c0 + QK_PAD] = (rot * scale).astype(q_ref.dtype)

    a = _rms(ckv_ref[...].astype(F32), gkv_ref[...]).astype(BF16)
    kv = jnp.dot(a, wkv_ref[...], preferred_element_type=F32)
    blk = kr_ref[...].astype(F32)
    rot = (blk * cos + pltpu.roll(blk, QK_ROPE, axis=1) * sin).astype(k_ref.dtype)
    nk = MLA_HEADS * QK_NOPE
    for h in range(MLA_HEADS):
        c0 = h * QK_PAD
        k_ref[:, c0:c0 + QK_NOPE] = kv[:, h * QK_NOPE:(h + 1) * QK_NOPE].astype(k_ref.dtype)
        k_ref[:, c0 + QK_NOPE:c0 + QK_PAD] = rot
    ones = jnp.ones((v_ref.shape[0], V_PAD - V_HEAD), v_ref.dtype)
    for h in range(MLA_HEADS):
        c0 = h * V_PAD
        v_ref[:, c0:c0 + V_HEAD] = kv[:, nk + h * V_HEAD:nk + (h + 1) * V_HEAD].astype(v_ref.dtype)
        v_ref[:, c0 + V_HEAD:c0 + V_PAD] = ones


def qkv_projection(proj, q_norm, kv_norm, w_q, w_kv, cos_t, sin_t, *, rows, tm, lat_tiles, tiles_per_seq):
    tab_map = lambda i: (jnp.where(i < lat_tiles, i % tiles_per_seq, tiles_per_seq), 0)
    nk = MLA_HEADS * QK_PAD
    nv = MLA_HEADS * V_PAD
    return pl.pallas_call(
        _qkv_proj_kernel,
        grid=(rows // tm,),
        in_specs=[
            pl.BlockSpec((tm, Q_RANK), lambda i: (i, COL_CQ // Q_RANK)),
            pl.BlockSpec((tm, KV_RANK), lambda i: (i, COL_CKV // KV_RANK)),
            pl.BlockSpec((tm, LANE), lambda i: (i, COL_KR // LANE)),
            pl.BlockSpec((1, Q_RANK), lambda i: (0, 0)),
            pl.BlockSpec((1, KV_RANK), lambda i: (0, 0)),
            pl.BlockSpec((Q_RANK, nk), lambda i: (0, 0)),
            pl.BlockSpec((KV_RANK, MLA_HEADS * (QK_NOPE + V_HEAD)), lambda i: (0, 0)),
            pl.BlockSpec((tm, LANE), tab_map),
            pl.BlockSpec((tm, LANE), tab_map),
        ],
        out_specs=[pl.BlockSpec((tm, nk), lambda i: (i, 0)), pl.BlockSpec((tm, nk), lambda i: (i, 0)),
                   pl.BlockSpec((tm, nv), lambda i: (i, 0))],
        out_shape=[jax.ShapeDtypeStruct((rows, nk), BF16), jax.ShapeDtypeStruct((rows, nk), BF16),
                   jax.ShapeDtypeStruct((rows, nv), BF16)],
        compiler_params=_cparams(("arbitrary",), "qkv"),
    )(proj, proj, proj, q_norm.reshape(1, Q_RANK), kv_norm.reshape(1, KV_RANK), w_q, w_kv, cos_t, sin_t)


_NT = (((1,), (1,)), ((), ()))


HEAD_SPLIT = 2


def _attn_kernel(q_ref, kl_ref, kc_ref, vl_ref, vc_ref, o_ref, *, lat_steps, steps_per_batch, ctx_len):
    s = pl.program_id(0)
    hpg = MLA_HEADS // HEAD_SPLIT

    def heads(q_rows, c_rows, with_latent):
        for h in range(hpg):
            q = q_ref[q_rows, h * QK_PAD:(h + 1) * QK_PAD]
            sc = lax.dot_general(q, kc_ref[c_rows, h * QK_PAD:(h + 1) * QK_PAD], _NT,
                                 preferred_element_type=F32)
            m = jnp.max(sc, axis=-1, keepdims=True)
            if with_latent:
                sl = lax.dot_general(q, kl_ref[:, h * QK_PAD:(h + 1) * QK_PAD], _NT,
                                     preferred_element_type=F32)
                m = jnp.maximum(m, jnp.max(sl, axis=-1, keepdims=True))
            o = jnp.dot(jnp.exp2(sc - m).astype(BF16), vc_ref[c_rows, h * V_PAD:(h + 1) * V_PAD],
                        preferred_element_type=F32)
            if with_latent:
                o = o + jnp.dot(jnp.exp2(sl - m).astype(BF16), vl_ref[:, h * V_PAD:(h + 1) * V_PAD],
                                preferred_element_type=F32)
            o_ref[q_rows, h * V_HEAD:(h + 1) * V_HEAD] = (o[:, :V_HEAD] / o[:, V_HEAD:]).astype(o_ref.dtype)

    @pl.when(s < lat_steps)
    def _():
        b = s // steps_per_batch
        c0 = pl.multiple_of((b % 2) * ctx_len, ctx_len)
        heads(slice(None), pl.ds(c0, ctx_len), True)

    @pl.when(s >= lat_steps)
    def _():
        for hb in range(2):
            rows = slice(hb * ctx_len, (hb + 1) * ctx_len)
            heads(rows, rows, False)


def latent_attention(q, k, v, *, n_batch, seq, ctx_len, tq, ctx_queries, rows_out):
    assert tq == 2 * ctx_len and n_batch % 2 == 0 and seq % tq == 0
    lat_tiles = seq // tq
    spb = HEAD_SPLIT * lat_tiles
    lat_steps = n_batch * spb
    ctx_steps = (n_batch // 2) * HEAD_SPLIT if ctx_queries else 0
    ctx_blk0 = n_batch * seq // tq
    nq = MLA_HEADS * QK_PAD // HEAD_SPLIT
    nv = MLA_HEADS * V_HEAD // HEAD_SPLIT
    nvp = MLA_HEADS * V_PAD // HEAD_SPLIT

    def split(s):
        lat = s < lat_steps
        cs = s - lat_steps
        b = jnp.where(lat, s // spb, n_batch - 1)
        hh = jnp.where(lat, (s // lat_tiles) % HEAD_SPLIT, cs % HEAD_SPLIT)
        q_blk = jnp.where(lat, b * lat_tiles + s % lat_tiles, ctx_blk0 + cs // HEAD_SPLIT)
        c_blk = jnp.where(lat, ctx_blk0 + b // 2, ctx_blk0 + cs // HEAD_SPLIT)
        return b, hh, q_blk, c_blk

    def lat_map(s):
        b, hh, _, _ = split(s)
        return (b, jnp.where(s < lat_steps, hh, HEAD_SPLIT - 1))

    q_map = lambda s: (split(s)[2], split(s)[1])
    c_map = lambda s: (split(s)[3], split(s)[1])
    return pl.pallas_call(
        functools.partial(_attn_kernel, lat_steps=lat_steps, steps_per_batch=spb, ctx_len=ctx_len),
        grid=(lat_steps + ctx_steps,),
        in_specs=[
            pl.BlockSpec((tq, nq), q_map),
            pl.BlockSpec((seq, nq), lat_map),
            pl.BlockSpec((tq, nq), c_map),
            pl.BlockSpec((seq, nvp), lat_map),
            pl.BlockSpec((tq, nvp), c_map),
        ],
        out_specs=pl.BlockSpec((tq, nv), q_map),
        out_shape=jax.ShapeDtypeStruct((rows_out, MLA_HEADS * V_HEAD), BF16),
        compiler_params=_cparams(("arbitrary",), "attention"),
    )(q, k, k, v, v)


def _s5_disc_kernel(are_ref, aim_ref, ldt_ref, abr_ref, abi_ref, cfr_ref, cfi_ref):
    ar = are_ref[...]
    ai = aim_ref[...]
    dt = jnp.exp(ldt_ref[...])
    mag = jnp.exp(ar * dt)
    th = ai * dt
    br = mag * jnp.cos(th)
    bi = mag * jnp.sin(th)
    nr = br - 1.0
    den = ar * ar + ai * ai
    abr_ref[...] = br
    abi_ref[...] = bi
    cfr_ref[...] = (nr * ar + bi * ai) / den
    cfi_ref[...] = (bi * ar - nr * ai) / den


def s5_discretise(a_re, a_im, log_dt):
    shp = a_re.shape
    rows = shp[0] * shp[1] * shp[2]
    flat = lambda t: t.reshape(rows, shp[3])
    ldt = jnp.broadcast_to(log_dt[..., None], shp)
    outs = pl.pallas_call(
        _s5_disc_kernel,
        out_shape=[jax.ShapeDtypeStruct((rows, shp[3]), F32)] * 4,
    )(flat(a_re), flat(a_im), flat(ldt))
    return [o.reshape(shp) for o in outs]


def _cmul(xr, xi, yr, yi):
    return xr * yr - xi * yi, xr * yi + xi * yr


def _s5_mats_kernel(abc_re, abc_im, cex_re, cex_im, afb_re, afb_im, cfb_re, cfb_im, bfb_re, bfb_im,
                    cpp_re, cpp_im, bx_ref, by_ref, ma_ref, mi_ref, mo_ref, a16_ref):
    tw = CHUNK * SSM_GROUP
    p = SSM_STATE
    lane = lax.broadcasted_iota(jnp.int32, (p, tw), 1)
    tblk = lane // SSM_GROUP
    lane16 = lax.broadcasted_iota(jnp.int32, (SSM_GROUP, tw), 1)
    lane128 = lax.broadcasted_iota(jnp.int32, (1, 2 * p), 1)
    fwd_lanes = lane128 < p
    sgn = jnp.where(lax.broadcasted_iota(jnp.int32, (SSM_GROUP, 2 * p), 1) < p, 1.0, -1.0)
    zeros = jnp.zeros((p, tw), F32)
    mi_blocks = []
    for d in range(2):
        ar = jnp.broadcast_to(abc_re[d], (p, tw))
        ai = jnp.broadcast_to(abc_im[d], (p, tw))
        sel_r = tblk if d == 0 else (CHUNK - 1) - tblk
        sq_re, sq_im = ar, ai
        r_re, r_im = jnp.ones((p, tw), F32), zeros
        for b in range((CHUNK - 1).bit_length()):
            bit = ((sel_r >> b) & 1) == 1
            n_re, n_im = _cmul(r_re, r_im, sq_re, sq_im)
            r_re = jnp.where(bit, n_re, r_re)
            r_im = jnp.where(bit, n_im, r_im)
            sq_re, sq_im = _cmul(sq_re, sq_im, sq_re, sq_im)
        e_re, e_im = _cmul(r_re, r_im, ar, ai)
        cr = cex_re[d]
        ci = cex_im[d]
        w_re, w_im = _cmul(cr, ci, e_re, e_im)
        base = d * 4 * p
        mdt = mo_ref.dtype
        zeros_m = zeros.astype(mdt)
        if d == 0:
            mo_ref[base:base + p, :] = w_re.astype(mdt)
            mo_ref[base + p:base + 2 * p, :] = zeros_m
            mo_ref[base + 2 * p:base + 3 * p, :] = (-w_im).astype(mdt)
            mo_ref[base + 3 * p:base + 4 * p, :] = zeros_m
        else:
            mo_ref[base:base + p, :] = zeros_m
            mo_ref[base + p:base + 2 * p, :] = w_re.astype(mdt)
            mo_ref[base + 2 * p:base + 3 * p, :] = zeros_m
            mo_ref[base + 3 * p:base + 4 * p, :] = (-w_im).astype(mdt)
        rr_re, rr_im = _cmul(cr, ci, r_re, r_im)
        stacked = jnp.concatenate([rr_re, rr_im], axis=0)
        lm = sgn * cpp_re[d] * bx_ref[d] - cpp_im[d] * by_ref[d]
        kall = jnp.dot(lm, stacked, preferred_element_type=F32, precision=lax.Precision.HIGHEST)
        for s in range(CHUNK):
            if d == 0:
                shift = SSM_GROUP * s
                keep = lane16 >= SSM_GROUP * s
            else:
                shift = (SSM_GROUP * (s + 1)) % tw
                keep = lane16 < SSM_GROUP * (s + 1)
            rolled = pltpu.roll(kall, shift, axis=1) if shift else kall
            blk = jnp.where(keep, rolled, 0.0)
            if d == 0:
                mi_blocks.append(blk)
            else:
                mi_ref[SSM_GROUP * s:SSM_GROUP * (s + 1), :] = (mi_blocks[s] + blk).astype(mi_ref.dtype)

    a_re = afb_re[...]
    a_im = afb_im[...]
    cf_re = cfb_re[...]
    cf_im = cfb_im[...]
    b_re = bfb_re[...]
    b_im = bfb_im[...]
    powers = []
    qr, qi = jnp.ones((1, 2 * p), F32), jnp.zeros((1, 2 * p), F32)
    for k in range(CHUNK + 1):
        powers.append((qr, qi))
        if k < CHUNK:
            qr, qi = _cmul(qr, qi, a_re, a_im)
    for s in range(CHUNK):
        g_re = jnp.where(fwd_lanes, powers[CHUNK - 1 - s][0], powers[s][0])
        g_im = jnp.where(fwd_lanes, powers[CHUNK - 1 - s][1], powers[s][1])
        g_re, g_im = _cmul(g_re, g_im, cf_re, cf_im)
        rows = slice(SSM_GROUP * s, SSM_GROUP * (s + 1))
        ma_ref[rows, 0:2 * p] = (g_re * b_re - g_im * b_im).astype(ma_ref.dtype)
        ma_ref[rows, 2 * p:4 * p] = (g_re * b_im + g_im * b_re).astype(ma_ref.dtype)
    a16_ref[0:1, :] = powers[CHUNK][0]
    a16_ref[1:2, :] = powers[CHUNK][1]


def s5_matrices(abar_re, abar_im, coef_re, coef_im, b_re, b_im, c_re, c_im):
    depth, _, g, p = abar_re.shape
    hg = b_re.shape[-1]
    tw = CHUNK * hg
    t0213 = lambda t: t.transpose(0, 2, 1, 3)
    abc = [t0213(t)[..., None] for t in (abar_re, abar_im)]
    cex = [jnp.tile(t.transpose(0, 2, 1, 4, 3), (1, 1, 1, 1, CHUNK)) for t in (c_re, c_im)]
    fb = lambda t: t0213(t).reshape(depth, g, 1, 2 * p)
    afb = [fb(t) for t in (abar_re, abar_im)]
    cfb = [fb(t) for t in (coef_re, coef_im)]
    bfb = [t.transpose(0, 2, 4, 1, 3).reshape(depth, g, hg, 2 * p) for t in (b_re, b_im)]
    cpp = [t0213(jnp.concatenate([t, t], axis=-1))[:, :, :, None, :] for t in (coef_re, coef_im)]
    bt_re = b_re.transpose(0, 2, 1, 4, 3)
    bt_im = b_im.transpose(0, 2, 1, 4, 3)
    bx = jnp.concatenate([bt_re, bt_im], axis=-1)
    by = jnp.concatenate([bt_im, bt_re], axis=-1)
    ins = [*abc, *cex, *afb, *cfb, *bfb, *cpp, bx, by]

    def spec(t):
        blk = (None, None) + t.shape[2:]
        nz = len(t.shape) - 2
        return pl.BlockSpec(blk, lambda l, gi: (l, gi) + (0,) * nz)

    def ospec(r, c):
        return pl.BlockSpec((None, None, r, c), lambda l, gi: (l, gi, 0, 0))

    return pl.pallas_call(
        _s5_mats_kernel,
        grid=(depth, g),
        in_specs=[spec(t) for t in ins],
        out_specs=[ospec(tw, 4 * p), ospec(tw, tw), ospec(8 * p, tw), ospec(2, 2 * p)],
        out_shape=[
            jax.ShapeDtypeStruct((depth, g, tw, 4 * p), BF16),
            jax.ShapeDtypeStruct((depth, g, tw, tw), BF16),
            jax.ShapeDtypeStruct((depth, g, 8 * p, tw), BF16),
            jax.ShapeDtypeStruct((depth, g, 2, 2 * p), F32),
        ],
        compiler_params=_cparams(("arbitrary", "arbitrary"), "s5_mats"),
    )(*ins)


GROUP_BLOCK = LANE // SSM_GROUP


def lane_swap_matrix():
    idx = jnp.arange(GROUP_BLOCK * LANE)
    a, b, c = idx // LANE, (idx // SSM_GROUP) % GROUP_BLOCK, idx % SSM_GROUP
    dst = b * LANE + a * SSM_GROUP + c
    return (dst[:, None] == idx[None, :]).astype(BF16)


def _s5_main_kernel(*refs, order_f, order_b, nb, lat_chunks, ctx_chunks):
    (u_ref, p_ref, ma_ref, mi_ref, mo_ref, a_ref, y_ref, ug_scr, yg_scr,
     xr_scr, xi_scr, fr_scr, fi_scr, br_scr, bi_scr) = refs
    p2 = 2 * SSM_STATE
    half = CHUNK // 2
    rows = u_ref.shape[0] // CHUNK
    n_sub = xr_scr.shape[0]

    for th in range(2):
        slab = jnp.concatenate(
            [u_ref[pl.ds(half * th + tl, rows, stride=CHUNK), :].astype(BF16) for tl in range(half)], axis=1)
        perm = jnp.dot(slab, p_ref[...], preferred_element_type=F32).astype(BF16)
        for g in range(GROUP_BLOCK):
            ug_scr[g, :, th * LANE:(th + 1) * LANE] = perm[:, g * LANE:(g + 1) * LANE]

    segments = ((0, lat_chunks), (nb * lat_chunks, ctx_chunks))

    def to_chunk_major(dst_ref, k, val):
        for r0, n in segments:
            for b in range(nb):
                dst_ref[k, pl.ds(r0 + b, n, stride=nb), :] = val[r0 + b * n:r0 + (b + 1) * n, :]

    def to_batch_major(src_ref, k):
        return jnp.concatenate([src_ref[k, pl.ds(r0 + b, n, stride=nb), :]
                                for r0, n in segments for b in range(nb)], axis=0)

    def rows_of(j):
        return slice(nb * j, nb * j + nb)

    fwd_lanes = lax.broadcasted_iota(jnp.int32, (nb, p2), 1) < SSM_STATE
    for g0 in range(0, GROUP_BLOCK, n_sub):
        for k in range(n_sub):
            x = jnp.dot(ug_scr[g0 + k], ma_ref[g0 + k], preferred_element_type=F32)
            to_chunk_major(xr_scr, k, x[:, 0:p2])
            to_chunk_major(xi_scr, k, x[:, p2:2 * p2])

        a_re = [a_ref[g0 + k, 0:1, :] for k in range(n_sub)]
        a_im = [a_ref[g0 + k, 1:2, :] for k in range(n_sub)]
        s_re = [jnp.zeros((nb, p2), F32) for _ in range(n_sub)]
        s_im = [jnp.zeros((nb, p2), F32) for _ in range(n_sub)]
        for jf, jb in zip(order_f, order_b):
            rf = rows_of(jf)
            rb = rows_of(jb)
            for k in range(n_sub):
                fr_scr[k, rf, :] = s_re[k]
                fi_scr[k, rf, :] = s_im[k]
                br_scr[k, rb, :] = s_re[k]
                bi_scr[k, rb, :] = s_im[k]
                xr = jnp.where(fwd_lanes, xr_scr[k, rf, :], xr_scr[k, rb, :])
                xi = jnp.where(fwd_lanes, xi_scr[k, rf, :], xi_scr[k, rb, :])
                s_re[k], s_im[k] = (a_re[k] * s_re[k] - a_im[k] * s_im[k] + xr,
                                    a_re[k] * s_im[k] + a_im[k] * s_re[k] + xi)

        for k in range(n_sub):
            g = g0 + k
            y = jnp.dot(ug_scr[g], mi_ref[g], preferred_element_type=F32)
            sp_f = jnp.concatenate([to_batch_major(fr_scr, k), to_batch_major(fi_scr, k)], axis=1).astype(BF16)
            sp_b = jnp.concatenate([to_batch_major(br_scr, k), to_batch_major(bi_scr, k)], axis=1).astype(BF16)
            y = y + jnp.dot(sp_f, mo_ref[g, 0:2 * p2, :], preferred_element_type=F32)
            y = y + jnp.dot(sp_b, mo_ref[g, 2 * p2:4 * p2, :], preferred_element_type=F32)
            yg_scr[g] = y.astype(BF16)

    for th in range(2):
        slab = jnp.concatenate([yg_scr[g, :, th * LANE:(th + 1) * LANE] for g in range(GROUP_BLOCK)], axis=1)
        perm = jnp.dot(slab, p_ref[...], preferred_element_type=F32)
        for tl in range(half):
            y_ref[pl.ds(half * th + tl, rows, stride=CHUNK), :] = perm[:, tl * LANE:(tl + 1) * LANE]


def s5_chunked(u, ma, mi, mo, a16, layer, *, lat_chunks, ctx_chunks, nb):
    r_all = u.shape[0]
    rows = r_all // CHUNK
    tw = CHUNK * SSM_GROUP
    j_tot = lat_chunks + ctx_chunks
    order_f = tuple(range(lat_chunks, j_tot)) + tuple(range(lat_chunks))
    order_b = tuple(range(j_tot - 1, lat_chunks - 1, -1)) + tuple(range(lat_chunks - 1, -1, -1))
    p4 = 4 * SSM_STATE
    n_gb = SSM_GROUPS // GROUP_BLOCK
    n_sub = GROUP_BLOCK // 2

    def wspec(r, c):
        return pl.BlockSpec((None, GROUP_BLOCK, r, c), lambda gb: (layer, gb, 0, 0))

    return pl.pallas_call(
        functools.partial(_s5_main_kernel, order_f=order_f, order_b=order_b, nb=nb,
                          lat_chunks=lat_chunks, ctx_chunks=ctx_chunks),
        grid=(n_gb,),
        in_specs=[
            pl.BlockSpec((r_all, LANE), lambda gb: (0, gb)),
            pl.BlockSpec((GROUP_BLOCK * LANE, GROUP_BLOCK * LANE), lambda gb: (0, 0)),
            wspec(tw, p4), wspec(tw, tw), wspec(2 * p4, tw), wspec(2, 2 * SSM_STATE),
        ],
        out_specs=pl.BlockSpec((r_all, LANE), lambda gb: (0, gb)),
        out_shape=jax.ShapeDtypeStruct((r_all, D_SSM), F32),
        scratch_shapes=[
            pltpu.VMEM((GROUP_BLOCK, rows, tw), BF16),
            pltpu.VMEM((GROUP_BLOCK, rows, tw), BF16),
        ] + [pltpu.VMEM((n_sub, rows, 2 * SSM_STATE), F32)] * 6,
        compiler_params=_cparams(("arbitrary",), "s5_main"),
    )(u, lane_swap_matrix(), ma, mi, mo, a16)


def _s5_out_kernel(y_ref, u_ref, d_ref, w_ref, b_ref, gn_ref, o_ref):
    y = y_ref[...].astype(F32) + d_ref[...] * u_ref[...].astype(F32)
    g = jax.nn.gelu(y)
    z = jnp.dot(g.astype(BF16), w_ref[...], preferred_element_type=F32) + b_ref[...]
    o_ref[...] = _rms(g * jax.nn.sigmoid(z), gn_ref[...]).astype(o_ref.dtype)


def s5_output(y, u, ssm_d, w_glu, b_glu, gn, *, rows, tm):
    ch = D_SSM
    vec = lambda: pl.BlockSpec((1, ch), lambda i: (0, 0))
    return pl.pallas_call(
        _s5_out_kernel,
        grid=(rows // tm,),
        in_specs=[
            pl.BlockSpec((tm, ch), lambda i: (i, 0)),
            pl.BlockSpec((tm, ch), lambda i: (i, 0)),
            vec(),
            pl.BlockSpec((ch, ch), lambda i: (0, 0)),
            vec(), vec(),
        ],
        out_specs=pl.BlockSpec((tm, ch), lambda i: (i, 0)),
        out_shape=jax.ShapeDtypeStruct((rows, ch), BF16),
        compiler_params=_cparams(("arbitrary",), "s5_out"),
    )(y, u, ssm_d.reshape(1, ch), w_glu, b_glu.reshape(1, ch), gn.reshape(1, ch))


def _final_norm_kernel(x_ref, g_ref, o_ref):
    o_ref[...] = _rms(x_ref[...], g_ref[...])


def final_rms_norm(x, g, *, rows, tm):
    d = x.shape[1]
    return pl.pallas_call(
        _final_norm_kernel,
        grid=(rows // tm,),
        in_specs=[pl.BlockSpec((tm, d), lambda i: (i, 0)), pl.BlockSpec((1, d), lambda i: (0, 0))],
        out_specs=pl.BlockSpec((tm, d), lambda i: (i, 0)),
        out_shape=jax.ShapeDtypeStruct((rows, d), F32),
        compiler_params=_cparams(("arbitrary",), "final_norm"),
    )(x, g.reshape(1, d))


def _rope_partner_perm():
    idx = []
    for i in range(QK_ROPE):
        idx.append(i + 16 if (i % 32) < 16 else i - 16)
    return jnp.asarray(idx, jnp.int32)


def rope_tables(seq, tab_tile):
    rows = seq // GRID_W
    row = jnp.repeat(jnp.arange(rows, dtype=F32), GRID_W)
    col = jnp.tile(jnp.arange(GRID_W, dtype=F32), rows)
    n_freq = QK_ROPE // 4
    inv = ROPE_BASE ** (-jnp.arange(n_freq, dtype=F32) / n_freq)
    ar = row[:, None] * inv
    ac = col[:, None] * inv
    cos = jnp.concatenate([jnp.cos(ar), jnp.cos(ar), jnp.cos(ac), jnp.cos(ac)], axis=1)
    sin = jnp.concatenate([-jnp.sin(ar), jnp.sin(ar), -jnp.sin(ac), jnp.sin(ac)], axis=1)
    cos = jnp.concatenate([cos, jnp.ones((tab_tile, QK_ROPE), F32)], axis=0)
    sin = jnp.concatenate([sin, jnp.zeros((tab_tile, QK_ROPE), F32)], axis=0)
    pad = jnp.zeros((seq + tab_tile, LANE - QK_ROPE), F32)
    return jnp.concatenate([cos, pad], axis=1), jnp.concatenate([sin, pad], axis=1)


def prep_w_in(w_in):
    assert COL_U == COL_CQ + Q_RANK
    s = [COL_U, COL_U + KV_RANK, COL_U + KV_RANK + QK_ROPE, COL_U + KV_RANK + QK_ROPE + D_SSM]
    ckv, kr, u = [w_in[..., s[i]:s[i + 1]] for i in range(3)]
    krp = kr[..., _rope_partner_perm()]
    pad = jnp.zeros(w_in.shape[:-1] + (N_PROJ - (COL_KR + LANE),), w_in.dtype)
    tail = jnp.concatenate([u, ckv, kr, krp, pad], axis=-1).astype(BF16)
    return w_in[..., :COL_U].astype(BF16), tail


def prep_w_uq(w_uq):
    r = w_uq.shape[0]
    w = w_uq.reshape(r, MLA_HEADS, QK_NOPE + QK_ROPE)
    rope = w[:, :, QK_NOPE:]
    w = jnp.concatenate([w, rope[:, :, _rope_partner_perm()]], axis=-1)
    return w.reshape(r, MLA_HEADS * QK_PAD).astype(BF16)


def prep_w_ukv(w_ukv):
    r = w_ukv.shape[0]
    w = w_ukv.reshape(r, MLA_HEADS, QK_NOPE + V_HEAD)
    kn = w[:, :, :QK_NOPE].reshape(r, MLA_HEADS * QK_NOPE)
    v = w[:, :, QK_NOPE:].reshape(r, MLA_HEADS * V_HEAD)
    return jnp.concatenate([kn, v], axis=1).astype(BF16)


def kernel(x, c, ctx, c_ctx, w_ada, b_ada, norm1_g, norm2_g, w_in, conv_w, mla_q_norm, w_uq, mla_kv_norm, w_ukv, ssm_a_re, ssm_a_im, ssm_log_dt, ssm_b_re, ssm_b_im, ssm_c_re, ssm_c_im, ssm_d, w_glu, b_glu, mix_norm, w_o, w_gate, w_up, w_down, final_norm):
    nb, seq, d = x.shape
    ctx_len = ctx.shape[1]
    depth = w_ada.shape[0]
    r_lat = nb * seq
    r_ctx = nb * ctx_len
    r_all = r_lat + r_ctx
    assert seq % TM == 0 and r_ctx == TM and seq % CHUNK == 0 and ctx_len % CHUNK == 0
    tiles_per_seq = seq // TM
    lat_tiles_m = r_lat // TM
    tp = TILE_NARROW_ROWS
    tf = TILE_MERGE_ROWS
    assert nb + 1 <= SUBLANE and seq % tf == 0 and r_ctx % tf == 0 and seq % tp == 0 and r_ctx % tp == 0

    xs = jnp.concatenate([x.reshape(r_lat, d), ctx.reshape(r_ctx, d)], axis=0)

    cvec = jnp.concatenate([c, c_ctx[None, :], jnp.zeros((SUBLANE - nb - 1, d), F32)], axis=0)
    mod = ada_modulation(cvec, w_ada, b_ada)
    mod4 = mod.reshape(depth, SUBLANE, 1, 6 * d)

    cos_t, sin_t = rope_tables(seq, tp)
    abar_re, abar_im, coef_re, coef_im = s5_discretise(ssm_a_re, ssm_a_im, ssm_log_dt)
    ma, mi, mo, a16 = s5_matrices(abar_re, abar_im, coef_re, coef_im,
                                  ssm_b_re, ssm_b_im, ssm_c_re, ssm_c_im)
    w_in_main, w_in_tail = prep_w_in(w_in)
    w_o_b = w_o.astype(BF16)
    lat_chunks = seq // CHUNK
    ctx_chunks = ctx_len // CHUNK

    for i in range(depth):
        ctx_out = i < depth - 1
        m_tiles = lat_tiles_m + (1 if ctx_out else 0)
        rows_out = r_all if ctx_out else r_lat

        proj, u32 = input_projection(xs, norm1_g[i], mod4, i, w_in_main, w_in_tail,
                                     tiles_per_seq=tiles_per_seq, n_batch=nb)

        y_conv = conv_mixer(proj, conv_w[i], mix_norm[i, :D_CONV], rows=rows_out, tile=ctx_len,
                            tiles_per_seq=seq // ctx_len, lat_tiles=r_lat // ctx_len)

        q, k, v = qkv_projection(proj, mla_q_norm[i], mla_kv_norm[i], prep_w_uq(w_uq[i]),
                                 prep_w_ukv(w_ukv[i]), cos_t, sin_t, rows=r_all, tm=tp,
                                 lat_tiles=r_lat // tp, tiles_per_seq=seq // tp)
        y_att = latent_attention(q, k, v, n_batch=nb, seq=seq, ctx_len=ctx_len, tq=2 * ctx_len,
                                 ctx_queries=ctx_out, rows_out=rows_out)

        y_tok = s5_chunked(u32, ma, mi, mo, a16, i, lat_chunks=lat_chunks, ctx_chunks=ctx_chunks, nb=nb)
        y_ssm = s5_output(y_tok, u32, ssm_d[i], w_glu[i].astype(BF16), b_glu[i],
                          mix_norm[i, D_CONV + D_ATTN:], rows=rows_out, tm=tp)

        x2, h2 = merge_projection(y_conv, y_att, y_ssm, mix_norm[i, D_CONV:D_CONV + D_ATTN], w_o_b, xs,
                                  norm2_g[i], mod4, i,
                                  m_tiles=rows_out // tf, tiles_per_seq=seq // tf, n_batch=nb, tm=tf)
        hidden = ffn_up(h2, w_gate, w_up, i, m_tiles=m_tiles, tm=TM, tn=TILE_FFN_COLS)
        xs = ffn_down(hidden, w_down, x2, mod4, i, m_tiles=rows_out // tf, tiles_per_seq=seq // tf,
                      n_batch=nb, tm=tf, tn=TILE_FFN_COLS)

    out = final_rms_norm(xs, final_norm, rows=r_lat, tm=tp)
    return out.reshape(nb, seq, d)
```

```python
import functools
import math

import jax
import jax.numpy as jnp
from jax import lax
from jax.experimental import pallas as pl
from jax.experimental.pallas import tpu as pltpu

F32 = jnp.float32
BF16 = jnp.bfloat16

EPS = 1e-6
GRID_W = 64
CONV_W = 3
D_CONV = 512
D_SSM = 512
D_ATTN = 1024
MLA_HEADS = 8
QK_NOPE = 128
QK_ROPE = 64
V_HEAD = 128
Q_RANK = 512
KV_RANK = 256
ROPE_BASE = 10000.0
MLA_SCALE = (QK_NOPE + QK_ROPE) ** -0.5
SSM_GROUP = 16
SSM_GROUPS = 32
SSM_STATE = 64
CHUNK = 16

QK_PAD = 256
V_PAD = 256
LOG2E = math.log2(math.e)
LANE = 128
SUBLANE = 8
MIB = 1024 * 1024

TM = 1024
TILE_MERGE_ROWS = 512
TILE_NARROW_ROWS = 512
TILE_FFN_COLS = 512
CAST_ROWS = 512

VMEM_MIB = dict(ada=40, in_proj=52, merge=52, ffn_up=56, ffn_down=56, conv=32, qkv=40, attention=48,
                s5_mats=32, s5_main=56, s5_out=32, final_norm=32)

COL_CQ = 3 * D_CONV
COL_U = COL_CQ + Q_RANK
COL_CKV = COL_U + D_SSM
COL_KR = COL_CKV + KV_RANK
N_PROJ = 3072


def _cparams(sem, call):
    return pltpu.CompilerParams(dimension_semantics=sem, vmem_limit_bytes=VMEM_MIB[call] * MIB)


def _rms(x, g):
    return x * lax.rsqrt(jnp.mean(x * x, axis=-1, keepdims=True) + EPS) * g


def _ada_kernel(c_ref, w_ref, b_ref, o_ref):
    cv = c_ref[...]
    s = (cv * jax.nn.sigmoid(cv)).astype(BF16)
    o_ref[...] = jnp.dot(s, w_ref[...].astype(BF16), preferred_element_type=F32) + b_ref[...]


def ada_modulation(cvec, w_ada, b_ada, tn=1024):
    depth, d, n = w_ada.shape
    rows = cvec.shape[0]
    return pl.pallas_call(
        _ada_kernel,
        grid=(depth, n // tn),
        in_specs=[
            pl.BlockSpec((rows, d), lambda l, j: (0, 0)),
            pl.BlockSpec((None, d, tn), lambda l, j: (l, 0, j)),
            pl.BlockSpec((None, 1, tn), lambda l, j: (l, 0, j)),
        ],
        out_specs=pl.BlockSpec((None, rows, tn), lambda l, j: (l, 0, j)),
        out_shape=jax.ShapeDtypeStruct((depth, rows, n), F32),
        compiler_params=_cparams(("arbitrary", "arbitrary"), "ada"),
    )(cvec, w_ada, b_ada.reshape(depth, 1, n))


def _in_proj_kernel(x_ref, g_ref, sh_ref, sc_ref, wm_ref, wt_ref, o_ref, u_ref, a_scr, *, row_chunk,
                    main_tiles):
    tm = x_ref.shape[0]
    j = pl.program_id(1)

    @pl.when(j == 0)
    def _():
        g = g_ref[...]
        sh = sh_ref[...]
        sc1 = 1.0 + sc_ref[...]
        for r in range(0, tm, row_chunk):
            x = x_ref[r:r + row_chunk, :]
            a_scr[r:r + row_chunk, :] = (_rms(x, g) * sc1 + sh).astype(BF16)

    @pl.when(j < main_tiles)
    def _():
        o_ref[...] = jnp.dot(a_scr[...], wm_ref[...], preferred_element_type=F32).astype(o_ref.dtype)

    @pl.when(j == main_tiles)
    def _():
        acc = jnp.dot(a_scr[...], wt_ref[...], preferred_element_type=F32)
        o_ref[...] = acc.astype(o_ref.dtype)
        u_ref[...] = acc[:, :u_ref.shape[1]]


def input_projection(x, gain, mod4, layer, w_main, w_tail, *, tiles_per_seq, n_batch, tm=TM):
    m, d = x.shape
    tn = w_tail.shape[2]
    n = N_PROJ
    main_tiles = COL_U // tn
    assert COL_U % tn == 0 and n - COL_U == tn

    def mod_spec(col):
        return pl.BlockSpec((None, None, 1, d),
                            lambda i, j: (layer, jnp.minimum(i // tiles_per_seq, n_batch), 0, col))

    return pl.pallas_call(
        functools.partial(_in_proj_kernel, row_chunk=min(256, tm), main_tiles=main_tiles),
        grid=(m // tm, n // tn),
        in_specs=[
            pl.BlockSpec((tm, d), lambda i, j: (i, 0)),
            pl.BlockSpec((1, d), lambda i, j: (0, 0)),
            mod_spec(0),
            mod_spec(1),
            pl.BlockSpec((None, d, tn), lambda i, j: (layer, 0, jnp.minimum(j, main_tiles - 1))),
            pl.BlockSpec((None, d, tn), lambda i, j: (layer, 0, 0)),
        ],
        out_specs=[pl.BlockSpec((tm, tn), lambda i, j: (i, j)),
                   pl.BlockSpec((tm, D_SSM), lambda i, j: (i, 0))],
        out_shape=[jax.ShapeDtypeStruct((m, n), BF16), jax.ShapeDtypeStruct((m, D_SSM), F32)],
        scratch_shapes=[pltpu.VMEM((tm, d), BF16)],
        compiler_params=_cparams(("arbitrary", "arbitrary"), "in_proj"),
    )(x, gain.reshape(1, d), mod4, mod4, w_main, w_tail)


def _merge_kernel(yc_ref, ya_ref, ys_ref, gna_ref, w_ref, x_ref, gate_ref, g_ref, sh_ref, sc_ref,
                  xo_ref, ho_ref, *, row_chunk):
    tm = x_ref.shape[0]
    kc, ka = yc_ref.shape[1], ya_ref.shape[1]
    gna = gna_ref[...]
    gate = gate_ref[...]
    g = g_ref[...]
    sh = sh_ref[...]
    sc1 = 1.0 + sc_ref[...]
    for r in range(0, tm, row_chunk):
        rows = slice(r, r + row_chunk)
        acc = jnp.dot(yc_ref[rows, :], w_ref[0:kc, :], preferred_element_type=F32)
        ya = _rms(ya_ref[rows, :].astype(F32), gna).astype(BF16)
        acc = acc + jnp.dot(ya, w_ref[kc:kc + ka, :], preferred_element_type=F32)
        acc = acc + jnp.dot(ys_ref[rows, :], w_ref[kc + ka:, :], preferred_element_type=F32)
        x2 = x_ref[rows, :] + gate * acc
        xo_ref[rows, :] = x2
        ho_ref[rows, :] = (_rms(x2, g) * sc1 + sh).astype(ho_ref.dtype)


def merge_projection(y_conv, y_att, y_ssm, gn_att, w_o, x, gain2, mod4, layer, *, m_tiles, tiles_per_seq,
                     n_batch, tm):
    d = x.shape[1]
    rows = m_tiles * tm

    def mod_spec(col):
        return pl.BlockSpec((None, None, 1, d),
                            lambda i: (layer, jnp.minimum(i // tiles_per_seq, n_batch), 0, col))

    a_spec = lambda a: pl.BlockSpec((tm, a.shape[1]), lambda i: (i, 0))
    return pl.pallas_call(
        functools.partial(_merge_kernel, row_chunk=min(256, tm)),
        grid=(m_tiles,),
        in_specs=[
            a_spec(y_conv), a_spec(y_att), a_spec(y_ssm),
            pl.BlockSpec((1, y_att.shape[1]), lambda i: (0, 0)),
            pl.BlockSpec((None, d, d), lambda i: (layer, 0, 0)),
            pl.BlockSpec((tm, d), lambda i: (i, 0)),
            mod_spec(2),
            pl.BlockSpec((1, d), lambda i: (0, 0)),
            mod_spec(3),
            mod_spec(4),
        ],
        out_specs=[pl.BlockSpec((tm, d), lambda i: (i, 0)), pl.BlockSpec((tm, d), lambda i: (i, 0))],
        out_shape=[jax.ShapeDtypeStruct((rows, d), F32), jax.ShapeDtypeStruct((rows, d), BF16)],
        compiler_params=_cparams(("arbitrary",), "merge"),
    )(y_conv, y_att, y_ssm, gn_att.reshape(1, -1), w_o, x, mod4, gain2.reshape(1, d), mod4, mod4)


def _cast_rows(src_ref, dst_ref, row_chunk):
    for r in range(0, src_ref.shape[0], row_chunk):
        dst_ref[r:r + row_chunk, :] = src_ref[r:r + row_chunk, :].astype(dst_ref.dtype)


def _ffn_up_kernel(h_ref, wg_ref, wu_ref, o_ref, wg_scr, wu_scr):
    @pl.when(pl.program_id(1) == 0)
    def _():
        _cast_rows(wg_ref, wg_scr, CAST_ROWS)
        _cast_rows(wu_ref, wu_scr, CAST_ROWS)

    h = h_ref[...]
    gt = jnp.dot(h, wg_scr[...], preferred_element_type=F32)
    up = jnp.dot(h, wu_scr[...], preferred_element_type=F32)
    o_ref[...] = (gt * jax.nn.sigmoid(gt) * up).astype(o_ref.dtype)


def ffn_up(h, w_gate, w_up, layer, *, m_tiles, tm, tn):
    d = h.shape[1]
    f = w_gate.shape[2]
    w_spec = pl.BlockSpec((None, d, tn), lambda j, i: (layer, 0, j))
    return pl.pallas_call(
        _ffn_up_kernel,
        grid=(f // tn, m_tiles),
        in_specs=[pl.BlockSpec((tm, d), lambda j, i: (i, 0)), w_spec, w_spec],
        out_specs=pl.BlockSpec((tm, tn), lambda j, i: (i, j)),
        out_shape=jax.ShapeDtypeStruct((m_tiles * tm, f), BF16),
        scratch_shapes=[pltpu.VMEM((d, tn), BF16)] * 2,
        compiler_params=_cparams(("arbitrary", "arbitrary"), "ffn_up"),
    )(h, w_gate, w_up)


def _ffn_down_kernel(a_ref, w_ref, x_ref, gate_ref, o_ref, w_scr):
    @pl.when(pl.program_id(1) == 0)
    def _():
        _cast_rows(w_ref, w_scr, CAST_ROWS)

    acc = jnp.dot(a_ref[...], w_scr[...], preferred_element_type=F32)
    o_ref[...] = x_ref[...] + gate_ref[...] * acc


def ffn_down(a, w_down, x, mod4, layer, *, m_tiles, tiles_per_seq, n_batch, tm, tn):
    f = a.shape[1]
    d = x.shape[1]
    gate_blocks = d // tn
    return pl.pallas_call(
        _ffn_down_kernel,
        grid=(d // tn, m_tiles),
        in_specs=[
            pl.BlockSpec((tm, f), lambda j, i: (i, 0)),
            pl.BlockSpec((None, f, tn), lambda j, i: (layer, 0, j)),
            pl.BlockSpec((tm, tn), lambda j, i: (i, j)),
            pl.BlockSpec((None, None, 1, tn),
                         lambda j, i: (layer, jnp.minimum(i // tiles_per_seq, n_batch), 0,
                                       5 * gate_blocks + j)),
        ],
        out_specs=pl.BlockSpec((tm, tn), lambda j, i: (i, j)),
        out_shape=jax.ShapeDtypeStruct((m_tiles * tm, d), F32),
        scratch_shapes=[pltpu.VMEM((f, tn), BF16)],
        compiler_params=_cparams(("arbitrary", "arbitrary"), "ffn_down"),
    )(a, w_down, x, mod4)


HALO = 16


def _conv_kernel(h_ref, bg_ref, cg_ref, hp_ref, cp_ref, hn_ref, cn_ref, w_ref, gn_ref, o_ref, z_scr, *,
                 tiles_per_seq, lat_tiles):
    i = pl.program_id(0)
    t = h_ref.shape[0]
    whole_seq = i >= lat_tiles
    first = jnp.logical_or(i % tiles_per_seq == 0, whole_seq)
    last = jnp.logical_or(i % tiles_per_seq == tiles_per_seq - 1, whole_seq)
    z_prev = cp_ref[HALO - 1:HALO, :].astype(F32) * hp_ref[HALO - 1:HALO, :].astype(F32)
    z_next = cn_ref[0:1, :].astype(F32) * hn_ref[0:1, :].astype(F32)
    z_scr[7:8, :] = jnp.where(first, 0.0, z_prev)
    z_scr[8:8 + t, :] = cg_ref[...].astype(F32) * h_ref[...].astype(F32)
    z_scr[8 + t:9 + t, :] = jnp.where(last, 0.0, z_next)
    y = bg_ref[...].astype(F32) * (w_ref[0:1, :] * z_scr[7:7 + t, :] + w_ref[1:2, :] * z_scr[8:8 + t, :]
                                   + w_ref[2:3, :] * z_scr[9:9 + t, :])
    o_ref[...] = _rms(y, gn_ref[...]).astype(o_ref.dtype)


def conv_mixer(proj, conv_w, gn, *, rows, tile, tiles_per_seq, lat_tiles):
    ch = D_CONV
    n_tiles = rows // tile
    hb = tile // HALO
    last_blk = proj.shape[0] // HALO - 1
    main = lambda c: pl.BlockSpec((tile, ch), lambda i: (i, c))
    prev = lambda c: pl.BlockSpec((HALO, ch), lambda i: (jnp.maximum(i * hb - 1, 0), c))
    nxt = lambda c: pl.BlockSpec((HALO, ch), lambda i: (jnp.minimum((i + 1) * hb, last_blk), c))
    return pl.pallas_call(
        functools.partial(_conv_kernel, tiles_per_seq=tiles_per_seq, lat_tiles=lat_tiles),
        grid=(n_tiles,),
        in_specs=[main(0), main(1), main(2), prev(0), prev(2), nxt(0), nxt(2),
                  pl.BlockSpec((CONV_W, ch), lambda i: (0, 0)), pl.BlockSpec((1, ch), lambda i: (0, 0))],
        out_specs=pl.BlockSpec((tile, ch), lambda i: (i, 0)),
        out_shape=jax.ShapeDtypeStruct((rows, ch), BF16),
        scratch_shapes=[pltpu.VMEM((tile + 16, ch), F32)],
        compiler_params=_cparams(("arbitrary",), "conv"),
    )(proj, proj, proj, proj, proj, proj, proj, conv_w, gn.reshape(1, ch))


def _qkv_proj_kernel(cq_ref, ckv_ref, kr_ref, gq_ref, gkv_ref, wq_ref, wkv_ref, cos_ref, sin_ref,
                     q_ref, k_ref, v_ref):
    cos = cos_ref[...]
    sin = sin_ref[...]
    a = _rms(cq_ref[...].astype(F32), gq_ref[...]).astype(BF16)
    q = jnp.dot(a, wq_ref[...], preferred_element_type=F32)
    scale = MLA_SCALE * LOG2E
    for h in range(MLA_HEADS):
        c0 = h * QK_PAD
        q_ref[:, c0:c0 + QK_NOPE] = (q[:, c0:c0 + QK_NOPE] * scale).astype(q_ref.dtype)
        blk = q[:, c0 + QK_NOPE:c0 + QK_PAD]
        rot = blk * cos + pltpu.roll(blk, QK_ROPE, axis=1) * sin
        q_ref[:, c0 + QK_NOPE:c0 + QK_PAD] = (rot * scale).astype(q_ref.dtype)

    a = _rms(ckv_ref[...].astype(F32), gkv_ref[...]).astype(BF16)
    kv = jnp.dot(a, wkv_ref[...], preferred_element_type=F32)
    blk = kr_ref[...].astype(F32)
    rot = (blk * cos + pltpu.roll(blk, QK_ROPE, axis=1) * sin).astype(k_ref.dtype)
    nk = MLA_HEADS * QK_NOPE
    for h in range(MLA_HEADS):
        c0 = h * QK_PAD
        k_ref[:, c0:c0 + QK_NOPE] = kv[:, h * QK_NOPE:(h + 1) * QK_NOPE].astype(k_ref.dtype)
        k_ref[:, c0 + QK_NOPE:c0 + QK_PAD] = rot
    ones = jnp.ones((v_ref.shape[0], V_PAD - V_HEAD), v_ref.dtype)
    for h in range(MLA_HEADS):
        c0 = h * V_PAD
        v_ref[:, c0:c0 + V_HEAD] = kv[:, nk + h * V_HEAD:nk + (h + 1) * V_HEAD].astype(v_ref.dtype)
        v_ref[:, c0 + V_HEAD:c0 + V_PAD] = ones


def qkv_projection(proj, q_norm, kv_norm, w_q, w_kv, cos_t, sin_t, *, rows, tm, lat_tiles, tiles_per_seq):
    tab_map = lambda i: (jnp.where(i < lat_tiles, i % tiles_per_seq, tiles_per_seq), 0)
    nk = MLA_HEADS * QK_PAD
    nv = MLA_HEADS * V_PAD
    return pl.pallas_call(
        _qkv_proj_kernel,
        grid=(rows // tm,),
        in_specs=[
            pl.BlockSpec((tm, Q_RANK), lambda i: (i, COL_CQ // Q_RANK)),
            pl.BlockSpec((tm, KV_RANK), lambda i: (i, COL_CKV // KV_RANK)),
            pl.BlockSpec((tm, LANE), lambda i: (i, COL_KR // LANE)),
            pl.BlockSpec((1, Q_RANK), lambda i: (0, 0)),
            pl.BlockSpec((1, KV_RANK), lambda i: (0, 0)),
            pl.BlockSpec((Q_RANK, nk), lambda i: (0, 0)),
            pl.BlockSpec((KV_RANK, MLA_HEADS * (QK_NOPE + V_HEAD)), lambda i: (0, 0)),
            pl.BlockSpec((tm, LANE), tab_map),
            pl.BlockSpec((tm, LANE), tab_map),
        ],
        out_specs=[pl.BlockSpec((tm, nk), lambda i: (i, 0)), pl.BlockSpec((tm, nk), lambda i: (i, 0)),
                   pl.BlockSpec((tm, nv), lambda i: (i, 0))],
        out_shape=[jax.ShapeDtypeStruct((rows, nk), BF16), jax.ShapeDtypeStruct((rows, nk), BF16),
                   jax.ShapeDtypeStruct((rows, nv), BF16)],
        compiler_params=_cparams(("arbitrary",), "qkv"),
    )(proj, proj, proj, q_norm.reshape(1, Q_RANK), kv_norm.reshape(1, KV_RANK), w_q, w_kv, cos_t, sin_t)


_NT = (((1,), (1,)), ((), ()))


HEAD_SPLIT = 2


def _attn_kernel(q_ref, kl_ref, kc_ref, vl_ref, vc_ref, o_ref, *, lat_steps, steps_per_batch, ctx_len):
    s = pl.program_id(0)
    hpg = MLA_HEADS // HEAD_SPLIT

    def heads(q_rows, c_rows, with_latent):
        for h in range(hpg):
            q = q_ref[q_rows, h * QK_PAD:(h + 1) * QK_PAD]
            sc = lax.dot_general(q, kc_ref[c_rows, h * QK_PAD:(h + 1) * QK_PAD], _NT,
                                 preferred_element_type=F32)
            m = jnp.max(sc, axis=-1, keepdims=True)
            if with_latent:
                sl = lax.dot_general(q, kl_ref[:, h * QK_PAD:(h + 1) * QK_PAD], _NT,
                                     preferred_element_type=F32)
                m = jnp.maximum(m, jnp.max(sl, axis=-1, keepdims=True))
            o = jnp.dot(jnp.exp2(sc - m).astype(BF16), vc_ref[c_rows, h * V_PAD:(h + 1) * V_PAD],
                        preferred_element_type=F32)
            if with_latent:
                o = o + jnp.dot(jnp.exp2(sl - m).astype(BF16), vl_ref[:, h * V_PAD:(h + 1) * V_PAD],
                                preferred_element_type=F32)
            o_ref[q_rows, h * V_HEAD:(h + 1) * V_HEAD] = (o[:, :V_HEAD] / o[:, V_HEAD:]).astype(o_ref.dtype)

    @pl.when(s < lat_steps)
    def _():
        b = s // steps_per_batch
        c0 = pl.multiple_of((b % 2) * ctx_len, ctx_len)
        heads(slice(None), pl.ds(c0, ctx_len), True)

    @pl.when(s >= lat_steps)
    def _():
        for hb in range(2):
            rows = slice(hb * ctx_len, (hb + 1) * ctx_len)
            heads(rows, rows, False)


def latent_attention(q, k, v, *, n_batch, seq, ctx_len, tq, ctx_queries, rows_out):
    assert tq == 2 * ctx_len and n_batch % 2 == 0 and seq % tq == 0
    lat_tiles = seq // tq
    spb = HEAD_SPLIT * lat_tiles
    lat_steps = n_batch * spb
    ctx_steps = (n_batch // 2) * HEAD_SPLIT if ctx_queries else 0
    ctx_blk0 = n_batch * seq // tq
    nq = MLA_HEADS * QK_PAD // HEAD_SPLIT
    nv = MLA_HEADS * V_HEAD // HEAD_SPLIT
    nvp = MLA_HEADS * V_PAD // HEAD_SPLIT

    def split(s):
        lat = s < lat_steps
        cs = s - lat_steps
        b = jnp.where(lat, s // spb, n_batch - 1)
        hh = jnp.where(lat, (s // lat_tiles) % HEAD_SPLIT, cs % HEAD_SPLIT)
        q_blk = jnp.where(lat, b * lat_tiles + s % lat_tiles, ctx_blk0 + cs // HEAD_SPLIT)
        c_blk = jnp.where(lat, ctx_blk0 + b // 2, ctx_blk0 + cs // HEAD_SPLIT)
        return b, hh, q_blk, c_blk

    def lat_map(s):
        b, hh, _, _ = split(s)
        return (b, jnp.where(s < lat_steps, hh, HEAD_SPLIT - 1))

    q_map = lambda s: (split(s)[2], split(s)[1])
    c_map = lambda s: (split(s)[3], split(s)[1])
    return pl.pallas_call(
        functools.partial(_attn_kernel, lat_steps=lat_steps, steps_per_batch=spb, ctx_len=ctx_len),
        grid=(lat_steps + ctx_steps,),
        in_specs=[
            pl.BlockSpec((tq, nq), q_map),
            pl.BlockSpec((seq, nq), lat_map),
            pl.BlockSpec((tq, nq), c_map),
            pl.BlockSpec((seq, nvp), lat_map),
            pl.BlockSpec((tq, nvp), c_map),
        ],
        out_specs=pl.BlockSpec((tq, nv), q_map),
        out_shape=jax.ShapeDtypeStruct((rows_out, MLA_HEADS * V_HEAD), BF16),
        compiler_params=_cparams(("arbitrary",), "attention"),
    )(q, k, k, v, v)


def _s5_disc_kernel(are_ref, aim_ref, ldt_ref, abr_ref, abi_ref, cfr_ref, cfi_ref):
    ar = are_ref[...]
    ai = aim_ref[...]
    dt = jnp.exp(ldt_ref[...])
    mag = jnp.exp(ar * dt)
    th = ai * dt
    br = mag * jnp.cos(th)
    bi = mag * jnp.sin(th)
    nr = br - 1.0
    den = ar * ar + ai * ai
    abr_ref[...] = br
    abi_ref[...] = bi
    cfr_ref[...] = (nr * ar + bi * ai) / den
    cfi_ref[...] = (bi * ar - nr * ai) / den


def s5_discretise(a_re, a_im, log_dt):
    shp = a_re.shape
    rows = shp[0] * shp[1] * shp[2]
    flat = lambda t: t.reshape(rows, shp[3])
    ldt = jnp.broadcast_to(log_dt[..., None], shp)
    outs = pl.pallas_call(
        _s5_disc_kernel,
        out_shape=[jax.ShapeDtypeStruct((rows, shp[3]), F32)] * 4,
    )(flat(a_re), flat(a_im), flat(ldt))
    return [o.reshape(shp) for o in outs]


def _cmul(xr, xi, yr, yi):
    return xr * yr - xi * yi, xr * yi + xi * yr


def _s5_mats_kernel(abc_re, abc_im, cex_re, cex_im, afb_re, afb_im, cfb_re, cfb_im, bfb_re, bfb_im,
                    cpp_re, cpp_im, bx_ref, by_ref, ma_ref, mi_ref, mo_ref, a16_ref):
    tw = CHUNK * SSM_GROUP
    p = SSM_STATE
    lane = lax.broadcasted_iota(jnp.int32, (p, tw), 1)
    tblk = lane // SSM_GROUP
    lane16 = lax.broadcasted_iota(jnp.int32, (SSM_GROUP, tw), 1)
    lane128 = lax.broadcasted_iota(jnp.int32, (1, 2 * p), 1)
    fwd_lanes = lane128 < p
    sgn = jnp.where(lax.broadcasted_iota(jnp.int32, (SSM_GROUP, 2 * p), 1) < p, 1.0, -1.0)
    zeros = jnp.zeros((p, tw), F32)
    chan = lax.broadcasted_iota(jnp.int32, (SSM_GROUP, tw), 0)
    expand = jnp.where(lane16 % SSM_GROUP == chan, 1.0, 0.0)
    mi_blocks = []
    for d in range(2):
        ar = jnp.broadcast_to(abc_re[d], (p, tw))
        ai = jnp.broadcast_to(abc_im[d], (p, tw))
        sel_r = tblk if d == 0 else (CHUNK - 1) - tblk
        sq_re, sq_im = ar, ai
        r_re, r_im = jnp.ones((p, tw), F32), zeros
        for b in range((CHUNK - 1).bit_length()):
            bit = ((sel_r >> b) & 1) == 1
            n_re, n_im = _cmul(r_re, r_im, sq_re, sq_im)
            r_re = jnp.where(bit, n_re, r_re)
            r_im = jnp.where(bit, n_im, r_im)
            sq_re, sq_im = _cmul(sq_re, sq_im, sq_re, sq_im)
        e_re, e_im = _cmul(r_re, r_im, ar, ai)
        cr = jnp.dot(cex_re[d], expand, preferred_element_type=F32, precision=lax.Precision.HIGHEST)
        ci = jnp.dot(cex_im[d], expand, preferred_element_type=F32, precision=lax.Precision.HIGHEST)
        w_re, w_im = _cmul(cr, ci, e_re, e_im)
        base = d * 4 * p
        mdt = mo_ref.dtype
        zeros_m = zeros.astype(mdt)
        if d == 0:
            mo_ref[base:base + p, :] = w_re.astype(mdt)
            mo_ref[base + p:base + 2 * p, :] = zeros_m
            mo_ref[base + 2 * p:base + 3 * p, :] = (-w_im).astype(mdt)
            mo_ref[base + 3 * p:base + 4 * p, :] = zeros_m
        else:
            mo_ref[base:base + p, :] = zeros_m
            mo_ref[base + p:base + 2 * p, :] = w_re.astype(mdt)
            mo_ref[base + 2 * p:base + 3 * p, :] = zeros_m
            mo_ref[base + 3 * p:base + 4 * p, :] = (-w_im).astype(mdt)
        rr_re, rr_im = _cmul(cr, ci, r_re, r_im)
        stacked = jnp.concatenate([rr_re, rr_im], axis=0)
        lm = sgn * cpp_re[d] * bx_ref[d] - cpp_im[d] * by_ref[d]
        kall = jnp.dot(lm, stacked, preferred_element_type=F32, precision=lax.Precision.HIGHEST)
        for s in range(CHUNK):
            if d == 0:
                shift = SSM_GROUP * s
                keep = lane16 >= SSM_GROUP * s
            else:
                shift = (SSM_GROUP * (s + 1)) % tw
                keep = lane16 < SSM_GROUP * (s + 1)
            rolled = pltpu.roll(kall, shift, axis=1) if shift else kall
            blk = jnp.where(keep, rolled, 0.0)
            if d == 0:
                mi_blocks.append(blk)
            else:
                mi_ref[SSM_GROUP * s:SSM_GROUP * (s + 1), :] = (mi_blocks[s] + blk).astype(mi_ref.dtype)

    a_re = afb_re[...]
    a_im = afb_im[...]
    cf_re = cfb_re[...]
    cf_im = cfb_im[...]
    b_re = bfb_re[...]
    b_im = bfb_im[...]
    powers = []
    qr, qi = jnp.ones((1, 2 * p), F32), jnp.zeros((1, 2 * p), F32)
    for k in range(CHUNK + 1):
        powers.append((qr, qi))
        if k < CHUNK:
            qr, qi = _cmul(qr, qi, a_re, a_im)
    for s in range(CHUNK):
        g_re = jnp.where(fwd_lanes, powers[CHUNK - 1 - s][0], powers[s][0])
        g_im = jnp.where(fwd_lanes, powers[CHUNK - 1 - s][1], powers[s][1])
        g_re, g_im = _cmul(g_re, g_im, cf_re, cf_im)
        rows = slice(SSM_GROUP * s, SSM_GROUP * (s + 1))
        ma_ref[rows, 0:2 * p] = (g_re * b_re - g_im * b_im).astype(ma_ref.dtype)
        ma_ref[rows, 2 * p:4 * p] = (g_re * b_im + g_im * b_re).astype(ma_ref.dtype)
    a16_ref[0:1, :] = powers[CHUNK][0]
    a16_ref[1:2, :] = powers[CHUNK][1]


def s5_matrices(abar_re, abar_im, coef_re, coef_im, b_re, b_im, c_re, c_im):
    depth, _, g, p = abar_re.shape
    hg = b_re.shape[-1]
    tw = CHUNK * hg
    t0213 = lambda t: t.transpose(0, 2, 1, 3)
    abc = [t0213(t)[..., None] for t in (abar_re, abar_im)]
    cex = [t.transpose(0, 2, 1, 4, 3) for t in (c_re, c_im)]
    fb = lambda t: t0213(t).reshape(depth, g, 1, 2 * p)
    afb = [fb(t) for t in (abar_re, abar_im)]
    cfb = [fb(t) for t in (coef_re, coef_im)]
    bfb = [t.transpose(0, 2, 4, 1, 3).reshape(depth, g, hg, 2 * p) for t in (b_re, b_im)]
    cpp = [t0213(jnp.concatenate([t, t], axis=-1))[:, :, :, None, :] for t in (coef_re, coef_im)]
    bt_re = b_re.transpose(0, 2, 1, 4, 3)
    bt_im = b_im.transpose(0, 2, 1, 4, 3)
    bx = jnp.concatenate([bt_re, bt_im], axis=-1)
    by = jnp.concatenate([bt_im, bt_re], axis=-1)
    ins = [*abc, *cex, *afb, *cfb, *bfb, *cpp, bx, by]

    def spec(t):
        blk = (None, None) + t.shape[2:]
        nz = len(t.shape) - 2
        return pl.BlockSpec(blk, lambda l, gi: (l, gi) + (0,) * nz)

    def ospec(r, c):
        return pl.BlockSpec((None, None, r, c), lambda l, gi: (l, gi, 0, 0))

    return pl.pallas_call(
        _s5_mats_kernel,
        grid=(depth, g),
        in_specs=[spec(t) for t in ins],
        out_specs=[ospec(tw, 4 * p), ospec(tw, tw), ospec(8 * p, tw), ospec(2, 2 * p)],
        out_shape=[
            jax.ShapeDtypeStruct((depth, g, tw, 4 * p), BF16),
            jax.ShapeDtypeStruct((depth, g, tw, tw), BF16),
            jax.ShapeDtypeStruct((depth, g, 8 * p, tw), BF16),
            jax.ShapeDtypeStruct((depth, g, 2, 2 * p), F32),
        ],
        compiler_params=_cparams(("arbitrary", "arbitrary"), "s5_mats"),
    )(*ins)


GROUP_BLOCK = LANE // SSM_GROUP


def lane_swap_matrix():
    idx = jnp.arange(GROUP_BLOCK * LANE)
    a, b, c = idx // LANE, (idx // SSM_GROUP) % GROUP_BLOCK, idx % SSM_GROUP
    dst = b * LANE + a * SSM_GROUP + c
    return (dst[:, None] == idx[None, :]).astype(BF16)


def _s5_main_kernel(*refs, order_f, order_b, nb, lat_chunks, ctx_chunks):
    (u_ref, p_ref, ma_ref, mi_ref, mo_ref, a_ref, y_ref, ug_scr, yg_scr,
     xr_scr, xi_scr, fr_scr, fi_scr, br_scr, bi_scr) = refs
    p2 = 2 * SSM_STATE
    half = CHUNK // 2
    rows = u_ref.shape[0] // CHUNK
    n_sub = xr_scr.shape[0]

    for th in range(2):
        slab = jnp.concatenate(
            [u_ref[pl.ds(half * th + tl, rows, stride=CHUNK), :].astype(BF16) for tl in range(half)], axis=1)
        perm = jnp.dot(slab, p_ref[...], preferred_element_type=F32).astype(BF16)
        for g in range(GROUP_BLOCK):
            ug_scr[g, :, th * LANE:(th + 1) * LANE] = perm[:, g * LANE:(g + 1) * LANE]

    segments = ((0, lat_chunks), (nb * lat_chunks, ctx_chunks))

    def to_chunk_major(dst_ref, k, val):
        for r0, n in segments:
            for b in range(nb):
                dst_ref[k, pl.ds(r0 + b, n, stride=nb), :] = val[r0 + b * n:r0 + (b + 1) * n, :]

    def to_batch_major(src_ref, k):
        return jnp.concatenate([src_ref[k, pl.ds(r0 + b, n, stride=nb), :]
                                for r0, n in segments for b in range(nb)], axis=0)

    def rows_of(j):
        return slice(nb * j, nb * j + nb)

    fwd_lanes = lax.broadcasted_iota(jnp.int32, (nb, p2), 1) < SSM_STATE
    for g0 in range(0, GROUP_BLOCK, n_sub):
        for k in range(n_sub):
            x = jnp.dot(ug_scr[g0 + k], ma_ref[g0 + k], preferred_element_type=F32)
            to_chunk_major(xr_scr, k, x[:, 0:p2])
            to_chunk_major(xi_scr, k, x[:, p2:2 * p2])

        a_re = [a_ref[g0 + k, 0:1, :] for k in range(n_sub)]
        a_im = [a_ref[g0 + k, 1:2, :] for k in range(n_sub)]
        s_re = [jnp.zeros((nb, p2), F32) for _ in range(n_sub)]
        s_im = [jnp.zeros((nb, p2), F32) for _ in range(n_sub)]
        for jf, jb in zip(order_f, order_b):
            rf = rows_of(jf)
            rb = rows_of(jb)
            for k in range(n_sub):
                fr_scr[k, rf, :] = s_re[k]
                fi_scr[k, rf, :] = s_im[k]
                br_scr[k, rb, :] = s_re[k]
                bi_scr[k, rb, :] = s_im[k]
                xr = jnp.where(fwd_lanes, xr_scr[k, rf, :], xr_scr[k, rb, :])
                xi = jnp.where(fwd_lanes, xi_scr[k, rf, :], xi_scr[k, rb, :])
                s_re[k], s_im[k] = (a_re[k] * s_re[k] - a_im[k] * s_im[k] + xr,
                                    a_re[k] * s_im[k] + a_im[k] * s_re[k] + xi)

        for k in range(n_sub):
            g = g0 + k
            y = jnp.dot(ug_scr[g], mi_ref[g], preferred_element_type=F32)
            sp_f = jnp.concatenate([to_batch_major(fr_scr, k), to_batch_major(fi_scr, k)], axis=1).astype(BF16)
            sp_b = jnp.concatenate([to_batch_major(br_scr, k), to_batch_major(bi_scr, k)], axis=1).astype(BF16)
            y = y + jnp.dot(sp_f, mo_ref[g, 0:2 * p2, :], preferred_element_type=F32)
            y = y + jnp.dot(sp_b, mo_ref[g, 2 * p2:4 * p2, :], preferred_element_type=F32)
            yg_scr[g] = y.astype(BF16)

    for th in range(2):
        slab = jnp.concatenate([yg_scr[g, :, th * LANE:(th + 1) * LANE] for g in range(GROUP_BLOCK)], axis=1)
        perm = jnp.dot(slab, p_ref[...], preferred_element_type=F32)
        for tl in range(half):
            y_ref[pl.ds(half * th + tl, rows, stride=CHUNK), :] = perm[:, tl * LANE:(tl + 1) * LANE]


def s5_chunked(u, ma, mi, mo, a16, layer, *, lat_chunks, ctx_chunks, nb):
    r_all = u.shape[0]
    rows = r_all // CHUNK
    tw = CHUNK * SSM_GROUP
    j_tot = lat_chunks + ctx_chunks
    order_f = tuple(range(lat_chunks, j_tot)) + tuple(range(lat_chunks))
    order_b = tuple(range(j_tot - 1, lat_chunks - 1, -1)) + tuple(range(lat_chunks - 1, -1, -1))
    p4 = 4 * SSM_STATE
    n_gb = SSM_GROUPS // GROUP_BLOCK
    n_sub = GROUP_BLOCK // 2

    def wspec(r, c):
        return pl.BlockSpec((None, GROUP_BLOCK, r, c), lambda gb: (layer, gb, 0, 0))

    return pl.pallas_call(
        functools.partial(_s5_main_kernel, order_f=order_f, order_b=order_b, nb=nb,
                          lat_chunks=lat_chunks, ctx_chunks=ctx_chunks),
        grid=(n_gb,),
        in_specs=[
            pl.BlockSpec((r_all, LANE), lambda gb: (0, gb)),
            pl.BlockSpec((GROUP_BLOCK * LANE, GROUP_BLOCK * LANE), lambda gb: (0, 0)),
            wspec(tw, p4), wspec(tw, tw), wspec(2 * p4, tw), wspec(2, 2 * SSM_STATE),
        ],
        out_specs=pl.BlockSpec((r_all, LANE), lambda gb: (0, gb)),
        out_shape=jax.ShapeDtypeStruct((r_all, D_SSM), F32),
        scratch_shapes=[
            pltpu.VMEM((GROUP_BLOCK, rows, tw), BF16),
            pltpu.VMEM((GROUP_BLOCK, rows, tw), BF16),
        ] + [pltpu.VMEM((n_sub, rows, 2 * SSM_STATE), F32)] * 6,
        compiler_params=_cparams(("arbitrary",), "s5_main"),
    )(u, lane_swap_matrix(), ma, mi, mo, a16)


def _s5_out_kernel(y_ref, u_ref, d_ref, w_ref, b_ref, gn_ref, o_ref):
    y = y_ref[...].astype(F32) + d_ref[...] * u_ref[...].astype(F32)
    g = jax.nn.gelu(y)
    z = jnp.dot(g.astype(BF16), w_ref[...], preferred_element_type=F32) + b_ref[...]
    o_ref[...] = _rms(g * jax.nn.sigmoid(z), gn_ref[...]).astype(o_ref.dtype)


def s5_output(y, u, ssm_d, w_glu, b_glu, gn, *, rows, tm):
    ch = D_SSM
    vec = lambda: pl.BlockSpec((1, ch), lambda i: (0, 0))
    return pl.pallas_call(
        _s5_out_kernel,
        grid=(rows // tm,),
        in_specs=[
            pl.BlockSpec((tm, ch), lambda i: (i, 0)),
            pl.BlockSpec((tm, ch), lambda i: (i, 0)),
            vec(),
            pl.BlockSpec((ch, ch), lambda i: (0, 0)),
            vec(), vec(),
        ],
        out_specs=pl.BlockSpec((tm, ch), lambda i: (i, 0)),
        out_shape=jax.ShapeDtypeStruct((rows, ch), BF16),
        compiler_params=_cparams(("arbitrary",), "s5_out"),
    )(y, u, ssm_d.reshape(1, ch), w_glu, b_glu.reshape(1, ch), gn.reshape(1, ch))


def _final_norm_kernel(x_ref, g_ref, o_ref):
    o_ref[...] = _rms(x_ref[...], g_ref[...])


def final_rms_norm(x, g, *, rows, tm):
    d = x.shape[1]
    return pl.pallas_call(
        _final_norm_kernel,
        grid=(rows // tm,),
        in_specs=[pl.BlockSpec((tm, d), lambda i: (i, 0)), pl.BlockSpec((1, d), lambda i: (0, 0))],
        out_specs=pl.BlockSpec((tm, d), lambda i: (i, 0)),
        out_shape=jax.ShapeDtypeStruct((rows, d), F32),
        compiler_params=_cparams(("arbitrary",), "final_norm"),
    )(x, g.reshape(1, d))


def _rope_partner_perm():
    idx = []
    for i in range(QK_ROPE):
        idx.append(i + 16 if (i % 32) < 16 else i - 16)
    return jnp.asarray(idx, jnp.int32)


def rope_tables(seq, tab_tile):
    rows = seq // GRID_W
    row = jnp.repeat(jnp.arange(rows, dtype=F32), GRID_W)
    col = jnp.tile(jnp.arange(GRID_W, dtype=F32), rows)
    n_freq = QK_ROPE // 4
    inv = ROPE_BASE ** (-jnp.arange(n_freq, dtype=F32) / n_freq)
    ar = row[:, None] * inv
    ac = col[:, None] * inv
    cos = jnp.concatenate([jnp.cos(ar), jnp.cos(ar), jnp.cos(ac), jnp.cos(ac)], axis=1)
    sin = jnp.concatenate([-jnp.sin(ar), jnp.sin(ar), -jnp.sin(ac), jnp.sin(ac)], axis=1)
    cos = jnp.concatenate([cos, jnp.ones((tab_tile, QK_ROPE), F32)], axis=0)
    sin = jnp.concatenate([sin, jnp.zeros((tab_tile, QK_ROPE), F32)], axis=0)
    pad = jnp.zeros((seq + tab_tile, LANE - QK_ROPE), F32)
    return jnp.concatenate([cos, pad], axis=1), jnp.concatenate([sin, pad], axis=1)


def prep_w_in(w_in):
    assert COL_U == COL_CQ + Q_RANK
    s = [COL_U, COL_U + KV_RANK, COL_U + KV_RANK + QK_ROPE, COL_U + KV_RANK + QK_ROPE + D_SSM]
    ckv, kr, u = [w_in[..., s[i]:s[i + 1]] for i in range(3)]
    krp = kr[..., _rope_partner_perm()]
    pad = jnp.zeros(w_in.shape[:-1] + (N_PROJ - (COL_KR + LANE),), w_in.dtype)
    tail = jnp.concatenate([u, ckv, kr, krp, pad], axis=-1).astype(BF16)
    return w_in.astype(BF16), tail


def prep_w_uq(w_uq):
    r = w_uq.shape[0]
    w = w_uq.reshape(r, MLA_HEADS, QK_NOPE + QK_ROPE)
    rope = w[:, :, QK_NOPE:]
    w = jnp.concatenate([w, rope[:, :, _rope_partner_perm()]], axis=-1)
    return w.reshape(r, MLA_HEADS * QK_PAD).astype(BF16)


def prep_w_ukv(w_ukv):
    r = w_ukv.shape[0]
    w = w_ukv.reshape(r, MLA_HEADS, QK_NOPE + V_HEAD)
    kn = w[:, :, :QK_NOPE].reshape(r, MLA_HEADS * QK_NOPE)
    v = w[:, :, QK_NOPE:].reshape(r, MLA_HEADS * V_HEAD)
    return jnp.concatenate([kn, v], axis=1).astype(BF16)


def kernel(x, c, ctx, c_ctx, w_ada, b_ada, norm1_g, norm2_g, w_in, conv_w, mla_q_norm, w_uq, mla_kv_norm, w_ukv, ssm_a_re, ssm_a_im, ssm_log_dt, ssm_b_re, ssm_b_im, ssm_c_re, ssm_c_im, ssm_d, w_glu, b_glu, mix_norm, w_o, w_gate, w_up, w_down, final_norm):
    nb, seq, d = x.shape
    ctx_len = ctx.shape[1]
    depth = w_ada.shape[0]
    r_lat = nb * seq
    r_ctx = nb * ctx_len
    r_all = r_lat + r_ctx
    assert seq % TM == 0 and r_ctx == TM and seq % CHUNK == 0 and ctx_len % CHUNK == 0
    tiles_per_seq = seq // TM
    lat_tiles_m = r_lat // TM
    tp = TILE_NARROW_ROWS
    tf = TILE_MERGE_ROWS
    assert nb + 1 <= SUBLANE and seq % tf == 0 and r_ctx % tf == 0 and seq % tp == 0 and r_ctx % tp == 0

    xs = jnp.concatenate([x.reshape(r_lat, d), ctx.reshape(r_ctx, d)], axis=0)

    cvec = jnp.concatenate([c, c_ctx[None, :], jnp.zeros((SUBLANE - nb - 1, d), F32)], axis=0)
    mod = ada_modulation(cvec, w_ada, b_ada)
    mod4 = mod.reshape(depth, SUBLANE, 1, 6 * d)

    cos_t, sin_t = rope_tables(seq, tp)
    abar_re, abar_im, coef_re, coef_im = s5_discretise(ssm_a_re, ssm_a_im, ssm_log_dt)
    ma, mi, mo, a16 = s5_matrices(abar_re, abar_im, coef_re, coef_im,
                                  ssm_b_re, ssm_b_im, ssm_c_re, ssm_c_im)
    w_in_main, w_in_tail = prep_w_in(w_in)
    w_o_b = w_o.astype(BF16)
    lat_chunks = seq // CHUNK
    ctx_chunks = ctx_len // CHUNK

    for i in range(depth):
        ctx_out = i < depth - 1
        m_tiles = lat_tiles_m + (1 if ctx_out else 0)
        rows_out = r_all if ctx_out else r_lat

        proj, u32 = input_projection(xs, norm1_g[i], mod4, i, w_in_main, w_in_tail,
                                     tiles_per_seq=tiles_per_seq, n_batch=nb)

        y_conv = conv_mixer(proj, conv_w[i], mix_norm[i, :D_CONV], rows=rows_out, tile=ctx_len,
                            tiles_per_seq=seq // ctx_len, lat_tiles=r_lat // ctx_len)

        q, k, v = qkv_projection(proj, mla_q_norm[i], mla_kv_norm[i], prep_w_uq(w_uq[i]),
                                 prep_w_ukv(w_ukv[i]), cos_t, sin_t, rows=r_all, tm=tp,
                                 lat_tiles=r_lat // tp, tiles_per_seq=seq // tp)
        y_att = latent_attention(q, k, v, n_batch=nb, seq=seq, ctx_len=ctx_len, tq=2 * ctx_len,
                                 ctx_queries=ctx_out, rows_out=rows_out)

        y_tok = s5_chunked(u32, ma, mi, mo, a16, i, lat_chunks=lat_chunks, ctx_chunks=ctx_chunks, nb=nb)
        y_ssm = s5_output(y_tok, u32, ssm_d[i], w_glu[i].astype(BF16), b_glu[i],
                          mix_norm[i, D_CONV + D_ATTN:], rows=rows_out, tm=tp)

        x2, h2 = merge_projection(y_conv, y_att, y_ssm, mix_norm[i, D_CONV:D_CONV + D_ATTN], w_o_b, xs,
                                  norm2_g[i], mod4, i,
                                  m_tiles=rows_out // tf, tiles_per_seq=seq // tf, n_batch=nb, tm=tf)
        hidden = ffn_up(h2, w_gate, w_up, i, m_tiles=m_tiles, tm=TM, tn=TILE_FFN_COLS)
        xs = ffn_down(hidden, w_down, x2, mod4, i, m_tiles=rows_out // tf, tiles_per_seq=seq // tf,
                      n_batch=nb, tm=tf, tn=TILE_FFN_COLS)

    out = final_rms_norm(xs, final_norm, rows=r_lat, tm=tp)
    return out.reshape(nb, seq, d)
```

```python
import functools
import math

import jax
import jax.numpy as jnp
from jax import lax
from jax.experimental import pallas as pl
from jax.experimental.pallas import tpu as pltpu

F32 = jnp.float32
BF16 = jnp.bfloat16

EPS = 1e-6
GRID_W = 64
CONV_W = 3
D_CONV = 512
D_SSM = 512
D_ATTN = 1024
MLA_HEADS = 8
QK_NOPE = 128
QK_ROPE = 64
V_HEAD = 128
Q_RANK = 512
KV_RANK = 256
ROPE_BASE = 10000.0
MLA_SCALE = (QK_NOPE + QK_ROPE) ** -0.5
SSM_GROUP = 16
SSM_GROUPS = 32
SSM_STATE = 64
CHUNK = 16

QK_PAD = 256
V_PAD = 256
LOG2E = math.log2(math.e)
LANE = 128
SUBLANE = 8
MIB = 1024 * 1024

TM = 1024
TILE_MERGE_ROWS = 512
TILE_NARROW_ROWS = 512
TILE_FFN_COLS = 512
CAST_ROWS = 512

VMEM_MIB = dict(ada=40, in_proj=52, merge=52, ffn_up=56, ffn_down=56, conv=32, qkv=40, attention=48,
                s5_mats=32, s5_main=56, s5_out=32, final_norm=32)

COL_CQ = 3 * D_CONV
COL_U = COL_CQ + Q_RANK
COL_CKV = COL_U + D_SSM
COL_KR = COL_CKV + KV_RANK
N_PROJ = 3072


def _cparams(sem, call):
    return pltpu.CompilerParams(dimension_semantics=sem, vmem_limit_bytes=VMEM_MIB[call] * MIB)


def _rms(x, g):
    return x * lax.rsqrt(jnp.mean(x * x, axis=-1, keepdims=True) + EPS) * g


def _ada_kernel(c_ref, w_ref, b_ref, o_ref):
    cv = c_ref[...]
    s = (cv * jax.nn.sigmoid(cv)).astype(BF16)
    o_ref[...] = jnp.dot(s, w_ref[...].astype(BF16), preferred_element_type=F32) + b_ref[...]


def ada_modulation(cvec, w_ada, b_ada, tn=1024):
    depth, d, n = w_ada.shape
    rows = cvec.shape[0]
    return pl.pallas_call(
        _ada_kernel,
        grid=(depth, n // tn),
        in_specs=[
            pl.BlockSpec((rows, d), lambda l, j: (0, 0)),
            pl.BlockSpec((None, d, tn), lambda l, j: (l, 0, j)),
            pl.BlockSpec((None, 1, tn), lambda l, j: (l, 0, j)),
        ],
        out_specs=pl.BlockSpec((None, rows, tn), lambda l, j: (l, 0, j)),
        out_shape=jax.ShapeDtypeStruct((depth, rows, n), F32),
        compiler_params=_cparams(("arbitrary", "arbitrary"), "ada"),
    )(cvec, w_ada, b_ada.reshape(depth, 1, n))


def _in_proj_kernel(x_ref, g_ref, sh_ref, sc_ref, wm_ref, wt_ref, o_ref, u_ref, a_scr, *, row_chunk,
                    main_tiles):
    tm = x_ref.shape[0]
    j = pl.program_id(1)

    @pl.when(j == 0)
    def _():
        g = g_ref[...]
        sh = sh_ref[...]
        sc1 = 1.0 + sc_ref[...]
        for r in range(0, tm, row_chunk):
            x = x_ref[r:r + row_chunk, :]
            a_scr[r:r + row_chunk, :] = (_rms(x, g) * sc1 + sh).astype(BF16)

    @pl.when(j < main_tiles)
    def _():
        o_ref[...] = jnp.dot(a_scr[...], wm_ref[...], preferred_element_type=F32).astype(o_ref.dtype)

    @pl.when(j == main_tiles)
    def _():
        acc = jnp.dot(a_scr[...], wt_ref[...], preferred_element_type=F32)
        o_ref[...] = acc.astype(o_ref.dtype)
        u_ref[...] = acc[:, :u_ref.shape[1]]


def input_projection(x, gain, mod4, layer, w_main, w_tail, *, tiles_per_seq, n_batch, tm=TM):
    m, d = x.shape
    tn = w_tail.shape[2]
    n = N_PROJ
    main_tiles = COL_U // tn
    assert COL_U % tn == 0 and n - COL_U == tn

    def mod_spec(col):
        return pl.BlockSpec((None, None, 1, d),
                            lambda i, j: (layer, jnp.minimum(i // tiles_per_seq, n_batch), 0, col))

    return pl.pallas_call(
        functools.partial(_in_proj_kernel, row_chunk=min(256, tm), main_tiles=main_tiles),
        grid=(m // tm, n // tn),
        in_specs=[
            pl.BlockSpec((tm, d), lambda i, j: (i, 0)),
            pl.BlockSpec((1, d), lambda i, j: (0, 0)),
            mod_spec(0),
            mod_spec(1),
            pl.BlockSpec((None, d, tn), lambda i, j: (layer, 0, jnp.minimum(j, main_tiles - 1))),
            pl.BlockSpec((None, d, tn), lambda i, j: (layer, 0, 0)),
        ],
        out_specs=[pl.BlockSpec((tm, tn), lambda i, j: (i, j)),
                   pl.BlockSpec((tm, D_SSM), lambda i, j: (i, 0))],
        out_shape=[jax.ShapeDtypeStruct((m, n), BF16), jax.ShapeDtypeStruct((m, D_SSM), F32)],
        scratch_shapes=[pltpu.VMEM((tm, d), BF16)],
        compiler_params=_cparams(("arbitrary", "arbitrary"), "in_proj"),
    )(x, gain.reshape(1, d), mod4, mod4, w_main, w_tail)


def _merge_kernel(yc_ref, ya_ref, ys_ref, gna_ref, w_ref, x_ref, gate_ref, g_ref, sh_ref, sc_ref,
                  xo_ref, ho_ref, *, row_chunk):
    tm = x_ref.shape[0]
    kc, ka = yc_ref.shape[1], ya_ref.shape[1]
    gna = gna_ref[...]
    gate = gate_ref[...]
    g = g_ref[...]
    sh = sh_ref[...]
    sc1 = 1.0 + sc_ref[...]
    for r in range(0, tm, row_chunk):
        rows = slice(r, r + row_chunk)
        acc = jnp.dot(yc_ref[rows, :], w_ref[0:kc, :], preferred_element_type=F32)
        ya = _rms(ya_ref[rows, :].astype(F32), gna).astype(BF16)
        acc = acc + jnp.dot(ya, w_ref[kc:kc + ka, :], preferred_element_type=F32)
        acc = acc + jnp.dot(ys_ref[rows, :], w_ref[kc + ka:, :], preferred_element_type=F32)
        x2 = x_ref[rows, :] + gate * acc
        xo_ref[rows, :] = x2
        ho_ref[rows, :] = (_rms(x2, g) * sc1 + sh).astype(ho_ref.dtype)


def merge_projection(y_conv, y_att, y_ssm, gn_att, w_o, x, gain2, mod4, layer, *, m_tiles, tiles_per_seq,
                     n_batch, tm):
    d = x.shape[1]
    rows = m_tiles * tm

    def mod_spec(col):
        return pl.BlockSpec((None, None, 1, d),
                            lambda i: (layer, jnp.minimum(i // tiles_per_seq, n_batch), 0, col))

    a_spec = lambda a: pl.BlockSpec((tm, a.shape[1]), lambda i: (i, 0))
    return pl.pallas_call(
        functools.partial(_merge_kernel, row_chunk=min(256, tm)),
        grid=(m_tiles,),
        in_specs=[
            a_spec(y_conv), a_spec(y_att), a_spec(y_ssm),
            pl.BlockSpec((1, y_att.shape[1]), lambda i: (0, 0)),
            pl.BlockSpec((None, d, d), lambda i: (layer, 0, 0)),
            pl.BlockSpec((tm, d), lambda i: (i, 0)),
            mod_spec(2),
            pl.BlockSpec((1, d), lambda i: (0, 0)),
            mod_spec(3),
            mod_spec(4),
        ],
        out_specs=[pl.BlockSpec((tm, d), lambda i: (i, 0)), pl.BlockSpec((tm, d), lambda i: (i, 0))],
        out_shape=[jax.ShapeDtypeStruct((rows, d), F32), jax.ShapeDtypeStruct((rows, d), BF16)],
        compiler_params=_cparams(("arbitrary",), "merge"),
    )(y_conv, y_att, y_ssm, gn_att.reshape(1, -1), w_o, x, mod4, gain2.reshape(1, d), mod4, mod4)


def _cast_rows(src_ref, dst_ref, row_chunk):
    for r in range(0, src_ref.shape[0], row_chunk):
        dst_ref[r:r + row_chunk, :] = src_ref[r:r + row_chunk, :].astype(dst_ref.dtype)


def _ffn_up_kernel(h_ref, wg_ref, wu_ref, o_ref, wg_scr, wu_scr):
    @pl.when(pl.program_id(1) == 0)
    def _():
        _cast_rows(wg_ref, wg_scr, CAST_ROWS)
        _cast_rows(wu_ref, wu_scr, CAST_ROWS)

    h = h_ref[...]
    gt = jnp.dot(h, wg_scr[...], preferred_element_type=F32)
    up = jnp.dot(h, wu_scr[...], preferred_element_type=F32)
    o_ref[...] = (gt * jax.nn.sigmoid(gt) * up).astype(o_ref.dtype)


def ffn_up(h, w_gate, w_up, layer, *, m_tiles, tm, tn):
    d = h.shape[1]
    f = w_gate.shape[2]
    w_spec = pl.BlockSpec((None, d, tn), lambda j, i: (layer, 0, j))
    return pl.pallas_call(
        _ffn_up_kernel,
        grid=(f // tn, m_tiles),
        in_specs=[pl.BlockSpec((tm, d), lambda j, i: (i, 0)), w_spec, w_spec],
        out_specs=pl.BlockSpec((tm, tn), lambda j, i: (i, j)),
        out_shape=jax.ShapeDtypeStruct((m_tiles * tm, f), BF16),
        scratch_shapes=[pltpu.VMEM((d, tn), BF16)] * 2,
        compiler_params=_cparams(("arbitrary", "arbitrary"), "ffn_up"),
    )(h, w_gate, w_up)


def _ffn_down_kernel(a_ref, w_ref, x_ref, gate_ref, o_ref, w_scr):
    @pl.when(pl.program_id(1) == 0)
    def _():
        _cast_rows(w_ref, w_scr, CAST_ROWS)

    acc = jnp.dot(a_ref[...], w_scr[...], preferred_element_type=F32)
    o_ref[...] = x_ref[...] + gate_ref[...] * acc


def ffn_down(a, w_down, x, mod4, layer, *, m_tiles, tiles_per_seq, n_batch, tm, tn):
    f = a.shape[1]
    d = x.shape[1]
    gate_blocks = d // tn
    return pl.pallas_call(
        _ffn_down_kernel,
        grid=(d // tn, m_tiles),
        in_specs=[
            pl.BlockSpec((tm, f), lambda j, i: (i, 0)),
            pl.BlockSpec((None, f, tn), lambda j, i: (layer, 0, j)),
            pl.BlockSpec((tm, tn), lambda j, i: (i, j)),
            pl.BlockSpec((None, None, 1, tn),
                         lambda j, i: (layer, jnp.minimum(i // tiles_per_seq, n_batch), 0,
                                       5 * gate_blocks + j)),
        ],
        out_specs=pl.BlockSpec((tm, tn), lambda j, i: (i, j)),
        out_shape=jax.ShapeDtypeStruct((m_tiles * tm, d), F32),
        scratch_shapes=[pltpu.VMEM((f, tn), BF16)],
        compiler_params=_cparams(("arbitrary", "arbitrary"), "ffn_down"),
    )(a, w_down, x, mod4)


HALO = 16


def _conv_kernel(h_ref, bg_ref, cg_ref, hp_ref, cp_ref, hn_ref, cn_ref, w_ref, gn_ref, o_ref, z_scr, *,
                 tiles_per_seq, lat_tiles):
    i = pl.program_id(0)
    t = h_ref.shape[0]
    whole_seq = i >= lat_tiles
    first = jnp.logical_or(i % tiles_per_seq == 0, whole_seq)
    last = jnp.logical_or(i % tiles_per_seq == tiles_per_seq - 1, whole_seq)
    z_prev = cp_ref[HALO - 1:HALO, :].astype(F32) * hp_ref[HALO - 1:HALO, :].astype(F32)
    z_next = cn_ref[0:1, :].astype(F32) * hn_ref[0:1, :].astype(F32)
    z_scr[7:8, :] = jnp.where(first, 0.0, z_prev)
    z_scr[8:8 + t, :] = cg_ref[...].astype(F32) * h_ref[...].astype(F32)
    z_scr[8 + t:9 + t, :] = jnp.where(last, 0.0, z_next)
    y = bg_ref[...].astype(F32) * (w_ref[0:1, :] * z_scr[7:7 + t, :] + w_ref[1:2, :] * z_scr[8:8 + t, :]
                                   + w_ref[2:3, :] * z_scr[9:9 + t, :])
    o_ref[...] = _rms(y, gn_ref[...]).astype(o_ref.dtype)


def conv_mixer(proj, conv_w, gn, *, rows, tile, tiles_per_seq, lat_tiles):
    ch = D_CONV
    n_tiles = rows // tile
    hb = tile // HALO
    last_blk = proj.shape[0] // HALO - 1
    main = lambda c: pl.BlockSpec((tile, ch), lambda i: (i, c))
    prev = lambda c: pl.BlockSpec((HALO, ch), lambda i: (jnp.maximum(i * hb - 1, 0), c))
    nxt = lambda c: pl.BlockSpec((HALO, ch), lambda i: (jnp.minimum((i + 1) * hb, last_blk), c))
    return pl.pallas_call(
        functools.partial(_conv_kernel, tiles_per_seq=tiles_per_seq, lat_tiles=lat_tiles),
        grid=(n_tiles,),
        in_specs=[main(0), main(1), main(2), prev(0), prev(2), nxt(0), nxt(2),
                  pl.BlockSpec((CONV_W, ch), lambda i: (0, 0)), pl.BlockSpec((1, ch), lambda i: (0, 0))],
        out_specs=pl.BlockSpec((tile, ch), lambda i: (i, 0)),
        out_shape=jax.ShapeDtypeStruct((rows, ch), BF16),
        scratch_shapes=[pltpu.VMEM((tile + 16, ch), F32)],
        compiler_params=_cparams(("arbitrary",), "conv"),
    )(proj, proj, proj, proj, proj, proj, proj, conv_w, gn.reshape(1, ch))


def _qkv_proj_kernel(cq_ref, ckv_ref, kr_ref, gq_ref, gkv_ref, wq_ref, wkv_ref, cos_ref, sin_ref,
                     q_ref, k_ref, v_ref):
    cos = cos_ref[...]
    sin = sin_ref[...]
    a = _rms(cq_ref[...].astype(F32), gq_ref[...]).astype(BF16)
    q = jnp.dot(a, wq_ref[...], preferred_element_type=F32)
    scale = MLA_SCALE * LOG2E
    for h in range(MLA_HEADS):
        c0 = h * QK_PAD
        q_ref[:, c0:c0 + QK_NOPE] = (q[:, c0:c0 + QK_NOPE] * scale).astype(q_ref.dtype)
        blk = q[:, c0 + QK_NOPE:c0 + QK_PAD]
        rot = blk * cos + pltpu.roll(blk, QK_ROPE, axis=1) * sin
        q_ref[:, c0 + QK_NOPE:c0 + QK_PAD] = (rot * scale).astype(q_ref.dtype)

    a = _rms(ckv_ref[...].astype(F32), gkv_ref[...]).astype(BF16)
    kv = jnp.dot(a, wkv_ref[...], preferred_element_type=F32)
    blk = kr_ref[...].astype(F32)
    rot = (blk * cos + pltpu.roll(blk, QK_ROPE, axis=1) * sin).astype(k_ref.dtype)
    nk = MLA_HEADS * QK_NOPE
    for h in range(MLA_HEADS):
        c0 = h * QK_PAD
        k_ref[:, c0:c0 + QK_NOPE] = kv[:, h * QK_NOPE:(h + 1) * QK_NOPE].astype(k_ref.dtype)
        k_ref[:, c0 + QK_NOPE:c0 + QK_PAD] = rot
    ones = jnp.ones((v_ref.shape[0], V_PAD - V_HEAD), v_ref.dtype)
    for h in range(MLA_HEADS):
        c0 = h * V_PAD
        v_ref[:, c0:c0 + V_HEAD] = kv[:, nk + h * V_HEAD:nk + (h + 1) * V_HEAD].astype(v_ref.dtype)
        v_ref[:, c0 + V_HEAD:c0 + V_PAD] = ones


def qkv_projection(proj, q_norm, kv_norm, w_q, w_kv, cos_t, sin_t, *, rows, tm, lat_tiles, tiles_per_seq):
    tab_map = lambda i: (jnp.where(i < lat_tiles, i % tiles_per_seq, tiles_per_seq), 0)
    nk = MLA_HEADS * QK_PAD
    nv = MLA_HEADS * V_PAD
    return pl.pallas_call(
        _qkv_proj_kernel,
        grid=(rows // tm,),
        in_specs=[
            pl.BlockSpec((tm, Q_RANK), lambda i: (i, COL_CQ // Q_RANK)),
            pl.BlockSpec((tm, KV_RANK), lambda i: (i, COL_CKV // KV_RANK)),
            pl.BlockSpec((tm, LANE), lambda i: (i, COL_KR // LANE)),
            pl.BlockSpec((1, Q_RANK), lambda i: (0, 0)),
            pl.BlockSpec((1, KV_RANK), lambda i: (0, 0)),
            pl.BlockSpec((Q_RANK, nk), lambda i: (0, 0)),
            pl.BlockSpec((KV_RANK, MLA_HEADS * (QK_NOPE + V_HEAD)), lambda i: (0, 0)),
            pl.BlockSpec((tm, LANE), tab_map),
            pl.BlockSpec((tm, LANE), tab_map),
        ],
        out_specs=[pl.BlockSpec((tm, nk), lambda i: (i, 0)), pl.BlockSpec((tm, nk), lambda i: (i, 0)),
                   pl.BlockSpec((tm, nv), lambda i: (i, 0))],
        out_shape=[jax.ShapeDtypeStruct((rows, nk), BF16), jax.ShapeDtypeStruct((rows, nk), BF16),
                   jax.ShapeDtypeStruct((rows, nv), BF16)],
        compiler_params=_cparams(("arbitrary",), "qkv"),
    )(proj, proj, proj, q_norm.reshape(1, Q_RANK), kv_norm.reshape(1, KV_RANK), w_q, w_kv, cos_t, sin_t)


_NT = (((1,), (1,)), ((), ()))


HEAD_SPLIT = 2
KEY_CHUNK = 1024


def _attn_kernel(q_ref, kl_ref, kc_ref, vl_ref, vc_ref, o_ref, *, lat_steps, steps_per_batch, ctx_len):
    s = pl.program_id(0)
    hpg = MLA_HEADS // HEAD_SPLIT

    def heads(q_rows, c_rows, with_latent):
        for h in range(hpg):
            q = q_ref[q_rows, h * QK_PAD:(h + 1) * QK_PAD]
            sc = lax.dot_general(q, kc_ref[c_rows, h * QK_PAD:(h + 1) * QK_PAD], _NT,
                                 preferred_element_type=F32)
            m = jnp.max(sc, axis=-1, keepdims=True)
            o = jnp.dot(jnp.exp2(sc - m).astype(BF16), vc_ref[c_rows, h * V_PAD:(h + 1) * V_PAD],
                        preferred_element_type=F32)
            if with_latent:
                for k0 in range(0, kl_ref.shape[0], KEY_CHUNK):
                    keys = slice(k0, k0 + KEY_CHUNK)
                    sl = lax.dot_general(q, kl_ref[keys, h * QK_PAD:(h + 1) * QK_PAD], _NT,
                                         preferred_element_type=F32)
                    m_new = jnp.maximum(m, jnp.max(sl, axis=-1, keepdims=True))
                    o = o * jnp.exp2(m - m_new) + jnp.dot(
                        jnp.exp2(sl - m_new).astype(BF16), vl_ref[keys, h * V_PAD:(h + 1) * V_PAD],
                        preferred_element_type=F32)
                    m = m_new
            o_ref[q_rows, h * V_HEAD:(h + 1) * V_HEAD] = (o[:, :V_HEAD] / o[:, V_HEAD:]).astype(o_ref.dtype)

    @pl.when(s < lat_steps)
    def _():
        b = s // steps_per_batch
        c0 = pl.multiple_of((b % 2) * ctx_len, ctx_len)
        heads(slice(None), pl.ds(c0, ctx_len), True)

    @pl.when(s >= lat_steps)
    def _():
        for hb in range(2):
            rows = slice(hb * ctx_len, (hb + 1) * ctx_len)
            heads(rows, rows, False)


def latent_attention(q, k, v, *, n_batch, seq, ctx_len, tq, ctx_queries, rows_out):
    assert tq == 2 * ctx_len and n_batch % 2 == 0 and seq % tq == 0
    lat_tiles = seq // tq
    spb = HEAD_SPLIT * lat_tiles
    lat_steps = n_batch * spb
    ctx_steps = (n_batch // 2) * HEAD_SPLIT if ctx_queries else 0
    ctx_blk0 = n_batch * seq // tq
    nq = MLA_HEADS * QK_PAD // HEAD_SPLIT
    nv = MLA_HEADS * V_HEAD // HEAD_SPLIT
    nvp = MLA_HEADS * V_PAD // HEAD_SPLIT

    def split(s):
        lat = s < lat_steps
        cs = s - lat_steps
        b = jnp.where(lat, s // spb, n_batch - 1)
        hh = jnp.where(lat, (s // lat_tiles) % HEAD_SPLIT, cs % HEAD_SPLIT)
        q_blk = jnp.where(lat, b * lat_tiles + s % lat_tiles, ctx_blk0 + cs // HEAD_SPLIT)
        c_blk = jnp.where(lat, ctx_blk0 + b // 2, ctx_blk0 + cs // HEAD_SPLIT)
        return b, hh, q_blk, c_blk

    def lat_map(s):
        b, hh, _, _ = split(s)
        return (b, jnp.where(s < lat_steps, hh, HEAD_SPLIT - 1))

    q_map = lambda s: (split(s)[2], split(s)[1])
    c_map = lambda s: (split(s)[3], split(s)[1])
    return pl.pallas_call(
        functools.partial(_attn_kernel, lat_steps=lat_steps, steps_per_batch=spb, ctx_len=ctx_len),
        grid=(lat_steps + ctx_steps,),
        in_specs=[
            pl.BlockSpec((tq, nq), q_map),
            pl.BlockSpec((seq, nq), lat_map),
            pl.BlockSpec((tq, nq), c_map),
            pl.BlockSpec((seq, nvp), lat_map),
            pl.BlockSpec((tq, nvp), c_map),
        ],
        out_specs=pl.BlockSpec((tq, nv), q_map),
        out_shape=jax.ShapeDtypeStruct((rows_out, MLA_HEADS * V_HEAD), BF16),
        compiler_params=_cparams(("arbitrary",), "attention"),
    )(q, k, k, v, v)


def _s5_disc_kernel(are_ref, aim_ref, ldt_ref, abr_ref, abi_ref, cfr_ref, cfi_ref):
    ar = are_ref[...]
    ai = aim_ref[...]
    dt = jnp.exp(ldt_ref[...])
    mag = jnp.exp(ar * dt)
    th = ai * dt
    br = mag * jnp.cos(th)
    bi = mag * jnp.sin(th)
    nr = br - 1.0
    den = ar * ar + ai * ai
    abr_ref[...] = br
    abi_ref[...] = bi
    cfr_ref[...] = (nr * ar + bi * ai) / den
    cfi_ref[...] = (bi * ar - nr * ai) / den


def s5_discretise(a_re, a_im, log_dt):
    shp = a_re.shape
    rows = shp[0] * shp[1] * shp[2]
    flat = lambda t: t.reshape(rows, shp[3])
    ldt = jnp.broadcast_to(log_dt[..., None], shp)
    outs = pl.pallas_call(
        _s5_disc_kernel,
        out_shape=[jax.ShapeDtypeStruct((rows, shp[3]), F32)] * 4,
    )(flat(a_re), flat(a_im), flat(ldt))
    return [o.reshape(shp) for o in outs]


def _cmul(xr, xi, yr, yi):
    return xr * yr - xi * yi, xr * yi + xi * yr


def _s5_mats_kernel(abc_re, abc_im, cex_re, cex_im, afb_re, afb_im, cfb_re, cfb_im, bfb_re, bfb_im,
                    cpp_re, cpp_im, bx_ref, by_ref, ma_ref, mi_ref, mo_ref, a16_ref):
    tw = CHUNK * SSM_GROUP
    p = SSM_STATE
    lane = lax.broadcasted_iota(jnp.int32, (p, tw), 1)
    tblk = lane // SSM_GROUP
    lane16 = lax.broadcasted_iota(jnp.int32, (SSM_GROUP, tw), 1)
    lane128 = lax.broadcasted_iota(jnp.int32, (1, 2 * p), 1)
    fwd_lanes = lane128 < p
    sgn = jnp.where(lax.broadcasted_iota(jnp.int32, (SSM_GROUP, 2 * p), 1) < p, 1.0, -1.0)
    zeros = jnp.zeros((p, tw), F32)
    chan = lax.broadcasted_iota(jnp.int32, (SSM_GROUP, tw), 0)
    expand = jnp.where(lane16 % SSM_GROUP == chan, 1.0, 0.0)
    mi_blocks = []
    for d in range(2):
        ar = jnp.broadcast_to(abc_re[d], (p, tw))
        ai = jnp.broadcast_to(abc_im[d], (p, tw))
        sel_r = tblk if d == 0 else (CHUNK - 1) - tblk
        sq_re, sq_im = ar, ai
        r_re, r_im = jnp.ones((p, tw), F32), zeros
        for b in range((CHUNK - 1).bit_length()):
            bit = ((sel_r >> b) & 1) == 1
            n_re, n_im = _cmul(r_re, r_im, sq_re, sq_im)
            r_re = jnp.where(bit, n_re, r_re)
            r_im = jnp.where(bit, n_im, r_im)
            sq_re, sq_im = _cmul(sq_re, sq_im, sq_re, sq_im)
        e_re, e_im = _cmul(r_re, r_im, ar, ai)
        cr = jnp.dot(cex_re[d], expand, preferred_element_type=F32, precision=lax.Precision.HIGHEST)
        ci = jnp.dot(cex_im[d], expand, preferred_element_type=F32, precision=lax.Precision.HIGHEST)
        w_re, w_im = _cmul(cr, ci, e_re, e_im)
        base = d * 4 * p
        mdt = mo_ref.dtype
        zeros_m = zeros.astype(mdt)
        if d == 0:
            mo_ref[base:base + p, :] = w_re.astype(mdt)
            mo_ref[base + p:base + 2 * p, :] = zeros_m
            mo_ref[base + 2 * p:base + 3 * p, :] = (-w_im).astype(mdt)
            mo_ref[base + 3 * p:base + 4 * p, :] = zeros_m
        else:
            mo_ref[base:base + p, :] = zeros_m
            mo_ref[base + p:base + 2 * p, :] = w_re.astype(mdt)
            mo_ref[base + 2 * p:base + 3 * p, :] = zeros_m
            mo_ref[base + 3 * p:base + 4 * p, :] = (-w_im).astype(mdt)
        rr_re, rr_im = _cmul(cr, ci, r_re, r_im)
        stacked = jnp.concatenate([rr_re, rr_im], axis=0)
        lm = sgn * cpp_re[d] * bx_ref[d] - cpp_im[d] * by_ref[d]
        kall = jnp.dot(lm, stacked, preferred_element_type=F32, precision=lax.Precision.HIGHEST)
        for s in range(CHUNK):
            if d == 0:
                shift = SSM_GROUP * s
                keep = lane16 >= SSM_GROUP * s
            else:
                shift = (SSM_GROUP * (s + 1)) % tw
                keep = lane16 < SSM_GROUP * (s + 1)
            rolled = pltpu.roll(kall, shift, axis=1) if shift else kall
            blk = jnp.where(keep, rolled, 0.0)
            if d == 0:
                mi_blocks.append(blk)
            else:
                mi_ref[SSM_GROUP * s:SSM_GROUP * (s + 1), :] = (mi_blocks[s] + blk).astype(mi_ref.dtype)

    a_re = afb_re[...]
    a_im = afb_im[...]
    cf_re = cfb_re[...]
    cf_im = cfb_im[...]
    b_re = bfb_re[...]
    b_im = bfb_im[...]
    powers = []
    qr, qi = jnp.ones((1, 2 * p), F32), jnp.zeros((1, 2 * p), F32)
    for k in range(CHUNK + 1):
        powers.append((qr, qi))
        if k < CHUNK:
            qr, qi = _cmul(qr, qi, a_re, a_im)
    for s in range(CHUNK):
        g_re = jnp.where(fwd_lanes, powers[CHUNK - 1 - s][0], powers[s][0])
        g_im = jnp.where(fwd_lanes, powers[CHUNK - 1 - s][1], powers[s][1])
        g_re, g_im = _cmul(g_re, g_im, cf_re, cf_im)
        rows = slice(SSM_GROUP * s, SSM_GROUP * (s + 1))
        ma_ref[rows, 0:2 * p] = (g_re * b_re - g_im * b_im).astype(ma_ref.dtype)
        ma_ref[rows, 2 * p:4 * p] = (g_re * b_im + g_im * b_re).astype(ma_ref.dtype)
    a16_ref[0:1, :] = powers[CHUNK][0]
    a16_ref[1:2, :] = powers[CHUNK][1]


def s5_matrices(abar_re, abar_im, coef_re, coef_im, b_re, b_im, c_re, c_im):
    depth, _, g, p = abar_re.shape
    hg = b_re.shape[-1]
    tw = CHUNK * hg
    t0213 = lambda t: t.transpose(0, 2, 1, 3)
    abc = [t0213(t)[..., None] for t in (abar_re, abar_im)]
    cex = [t.transpose(0, 2, 1, 4, 3) for t in (c_re, c_im)]
    fb = lambda t: t0213(t).reshape(depth, g, 1, 2 * p)
    afb = [fb(t) for t in (abar_re, abar_im)]
    cfb = [fb(t) for t in (coef_re, coef_im)]
    bfb = [t.transpose(0, 2, 4, 1, 3).reshape(depth, g, hg, 2 * p) for t in (b_re, b_im)]
    cpp = [t0213(jnp.concatenate([t, t], axis=-1))[:, :, :, None, :] for t in (coef_re, coef_im)]
    bt_re = b_re.transpose(0, 2, 1, 4, 3)
    bt_im = b_im.transpose(0, 2, 1, 4, 3)
    bx = jnp.concatenate([bt_re, bt_im], axis=-1)
    by = jnp.concatenate([bt_im, bt_re], axis=-1)
    ins = [*abc, *cex, *afb, *cfb, *bfb, *cpp, bx, by]

    def spec(t):
        blk = (None, None) + t.shape[2:]
        nz = len(t.shape) - 2
        return pl.BlockSpec(blk, lambda l, gi: (l, gi) + (0,) * nz)

    def ospec(r, c):
        return pl.BlockSpec((None, None, r, c), lambda l, gi: (l, gi, 0, 0))

    return pl.pallas_call(
        _s5_mats_kernel,
        grid=(depth, g),
        in_specs=[spec(t) for t in ins],
        out_specs=[ospec(tw, 4 * p), ospec(tw, tw), ospec(8 * p, tw), ospec(2, 2 * p)],
        out_shape=[
            jax.ShapeDtypeStruct((depth, g, tw, 4 * p), BF16),
            jax.ShapeDtypeStruct((depth, g, tw, tw), BF16),
            jax.ShapeDtypeStruct((depth, g, 8 * p, tw), BF16),
            jax.ShapeDtypeStruct((depth, g, 2, 2 * p), F32),
        ],
        compiler_params=_cparams(("arbitrary", "arbitrary"), "s5_mats"),
    )(*ins)


GROUP_BLOCK = LANE // SSM_GROUP


def lane_swap_matrix():
    idx = jnp.arange(GROUP_BLOCK * LANE)
    a, b, c = idx // LANE, (idx // SSM_GROUP) % GROUP_BLOCK, idx % SSM_GROUP
    dst = b * LANE + a * SSM_GROUP + c
    return (dst[:, None] == idx[None, :]).astype(BF16)


def _s5_main_kernel(*refs, order_f, order_b, nb, lat_chunks, ctx_chunks):
    (u_ref, p_ref, ma_ref, mi_ref, mo_ref, a_ref, y_ref, ug_scr, yg_scr,
     xr_scr, xi_scr, fr_scr, fi_scr, br_scr, bi_scr) = refs
    p2 = 2 * SSM_STATE
    half = CHUNK // 2
    rows = u_ref.shape[0] // CHUNK
    n_sub = xr_scr.shape[0]

    for th in range(2):
        slab = jnp.concatenate(
            [u_ref[pl.ds(half * th + tl, rows, stride=CHUNK), :].astype(BF16) for tl in range(half)], axis=1)
        perm = jnp.dot(slab, p_ref[...], preferred_element_type=F32).astype(BF16)
        for g in range(GROUP_BLOCK):
            ug_scr[g, :, th * LANE:(th + 1) * LANE] = perm[:, g * LANE:(g + 1) * LANE]

    segments = ((0, lat_chunks), (nb * lat_chunks, ctx_chunks))

    def to_chunk_major(dst_ref, k, val):
        for r0, n in segments:
            for b in range(nb):
                dst_ref[k, pl.ds(r0 + b, n, stride=nb), :] = val[r0 + b * n:r0 + (b + 1) * n, :]

    def to_batch_major(src_ref, k):
        return jnp.concatenate([src_ref[k, pl.ds(r0 + b, n, stride=nb), :]
                                for r0, n in segments for b in range(nb)], axis=0)

    def rows_of(j):
        return slice(nb * j, nb * j + nb)

    fwd_lanes = lax.broadcasted_iota(jnp.int32, (nb, p2), 1) < SSM_STATE
    for g0 in range(0, GROUP_BLOCK, n_sub):
        for k in range(n_sub):
            x = jnp.dot(ug_scr[g0 + k], ma_ref[g0 + k], preferred_element_type=F32)
            to_chunk_major(xr_scr, k, x[:, 0:p2])
            to_chunk_major(xi_scr, k, x[:, p2:2 * p2])

        a_re = [a_ref[g0 + k, 0:1, :] for k in range(n_sub)]
        a_im = [a_ref[g0 + k, 1:2, :] for k in range(n_sub)]
        s_re = [jnp.zeros((nb, p2), F32) for _ in range(n_sub)]
        s_im = [jnp.zeros((nb, p2), F32) for _ in range(n_sub)]
        for jf, jb in zip(order_f, order_b):
            rf = rows_of(jf)
            rb = rows_of(jb)
            for k in range(n_sub):
                fr_scr[k, rf, :] = s_re[k]
                fi_scr[k, rf, :] = s_im[k]
                br_scr[k, rb, :] = s_re[k]
                bi_scr[k, rb, :] = s_im[k]
                xr = jnp.where(fwd_lanes, xr_scr[k, rf, :], xr_scr[k, rb, :])
                xi = jnp.where(fwd_lanes, xi_scr[k, rf, :], xi_scr[k, rb, :])
                s_re[k], s_im[k] = (a_re[k] * s_re[k] - a_im[k] * s_im[k] + xr,
                                    a_re[k] * s_im[k] + a_im[k] * s_re[k] + xi)

        for k in range(n_sub):
            g = g0 + k
            y = jnp.dot(ug_scr[g], mi_ref[g], preferred_element_type=F32)
            sp_f = jnp.concatenate([to_batch_major(fr_scr, k), to_batch_major(fi_scr, k)], axis=1).astype(BF16)
            sp_b = jnp.concatenate([to_batch_major(br_scr, k), to_batch_major(bi_scr, k)], axis=1).astype(BF16)
            y = y + jnp.dot(sp_f, mo_ref[g, 0:2 * p2, :], preferred_element_type=F32)
            y = y + jnp.dot(sp_b, mo_ref[g, 2 * p2:4 * p2, :], preferred_element_type=F32)
            yg_scr[g] = y.astype(BF16)

    for th in range(2):
        slab = jnp.concatenate([yg_scr[g, :, th * LANE:(th + 1) * LANE] for g in range(GROUP_BLOCK)], axis=1)
        perm = jnp.dot(slab, p_ref[...], preferred_element_type=F32)
        for tl in range(half):
            y_ref[pl.ds(half * th + tl, rows, stride=CHUNK), :] = perm[:, tl * LANE:(tl + 1) * LANE]


def s5_chunked(u, ma, mi, mo, a16, layer, *, lat_chunks, ctx_chunks, nb):
    r_all = u.shape[0]
    rows = r_all // CHUNK
    tw = CHUNK * SSM_GROUP
    j_tot = lat_chunks + ctx_chunks
    order_f = tuple(range(lat_chunks, j_tot)) + tuple(range(lat_chunks))
    order_b = tuple(range(j_tot - 1, lat_chunks - 1, -1)) + tuple(range(lat_chunks - 1, -1, -1))
    p4 = 4 * SSM_STATE
    n_gb = SSM_GROUPS // GROUP_BLOCK
    n_sub = GROUP_BLOCK // 2

    def wspec(r, c):
        return pl.BlockSpec((None, GROUP_BLOCK, r, c), lambda gb: (layer, gb, 0, 0))

    return pl.pallas_call(
        functools.partial(_s5_main_kernel, order_f=order_f, order_b=order_b, nb=nb,
                          lat_chunks=lat_chunks, ctx_chunks=ctx_chunks),
        grid=(n_gb,),
        in_specs=[
            pl.BlockSpec((r_all, LANE), lambda gb: (0, gb)),
            pl.BlockSpec((GROUP_BLOCK * LANE, GROUP_BLOCK * LANE), lambda gb: (0, 0)),
            wspec(tw, p4), wspec(tw, tw), wspec(2 * p4, tw), wspec(2, 2 * SSM_STATE),
        ],
        out_specs=pl.BlockSpec((r_all, LANE), lambda gb: (0, gb)),
        out_shape=jax.ShapeDtypeStruct((r_all, D_SSM), F32),
        scratch_shapes=[
            pltpu.VMEM((GROUP_BLOCK, rows, tw), BF16),
            pltpu.VMEM((GROUP_BLOCK, rows, tw), BF16),
        ] + [pltpu.VMEM((n_sub, rows, 2 * SSM_STATE), F32)] * 6,
        compiler_params=_cparams(("arbitrary",), "s5_main"),
    )(u, lane_swap_matrix(), ma, mi, mo, a16)


def _s5_out_kernel(y_ref, u_ref, d_ref, w_ref, b_ref, gn_ref, o_ref):
    y = y_ref[...].astype(F32) + d_ref[...] * u_ref[...].astype(F32)
    g = jax.nn.gelu(y)
    z = jnp.dot(g.astype(BF16), w_ref[...], preferred_element_type=F32) + b_ref[...]
    o_ref[...] = _rms(g * jax.nn.sigmoid(z), gn_ref[...]).astype(o_ref.dtype)


def s5_output(y, u, ssm_d, w_glu, b_glu, gn, *, rows, tm):
    ch = D_SSM
    vec = lambda: pl.BlockSpec((1, ch), lambda i: (0, 0))
    return pl.pallas_call(
        _s5_out_kernel,
        grid=(rows // tm,),
        in_specs=[
            pl.BlockSpec((tm, ch), lambda i: (i, 0)),
            pl.BlockSpec((tm, ch), lambda i: (i, 0)),
            vec(),
            pl.BlockSpec((ch, ch), lambda i: (0, 0)),
            vec(), vec(),
        ],
        out_specs=pl.BlockSpec((tm, ch), lambda i: (i, 0)),
        out_shape=jax.ShapeDtypeStruct((rows, ch), BF16),
        compiler_params=_cparams(("arbitrary",), "s5_out"),
    )(y, u, ssm_d.reshape(1, ch), w_glu, b_glu.reshape(1, ch), gn.reshape(1, ch))


def _final_norm_kernel(x_ref, g_ref, o_ref):
    o_ref[...] = _rms(x_ref[...], g_ref[...])


def final_rms_norm(x, g, *, rows, tm):
    d = x.shape[1]
    return pl.pallas_call(
        _final_norm_kernel,
        grid=(rows // tm,),
        in_specs=[pl.BlockSpec((tm, d), lambda i: (i, 0)), pl.BlockSpec((1, d), lambda i: (0, 0))],
        out_specs=pl.BlockSpec((tm, d), lambda i: (i, 0)),
        out_shape=jax.ShapeDtypeStruct((rows, d), F32),
        compiler_params=_cparams(("arbitrary",), "final_norm"),
    )(x, g.reshape(1, d))


def _rope_partner_perm():
    idx = []
    for i in range(QK_ROPE):
        idx.append(i + 16 if (i % 32) < 16 else i - 16)
    return jnp.asarray(idx, jnp.int32)


def rope_tables(seq, tab_tile):
    rows = seq // GRID_W
    row = jnp.repeat(jnp.arange(rows, dtype=F32), GRID_W)
    col = jnp.tile(jnp.arange(GRID_W, dtype=F32), rows)
    n_freq = QK_ROPE // 4
    inv = ROPE_BASE ** (-jnp.arange(n_freq, dtype=F32) / n_freq)
    ar = row[:, None] * inv
    ac = col[:, None] * inv
    cos = jnp.concatenate([jnp.cos(ar), jnp.cos(ar), jnp.cos(ac), jnp.cos(ac)], axis=1)
    sin = jnp.concatenate([-jnp.sin(ar), jnp.sin(ar), -jnp.sin(ac), jnp.sin(ac)], axis=1)
    cos = jnp.concatenate([cos, jnp.ones((tab_tile, QK_ROPE), F32)], axis=0)
    sin = jnp.concatenate([sin, jnp.zeros((tab_tile, QK_ROPE), F32)], axis=0)
    pad = jnp.zeros((seq + tab_tile, LANE - QK_ROPE), F32)
    return jnp.concatenate([cos, pad], axis=1), jnp.concatenate([sin, pad], axis=1)


def prep_w_in(w_in):
    assert COL_U == COL_CQ + Q_RANK
    s = [COL_U, COL_U + KV_RANK, COL_U + KV_RANK + QK_ROPE, COL_U + KV_RANK + QK_ROPE + D_SSM]
    ckv, kr, u = [w_in[..., s[i]:s[i + 1]] for i in range(3)]
    krp = kr[..., _rope_partner_perm()]
    pad = jnp.zeros(w_in.shape[:-1] + (N_PROJ - (COL_KR + LANE),), w_in.dtype)
    tail = jnp.concatenate([u, ckv, kr, krp, pad], axis=-1).astype(BF16)
    return w_in.astype(BF16), tail


def prep_w_uq(w_uq):
    r = w_uq.shape[0]
    w = w_uq.reshape(r, MLA_HEADS, QK_NOPE + QK_ROPE)
    rope = w[:, :, QK_NOPE:]
    w = jnp.concatenate([w, rope[:, :, _rope_partner_perm()]], axis=-1)
    return w.reshape(r, MLA_HEADS * QK_PAD).astype(BF16)


def prep_w_ukv(w_ukv):
    r = w_ukv.shape[0]
    w = w_ukv.reshape(r, MLA_HEADS, QK_NOPE + V_HEAD)
    kn = w[:, :, :QK_NOPE].reshape(r, MLA_HEADS * QK_NOPE)
    v = w[:, :, QK_NOPE:].reshape(r, MLA_HEADS * V_HEAD)
    return jnp.concatenate([kn, v], axis=1).astype(BF16)


def kernel(x, c, ctx, c_ctx, w_ada, b_ada, norm1_g, norm2_g, w_in, conv_w, mla_q_norm, w_uq, mla_kv_norm, w_ukv, ssm_a_re, ssm_a_im, ssm_log_dt, ssm_b_re, ssm_b_im, ssm_c_re, ssm_c_im, ssm_d, w_glu, b_glu, mix_norm, w_o, w_gate, w_up, w_down, final_norm):
    nb, seq, d = x.shape
    ctx_len = ctx.shape[1]
    depth = w_ada.shape[0]
    r_lat = nb * seq
    r_ctx = nb * ctx_len
    r_all = r_lat + r_ctx
    assert seq % TM == 0 and r_ctx == TM and seq % CHUNK == 0 and ctx_len % CHUNK == 0
    tiles_per_seq = seq // TM
    lat_tiles_m = r_lat // TM
    tp = TILE_NARROW_ROWS
    tf = TILE_MERGE_ROWS
    assert nb + 1 <= SUBLANE and seq % tf == 0 and r_ctx % tf == 0 and seq % tp == 0 and r_ctx % tp == 0

    xs = jnp.concatenate([x.reshape(r_lat, d), ctx.reshape(r_ctx, d)], axis=0)

    cvec = jnp.concatenate([c, c_ctx[None, :], jnp.zeros((SUBLANE - nb - 1, d), F32)], axis=0)
    mod = ada_modulation(cvec, w_ada, b_ada)
    mod4 = mod.reshape(depth, SUBLANE, 1, 6 * d)

    cos_t, sin_t = rope_tables(seq, tp)
    abar_re, abar_im, coef_re, coef_im = s5_discretise(ssm_a_re, ssm_a_im, ssm_log_dt)
    ma, mi, mo, a16 = s5_matrices(abar_re, abar_im, coef_re, coef_im,
                                  ssm_b_re, ssm_b_im, ssm_c_re, ssm_c_im)
    w_in_main, w_in_tail = prep_w_in(w_in)
    w_o_b = w_o.astype(BF16)
    lat_chunks = seq // CHUNK
    ctx_chunks = ctx_len // CHUNK

    for i in range(depth):
        ctx_out = i < depth - 1
        m_tiles = lat_tiles_m + (1 if ctx_out else 0)
        rows_out = r_all if ctx_out else r_lat

        proj, u32 = input_projection(xs, norm1_g[i], mod4, i, w_in_main, w_in_tail,
                                     tiles_per_seq=tiles_per_seq, n_batch=nb)

        y_conv = conv_mixer(proj, conv_w[i], mix_norm[i, :D_CONV], rows=rows_out, tile=ctx_len,
                            tiles_per_seq=seq // ctx_len, lat_tiles=r_lat // ctx_len)

        q, k, v = qkv_projection(proj, mla_q_norm[i], mla_kv_norm[i], prep_w_uq(w_uq[i]),
                                 prep_w_ukv(w_ukv[i]), cos_t, sin_t, rows=r_all, tm=tp,
                                 lat_tiles=r_lat // tp, tiles_per_seq=seq // tp)
        y_att = latent_attention(q, k, v, n_batch=nb, seq=seq, ctx_len=ctx_len, tq=2 * ctx_len,
                                 ctx_queries=ctx_out, rows_out=rows_out)

        y_tok = s5_chunked(u32, ma, mi, mo, a16, i, lat_chunks=lat_chunks, ctx_chunks=ctx_chunks, nb=nb)
        y_ssm = s5_output(y_tok, u32, ssm_d[i], w_glu[i].astype(BF16), b_glu[i],
                          mix_norm[i, D_CONV + D_ATTN:], rows=rows_out, tm=tp)

        x2, h2 = merge_projection(y_conv, y_att, y_ssm, mix_norm[i, D_CONV:D_CONV + D_ATTN], w_o_b, xs,
                                  norm2_g[i], mod4, i,
                                  m_tiles=rows_out // tf, tiles_per_seq=seq // tf, n_batch=nb, tm=tf)
        hidden = ffn_up(h2, w_gate, w_up, i, m_tiles=m_tiles, tm=TM, tn=TILE_FFN_COLS)
        xs = ffn_down(hidden, w_down, x2, mod4, i, m_tiles=rows_out // tf, tiles_per_seq=seq // tf,
                      n_batch=nb, tm=tf, tn=TILE_FFN_COLS)

    out = final_rms_norm(xs, final_norm, rows=r_lat, tm=tp)
    return out.reshape(nb, seq, d)
```
